```python
import math, functools
import jax, jax.numpy as jnp
from jax import lax
import numpy as np

D_MODEL = 1024
BATCH = 32
SEQ = 256
DEPTH = 2
DEC_BATCH = 8
DEC_SEQ = 2048
PAST_LEN = 256

GRID_W = 64
HEAD_DIM = 64
N_HEADS = 8
N_KV_HEADS = 2
GQA_GROUP = N_HEADS // N_KV_HEADS
ATTN_WIDTH = N_HEADS * HEAD_DIM
KV_WIDTH = N_KV_HEADS * HEAD_DIM
WINDOW = 128
BLOCK = 128
ATTN_SCALE = HEAD_DIM ** -0.5
ROPE_BASE = 10000.0
NEG_INF = -1e30
FNET_HEADS = 4
FNET_HEAD_DIM = 64
FNET_WIDTH = FNET_HEADS * FNET_HEAD_DIM
SSM_WIDTH = 256
SSM_GROUP = 16
SSM_GROUPS = SSM_WIDTH // SSM_GROUP
SSM_STATE = 64
MIX_WIDTH = ATTN_WIDTH + FNET_WIDTH + SSM_WIDTH
Q_END = ATTN_WIDTH
K_END = Q_END + KV_WIDTH
V_END = K_END + KV_WIDTH
F_END = V_END + FNET_WIDTH
S_END = F_END + SSM_WIDTH
IN_WIDTH = S_END
D_FF = 2816
N_MOD = 9
NORM_EPS = 1e-6

kernel_name = 'hymba_style_fnet_s5_diffusion_step'


def rms_norm(x, g):
    xf = x.astype(jnp.float32)
    y = xf * lax.rsqrt(jnp.mean(xf * xf, axis=-1, keepdims=True) + NORM_EPS)
    return (y * g.astype(jnp.float32)).astype(x.dtype)


def modulate(h, shift, scale):
    return h * (1.0 + scale[..., None, :]) + shift[..., None, :]


def swiglu(h, w1, w3, w2):
    return (jax.nn.silu(h @ w1) * (h @ w3)) @ w2


def axial_rope(x):
    L = x.shape[1]
    rows = L // GRID_W
    row = jnp.repeat(jnp.arange(rows, dtype=jnp.float32), GRID_W)
    col = jnp.tile(jnp.arange(GRID_W, dtype=jnp.float32), rows)
    half = HEAD_DIM // 2
    quarter = half // 2
    inv_freq = 1.0 / (ROPE_BASE ** (jnp.arange(quarter, dtype=jnp.float32) * 2.0 / half))
    xf = x.astype(jnp.float32)

    def rotate(xh, pos):
        ang = pos[:, None] * inv_freq[None, :]
        cos = jnp.cos(ang)[None, :, None, :]
        sin = jnp.sin(ang)[None, :, None, :]
        x1, x2 = xh[..., :quarter], xh[..., quarter:]
        return jnp.concatenate([x1 * cos - x2 * sin, x1 * sin + x2 * cos], axis=-1)

    out = jnp.concatenate([rotate(xf[..., :half], row), rotate(xf[..., half:], col)], axis=-1)
    return out.astype(x.dtype)


def attn_heads(u, q_g, k_g):
    B, L, _ = u.shape
    q = u[..., :Q_END].reshape(B, L, N_HEADS, HEAD_DIM)
    k = u[..., Q_END:K_END].reshape(B, L, N_KV_HEADS, HEAD_DIM)
    v = u[..., K_END:V_END].reshape(B, L, N_KV_HEADS, HEAD_DIM)
    return rms_norm(q, q_g), rms_norm(k, k_g), v


def sink_softmax(s, sink):
    sk = jnp.broadcast_to(sink.astype(jnp.float32).reshape(N_KV_HEADS, GQA_GROUP, 1, 1), s.shape[:-1] + (1,))
    return jax.nn.softmax(jnp.concatenate([sk, s], axis=-1), axis=-1)[..., 1:]


def context_attention(q, k, v, sink):
    B, C = q.shape[:2]
    nb = C // BLOCK
    qb = jnp.moveaxis(q.reshape(B, nb, BLOCK, N_KV_HEADS, GQA_GROUP, HEAD_DIM), 1, 0)
    kf = k.astype(jnp.float32)
    vf = v.astype(jnp.float32)

    def one_block(qblk):
        s = jnp.einsum('bqhgd,bkhd->bhgqk', qblk.astype(jnp.float32), kf) * ATTN_SCALE
        p = sink_softmax(s, sink)
        return jnp.einsum('bhgqk,bkhd->bqhgd', p, vf)

    o = lax.map(one_block, qb)
    return jnp.moveaxis(o, 0, 1).reshape(B, C, ATTN_WIDTH).astype(q.dtype)


def latent_attention(q, k, v, k_ctx, v_ctx, sink):
    B, L = q.shape[:2]
    nb = L // BLOCK
    span = BLOCK + 2 * WINDOW
    qb = jnp.moveaxis(q.reshape(B, nb, BLOCK, N_KV_HEADS, GQA_GROUP, HEAD_DIM), 1, 0)
    pad = ((0, 0), (WINDOW, WINDOW), (0, 0), (0, 0))
    kp = jnp.pad(k.astype(jnp.float32), pad)
    vp = jnp.pad(v.astype(jnp.float32), pad)
    kc = k_ctx.astype(jnp.float32)
    vc = v_ctx.astype(jnp.float32)
    C = kc.shape[1]
    rel = (jnp.arange(span)[None, :] - WINDOW) - jnp.arange(BLOCK)[:, None]
    in_window = jnp.abs(rel) <= WINDOW

    def one_block(args):
        n, qblk = args
        start = n * BLOCK
        kw = lax.dynamic_slice_in_dim(kp, start, span, axis=1)
        vw = lax.dynamic_slice_in_dim(vp, start, span, axis=1)
        kpos = start - WINDOW + jnp.arange(span)
        valid = in_window & ((kpos >= 0) & (kpos < L))[None, :]
        qf = qblk.astype(jnp.float32)
        s_loc = jnp.einsum('bqhgd,bkhd->bhgqk', qf, kw) * ATTN_SCALE
        s_loc = jnp.where(valid, s_loc, NEG_INF)
        s_ctx = jnp.einsum('bqhgd,bchd->bhgqc', qf, kc) * ATTN_SCALE
        p = sink_softmax(jnp.concatenate([s_ctx, s_loc], axis=-1), sink)
        return (jnp.einsum('bhgqc,bchd->bqhgd', p[..., :C], vc)
                + jnp.einsum('bhgqk,bkhd->bqhgd', p[..., C:], vw))

    o = lax.map(one_block, (jnp.arange(nb), qb))
    return jnp.moveaxis(o, 0, 1).reshape(B, L, ATTN_WIDTH).astype(q.dtype)


def fourier_mix(u, w_f):
    B, L, _ = u.shape
    uf = u.astype(jnp.float32).reshape(B, L, FNET_HEADS, FNET_HEAD_DIM)
    z = jnp.fft.fft2(uf, axes=(1, 3), norm='ortho').real
    return z.reshape(B, L, FNET_WIDTH).astype(u.dtype) @ w_f


def s5_scan(bu, lam_bar, h0):
    if h0 is not None:
        bu = bu.at[:, 0].add(lam_bar * h0)
    a = jnp.broadcast_to(lam_bar, bu.shape)

    def combine(e1, e2):
        return e1[0] * e2[0], e2[0] * e1[1] + e2[1]

    return lax.associative_scan(combine, (a, bu), axis=1)[1]


def s5_mixer(u, lp, h0, return_state):
    B, L, _ = u.shape
    ug = u.astype(jnp.float32).reshape(B, L, SSM_GROUPS, SSM_GROUP)
    y = lp['ssm_d'].astype(jnp.float32) * ug
    finals_re, finals_im = [], []
    for r in range(2):
        lam = lax.complex(lp['ssm_lambda_re'][r].astype(jnp.float32), lp['ssm_lambda_im'][r].astype(jnp.float32))
        bmat = lax.complex(lp['ssm_b_re'][r].astype(jnp.float32), lp['ssm_b_im'][r].astype(jnp.float32))
        step = jnp.exp(lp['ssm_log_step'][r].astype(jnp.float32))[:, None]
        lam_bar = jnp.exp(lam * step)
        b_bar = ((lam_bar - 1.0) / lam)[..., None] * bmat
        bu = jnp.einsum('blgh,gph->blgp', ug, b_bar)
        init = None if h0 is None else lax.complex(h0[0][:, r].astype(jnp.float32), h0[1][:, r].astype(jnp.float32))
        if r == 1:
            bu = jnp.flip(bu, axis=1)
        s = s5_scan(bu, lam_bar, init)
        if return_state:
            finals_re.append(s[:, -1].real)
            finals_im.append(s[:, -1].imag)
        if r == 1:
            s = jnp.flip(s, axis=1)
        y = (y + jnp.einsum('ghp,blgp->blgh', lp['ssm_c_re'][r].astype(jnp.float32), s.real)
             - jnp.einsum('ghp,blgp->blgh', lp['ssm_c_im'][r].astype(jnp.float32), s.imag))
    y = jax.nn.gelu(y.reshape(B, L, SSM_WIDTH))
    gl = y @ lp['ssm_w_glu'].astype(jnp.float32)
    out = gl[..., :SSM_WIDTH] * jax.nn.sigmoid(gl[..., SSM_WIDTH:])
    state = (jnp.stack(finals_re, axis=1), jnp.stack(finals_im, axis=1)) if return_state else None
    return out.astype(u.dtype), state


def context_mix(u, lp):
    q, k, v = attn_heads(u, lp['q_norm_g'], lp['k_norm_g'])
    a = context_attention(q, k, v, lp['attn_sink'])
    f = fourier_mix(u[..., V_END:F_END], lp['w_fnet'])
    s, (st_re, st_im) = s5_mixer(u[..., F_END:S_END], lp, None, True)
    return jnp.concatenate([a, f, s], axis=-1), (k, v, st_re, st_im)


def latent_mix(u, lp, k_ctx, v_ctx, h0_re, h0_im):
    q, k, v = attn_heads(u, lp['q_norm_g'], lp['k_norm_g'])
    q, k = axial_rope(q), axial_rope(k)
    a = latent_attention(q, k, v, k_ctx, v_ctx, lp['attn_sink'])
    f = fourier_mix(u[..., V_END:F_END], lp['w_fnet'])
    s, _ = s5_mixer(u[..., F_END:S_END], lp, (h0_re, h0_im), False)
    return jnp.concatenate([a, f, s], axis=-1), None


def layer_forward(x, cond, lp, mix_fn):
    m = jax.nn.silu(cond) @ lp['w_mod'] + lp['b_mod']
    sh1, sc1, g1, sh2, sc2, g2, sh3, sc3, g3 = jnp.split(m, N_MOD, axis=-1)
    h = modulate(rms_norm(x, lp['norm_g'][0]), sh1, sc1)
    x = x + 0.5 * g1[..., None, :] * swiglu(h, lp['ffn_w1'][0], lp['ffn_w3'][0], lp['ffn_w2'][0])
    h = modulate(rms_norm(x, lp['norm_g'][1]), sh2, sc2)
    mixed, extras = mix_fn(h @ lp['w_in'])
    x = x + g2[..., None, :] * (mixed @ lp['w_out'])
    h = modulate(rms_norm(x, lp['norm_g'][2]), sh3, sc3)
    x = x + 0.5 * g3[..., None, :] * swiglu(h, lp['ffn_w1'][1], lp['ffn_w3'][1], lp['ffn_w2'][1])
    return x, extras


def setup_inputs(seed: int = 0) -> dict:
    key = jax.random.key(seed)
    ks = jax.random.split(key, 29)

    def nrm(k, shape, s):
        return jax.random.normal(k, shape, jnp.float32) * s

    ssm_shape = (DEPTH, 2, SSM_GROUPS, SSM_STATE)
    n = jnp.arange(SSM_STATE, dtype=jnp.float32)
    return {
        'x_prompt': nrm(ks[0], (BATCH, SEQ, D_MODEL), 1.0),
        'x_sample': nrm(ks[1], (DEC_BATCH, DEC_SEQ, D_MODEL), 1.0),
        'cache_k': nrm(ks[2], (DEC_BATCH, DEPTH, PAST_LEN, N_KV_HEADS, HEAD_DIM), 1.0),
        'cache_v': nrm(ks[3], (DEC_BATCH, DEPTH, PAST_LEN, N_KV_HEADS, HEAD_DIM), 1.0),
        'state_ssm_re': nrm(ks[4], (DEC_BATCH, DEPTH, 2, SSM_GROUPS, SSM_STATE), 0.1),
        'state_ssm_im': nrm(ks[5], (DEC_BATCH, DEPTH, 2, SSM_GROUPS, SSM_STATE), 0.1),
        'c': nrm(ks[6], (DEC_BATCH, D_MODEL), 1.0),
        'c_ctx': nrm(ks[7], (D_MODEL,), 1.0),
        'w_mod': nrm(ks[8], (DEPTH, D_MODEL, N_MOD * D_MODEL), 0.5 * D_MODEL ** -0.5),
        'b_mod': nrm(ks[9], (DEPTH, N_MOD * D_MODEL), 0.01),
        'norm_g': 1.0 + nrm(ks[10], (DEPTH, 3, D_MODEL), 0.01),
        'ffn_w1': nrm(ks[11], (DEPTH, 2, D_MODEL, D_FF), D_MODEL ** -0.5),
        'ffn_w3': nrm(ks[12], (DEPTH, 2, D_MODEL, D_FF), D_MODEL ** -0.5),
        'ffn_w2': nrm(ks[13], (DEPTH, 2, D_FF, D_MODEL), D_FF ** -0.5),
        'w_in': nrm(ks[14], (DEPTH, D_MODEL, IN_WIDTH), D_MODEL ** -0.5),
        'w_out': nrm(ks[15], (DEPTH, MIX_WIDTH, D_MODEL), MIX_WIDTH ** -0.5),
        'q_norm_g': 1.0 + nrm(ks[16], (DEPTH, HEAD_DIM), 0.01),
        'k_norm_g': 1.0 + nrm(ks[17], (DEPTH, HEAD_DIM), 0.01),
        'attn_sink': nrm(ks[18], (DEPTH, N_HEADS), 0.5),
        'w_fnet': nrm(ks[19], (DEPTH, FNET_WIDTH, FNET_WIDTH), FNET_WIDTH ** -0.5),
        'ssm_lambda_re': -0.5 + nrm(ks[20], ssm_shape, 0.01),
        'ssm_lambda_im': math.pi * n + nrm(ks[21], ssm_shape, 0.01),
        'ssm_b_re': nrm(ks[22], (DEPTH, 2, SSM_GROUPS, SSM_STATE, SSM_GROUP), (2.0 * SSM_GROUP) ** -0.5),
        'ssm_b_im': nrm(ks[23], (DEPTH, 2, SSM_GROUPS, SSM_STATE, SSM_GROUP), (2.0 * SSM_GROUP) ** -0.5),
        'ssm_c_re': nrm(ks[24], (DEPTH, 2, SSM_GROUPS, SSM_GROUP, SSM_STATE), SSM_STATE ** -0.5),
        'ssm_c_im': nrm(ks[25], (DEPTH, 2, SSM_GROUPS, SSM_GROUP, SSM_STATE), SSM_STATE ** -0.5),
        'ssm_d': nrm(ks[26], (DEPTH, SSM_GROUPS, SSM_GROUP), 1.0),
        'ssm_log_step': jax.random.uniform(ks[27], (DEPTH, 2, SSM_GROUPS), jnp.float32, math.log(1e-3), math.log(1e-1)),
        'ssm_w_glu': nrm(ks[28], (DEPTH, SSM_WIDTH, 2 * SSM_WIDTH), SSM_WIDTH ** -0.5),
    }


def reference(x_prompt, x_sample, cache_k, cache_v, state_ssm_re, state_ssm_im, c, c_ctx,
              w_mod, b_mod, norm_g, ffn_w1, ffn_w3, ffn_w2, w_in, w_out, q_norm_g, k_norm_g,
              attn_sink, w_fnet, ssm_lambda_re, ssm_lambda_im, ssm_b_re, ssm_b_im, ssm_c_re,
              ssm_c_im, ssm_d, ssm_log_step, ssm_w_glu):
    y_prompt = x_prompt
    y_sample = x_sample
    ks, vs, sre, sim = [], [], [], []
    for l in range(DEPTH):
        lp = {
            'w_mod': w_mod[l], 'b_mod': b_mod[l], 'norm_g': norm_g[l],
            'ffn_w1': ffn_w1[l], 'ffn_w3': ffn_w3[l], 'ffn_w2': ffn_w2[l],
            'w_in': w_in[l], 'w_out': w_out[l],
            'q_norm_g': q_norm_g[l], 'k_norm_g': k_norm_g[l], 'attn_sink': attn_sink[l],
            'w_fnet': w_fnet[l],
            'ssm_lambda_re': ssm_lambda_re[l], 'ssm_lambda_im': ssm_lambda_im[l],
            'ssm_b_re': ssm_b_re[l], 'ssm_b_im': ssm_b_im[l],
            'ssm_c_re': ssm_c_re[l], 'ssm_c_im': ssm_c_im[l],
            'ssm_d': ssm_d[l], 'ssm_log_step': ssm_log_step[l], 'ssm_w_glu': ssm_w_glu[l],
        }
        y_prompt, (k_l, v_l, st_re, st_im) = layer_forward(
            y_prompt, c_ctx, lp, functools.partial(context_mix, lp=lp))
        ks.append(k_l)
        vs.append(v_l)
        sre.append(st_re)
        sim.append(st_im)
        y_sample, _ = layer_forward(
            y_sample, c, lp,
            functools.partial(latent_mix, lp=lp, k_ctx=cache_k[:, l], v_ctx=cache_v[:, l],
                              h0_re=state_ssm_re[:, l], h0_im=state_ssm_im[:, l]))
    new_cache_k = jnp.stack(ks, axis=1)
    new_cache_v = jnp.stack(vs, axis=1)
    new_state_ssm_re = jnp.stack(sre, axis=1)
    new_state_ssm_im = jnp.stack(sim, axis=1)
    return (y_prompt, y_sample, new_cache_k, new_cache_v, new_state_ssm_re, new_state_ssm_im)
```

```python
import functools
import math

import numpy as np
import jax
import jax.numpy as jnp
from jax import lax
from jax.experimental import pallas as pl
from jax.experimental.pallas import tpu as pltpu

F32 = jnp.float32
BF16 = jnp.bfloat16

D_MODEL = 1024
DEPTH = 2
GRID_W = 64
HEAD_DIM = 64
N_HEADS = 8
N_KV_HEADS = 2
ATTN_WIDTH = N_HEADS * HEAD_DIM
KV_WIDTH = N_KV_HEADS * HEAD_DIM
WINDOW = 128
BLOCK = 128
ATTN_SCALE = HEAD_DIM ** -0.5
ROPE_BASE = 10000.0
NEG_INF = -1e30
FNET_HEADS = 4
FNET_HEAD_DIM = 64
FNET_WIDTH = FNET_HEADS * FNET_HEAD_DIM
SSM_WIDTH = 256
SSM_GROUP = 16
SSM_GROUPS = SSM_WIDTH // SSM_GROUP
SSM_STATE = 64
SSM_LANES = SSM_GROUPS * SSM_STATE
MIX_WIDTH = ATTN_WIDTH + FNET_WIDTH + SSM_WIDTH
Q_END = ATTN_WIDTH
K_END = Q_END + KV_WIDTH
V_END = K_END + KV_WIDTH
F_END = V_END + FNET_WIDTH
S_END = F_END + SSM_WIDTH
IN_WIDTH = S_END
D_FF = 2816
N_MOD = 9
NORM_EPS = 1e-6

LANES = 128
SUBLANES = 8
VMEM_LIMIT_BYTES = 56 * 1024 * 1024
TOKEN_TILE = 256
SCAN_STEPS = 64
SCAN_LANE_CHUNK = 512
COND_ROWS = 16


def _params(*sem):
    return pltpu.CompilerParams(dimension_semantics=sem, vmem_limit_bytes=VMEM_LIMIT_BYTES)


def _dot(a, b):
    return jnp.dot(a, b, preferred_element_type=F32)


def _norm_mod(x, g, shift, scale):
    y = x * lax.rsqrt(jnp.mean(x * x, axis=-1, keepdims=True) + NORM_EPS)
    return (y * g) * (1.0 + scale) + shift


def _mod_kernel(c_ref, w_ref, b_ref, o_ref):
    c = c_ref[...]
    a = (c * jax.nn.sigmoid(c)).astype(BF16)
    o_ref[0] = _dot(a, w_ref[0].astype(BF16)) + b_ref[0]


def _mod_call(cond, w_mod, b_mod):
    tn = 1024
    return pl.pallas_call(
        _mod_kernel,
        out_shape=jax.ShapeDtypeStruct((DEPTH, COND_ROWS, N_MOD * D_MODEL), F32),
        grid=(DEPTH, N_MOD * D_MODEL // tn),
        in_specs=[pl.BlockSpec((COND_ROWS, D_MODEL), lambda l, j: (0, 0)),
                  pl.BlockSpec((1, D_MODEL, tn), lambda l, j: (l, 0, j)),
                  pl.BlockSpec((1, 1, tn), lambda l, j: (l, 0, j))],
        out_specs=pl.BlockSpec((1, COND_ROWS, tn), lambda l, j: (l, 0, j)),
        compiler_params=_params("arbitrary", "arbitrary"),
        name="cond_mod",
    )(cond, w_mod, b_mod.reshape(DEPTH, 1, N_MOD * D_MODEL))


def _ffn_kernel(x_ref, mod_ref, g_ref, w1_ref, w3_ref, w2_ref, o_ref, *, k0):
    x = x_ref[...]
    m = mod_ref[0]
    h = _norm_mod(x, g_ref[...], m[k0:k0 + 1], m[k0 + 1:k0 + 2]).astype(BF16)
    a = _dot(h, w1_ref[...])
    b = _dot(h, w3_ref[...])
    t = ((a * jax.nn.sigmoid(a)) * b).astype(BF16)
    y = _dot(t, w2_ref[...])
    o_ref[...] = x + (0.5 * m[k0 + 2:k0 + 3]) * y


def _const_spec(shape):
    nd = len(shape)
    return pl.BlockSpec(shape, lambda i: (0,) * nd, pipeline_mode=pl.Buffered(1))


def _ffn_call(x, mod, k0, g, w1, w3, w2, rows_per_group):
    n = x.shape[0]
    tm = TOKEN_TILE
    tiles_per_group = rows_per_group // tm
    return pl.pallas_call(
        functools.partial(_ffn_kernel, k0=k0),
        out_shape=jax.ShapeDtypeStruct((n, D_MODEL), F32),
        grid=(n // tm,),
        in_specs=[pl.BlockSpec((tm, D_MODEL), lambda i: (i, 0)),
                  pl.BlockSpec((1, N_MOD, D_MODEL), lambda i: (i // tiles_per_group, 0, 0)),
                  _const_spec((1, D_MODEL)),
                  _const_spec((D_MODEL, D_FF)),
                  _const_spec((D_MODEL, D_FF)),
                  _const_spec((D_FF, D_MODEL))],
        out_specs=pl.BlockSpec((tm, D_MODEL), lambda i: (i, 0)),
        compiler_params=_params("arbitrary"),
        name="half_ffn",
    )(x, mod, g, w1, w3, w2)


def _head_norm(z, e, g):
    z2 = z * z
    hi = z2.astype(BF16)
    lo = (z2 - hi.astype(F32)).astype(BF16)
    ms = _dot(hi, e) + _dot(lo, e)
    return (z * lax.rsqrt(ms + NORM_EPS)) * g


def _inproj_kernel(*refs, rope):
    if rope:
        (x_ref, mod_ref, g_ref, w_ref, qg_ref, kg_ref, e_ref, cos_ref, sa_ref, sb_ref,
         q_ref, k_ref, v_ref, f_ref, s_ref) = refs
    else:
        (x_ref, mod_ref, g_ref, w_ref, qg_ref, kg_ref, e_ref,
         q_ref, k_ref, v_ref, f_ref, s_ref) = refs
    m = mod_ref[0]
    h = _norm_mod(x_ref[...], g_ref[...], m[3:4], m[4:5]).astype(BF16)
    u = _dot(h, w_ref[...])
    q = _head_norm(u[:, :Q_END], e_ref[...], qg_ref[...])
    k = _head_norm(u[:, Q_END:K_END], e_ref[0:KV_WIDTH, 0:KV_WIDTH], kg_ref[...])
    if rope:
        cos, sa, sb = cos_ref[...], sa_ref[...], sb_ref[...]

        def rot(z):
            return z * cos + pltpu.roll(z, LANES - 16, 1) * sa + pltpu.roll(z, 16, 1) * sb

        q = jnp.concatenate([rot(q[:, j * LANES:(j + 1) * LANES])
                             for j in range(ATTN_WIDTH // LANES)], axis=1)
        k = rot(k)
    q_ref[...] = (q * ATTN_SCALE).astype(BF16)
    k_ref[...] = k
    v_ref[...] = u[:, K_END:V_END]
    f_ref[...] = u[:, V_END:F_END].astype(BF16)
    s_ref[...] = u[:, F_END:S_END]


def _inproj_call(x, mod, g, w_in, qg, kg, e, rope_tabs, rows_per_group, seq):
    n = x.shape[0]
    tm = TOKEN_TILE
    tiles_per_group = rows_per_group // tm
    tiles_per_seq = seq // tm
    in_specs = [pl.BlockSpec((tm, D_MODEL), lambda i: (i, 0)),
                pl.BlockSpec((1, N_MOD, D_MODEL), lambda i: (i // tiles_per_group, 0, 0)),
                _const_spec((1, D_MODEL)),
                _const_spec((D_MODEL, IN_WIDTH)),
                _const_spec((1, ATTN_WIDTH)),
                _const_spec((1, KV_WIDTH)),
                _const_spec((ATTN_WIDTH, ATTN_WIDTH))]
    args = [x, mod, g, w_in, qg, kg, e]
    if rope_tabs is not None:
        in_specs += [pl.BlockSpec((tm, LANES), lambda i: (i % tiles_per_seq, 0))] * 3
        args += list(rope_tabs)
    widths = (ATTN_WIDTH, KV_WIDTH, KV_WIDTH, FNET_WIDTH, SSM_WIDTH)
    dtypes = (BF16, F32, F32, BF16, F32)
    return pl.pallas_call(
        functools.partial(_inproj_kernel, rope=rope_tabs is not None),
        out_shape=[jax.ShapeDtypeStruct((n, w), d) for w, d in zip(widths, dtypes)],
        grid=(n // tm,),
        in_specs=in_specs,
        out_specs=[pl.BlockSpec((tm, w), lambda i: (i, 0)) for w in widths],
        compiler_params=_params("arbitrary"),
        name="in_proj",
    )(*args)


def _kv_variants(k, v):
    low = lax.broadcasted_iota(jnp.int32, k.shape, 1) < HEAD_DIM
    kr = pltpu.roll(k, HEAD_DIM, 1)
    vr = pltpu.roll(v, HEAD_DIM, 1)
    zero = jnp.zeros_like(k)
    ks = {(0, 0): jnp.where(low, k, zero), (0, 1): jnp.where(low, zero, kr),
          (1, 0): jnp.where(low, kr, zero), (1, 1): jnp.where(low, zero, k)}
    vs = {(0, 0): jnp.where(low, v, zero), (0, 1): jnp.where(low, zero, vr),
          (1, 0): jnp.where(low, vr, zero), (1, 1): jnp.where(low, zero, v)}
    return ({a: b.astype(BF16) for a, b in ks.items()}, {a: b.astype(BF16) for a, b in vs.items()})


def _attend(q_ref, o_ref, sink_ref, k, v, mask, nq):
    ks, vs = _kv_variants(k, v)
    for h in range(N_KV_HEADS):
        c0 = h * 2 * LANES
        qst = jnp.concatenate([q_ref[:, c0:c0 + LANES], q_ref[:, c0 + LANES:c0 + 2 * LANES]], axis=0)
        acc = None
        for p in range(2):
            j0 = 4 * h + p
            j1 = 4 * h + 2 + p
            sinkv = jnp.concatenate(
                [jnp.broadcast_to(sink_ref[j0:j0 + 1, 0:1], (nq, 1)),
                 jnp.broadcast_to(sink_ref[j1:j1 + 1, 0:1], (nq, 1))], axis=0)
            s = lax.dot_general(qst, ks[(h, p)], (((1,), (1,)), ((), ())),
                                preferred_element_type=F32)
            if mask is not None:
                s = jnp.where(mask, s, NEG_INF)
            m = jnp.maximum(jnp.max(s, axis=-1, keepdims=True), sinkv)
            e = jnp.exp(s - m)
            d = jnp.sum(e, axis=-1, keepdims=True) + jnp.exp(sinkv - m)
            o = _dot(e.astype(BF16), vs[(h, p)]) * (1.0 / d)
            acc = o if acc is None else acc + o
        o_ref[:, c0:c0 + LANES] = acc[0:nq].astype(o_ref.dtype)
        o_ref[:, c0 + LANES:c0 + 2 * LANES] = acc[nq:2 * nq].astype(o_ref.dtype)


def _ctx_attn_kernel(q_ref, k_ref, v_ref, sink_ref, o_ref, *, seq):
    _attend(q_ref, o_ref, sink_ref, k_ref[...], v_ref[...], None, seq)


def _ctx_attn_call(q, k, v, sink, batch, seq):
    return pl.pallas_call(
        functools.partial(_ctx_attn_kernel, seq=seq),
        out_shape=jax.ShapeDtypeStruct((batch * seq, ATTN_WIDTH), BF16),
        grid=(batch,),
        in_specs=[pl.BlockSpec((seq, ATTN_WIDTH), lambda b: (b, 0)),
                  pl.BlockSpec((seq, KV_WIDTH), lambda b: (b, 0)),
                  pl.BlockSpec((seq, KV_WIDTH), lambda b: (b, 0)),
                  pl.BlockSpec((SUBLANES, LANES), lambda b: (0, 0))],
        out_specs=pl.BlockSpec((seq, ATTN_WIDTH), lambda b: (b, 0)),
        compiler_params=_params("arbitrary"),
        name="ctx_attn",
    )(q, k, v, sink)


def _lat_attn_kernel(q_ref, kp_ref, vp_ref, kc_ref, vc_ref, sink_ref, o_ref, *, seq, past):
    n = pl.program_id(1)
    start = pl.multiple_of(n * BLOCK, BLOCK)
    span = BLOCK + 2 * WINDOW
    k = jnp.concatenate([kc_ref[0], kp_ref[0, pl.ds(start, span), :]], axis=0)
    v = jnp.concatenate([vc_ref[0], vp_ref[0, pl.ds(start, span), :]], axis=0)
    shape = (2 * BLOCK, past + span)
    qi = lax.broadcasted_iota(jnp.int32, shape, 0) % BLOCK
    col = lax.broadcasted_iota(jnp.int32, shape, 1)
    jj = col - past
    rel = jj - WINDOW - qi
    kpos = start - WINDOW + jj
    mask = (col < past) | ((jnp.abs(rel) <= WINDOW) & (kpos >= 0) & (kpos < seq))
    _attend(q_ref, o_ref, sink_ref, k, v, mask, BLOCK)


def _lat_attn_call(q, kpad, vpad, kctx, vctx, sink, batch, seq):
    past = kctx.shape[1]
    nb = seq // BLOCK
    padded = seq + 2 * WINDOW
    return pl.pallas_call(
        functools.partial(_lat_attn_kernel, seq=seq, past=past),
        out_shape=jax.ShapeDtypeStruct((batch * seq, ATTN_WIDTH), BF16),
        grid=(batch, nb),
        in_specs=[pl.BlockSpec((BLOCK, ATTN_WIDTH), lambda b, n: (b * nb + n, 0)),
                  pl.BlockSpec((1, padded, KV_WIDTH), lambda b, n: (b, 0, 0)),
                  pl.BlockSpec((1, padded, KV_WIDTH), lambda b, n: (b, 0, 0)),
                  pl.BlockSpec((1, past, KV_WIDTH), lambda b, n: (b, 0, 0)),
                  pl.BlockSpec((1, past, KV_WIDTH), lambda b, n: (b, 0, 0)),
                  pl.BlockSpec((SUBLANES, LANES), lambda b, n: (0, 0))],
        out_specs=pl.BlockSpec((BLOCK, ATTN_WIDTH), lambda b, n: (b * nb + n, 0)),
        compiler_params=_params("arbitrary", "arbitrary"),
        name="lat_attn",
    )(q, kpad, vpad, kctx, vctx, sink)


def _fnet_kernel(x_ref, wc_ref, dm_ref, wf_ref, o_ref, y_sc, *, seq):
    @pl.when(pl.program_id(1) == 0)
    def _():
        y = _dot(x_ref[...], wc_ref[...])
        y_sc[0:seq, :] = y[:, :FNET_WIDTH].astype(BF16)
        y_sc[seq:2 * seq, :] = y[:, FNET_WIDTH:].astype(BF16)

    z = _dot(dm_ref[...], y_sc[...]) * ((seq * FNET_HEAD_DIM) ** -0.5)
    o_ref[...] = _dot(z.astype(BF16), wf_ref[...]).astype(o_ref.dtype)


def _fnet_call(x, wc, dm, wf, batch, seq):
    tl = min(seq, 256)
    nt = seq // tl
    return pl.pallas_call(
        functools.partial(_fnet_kernel, seq=seq),
        out_shape=jax.ShapeDtypeStruct((batch * seq, FNET_WIDTH), BF16),
        grid=(batch, nt),
        in_specs=[pl.BlockSpec((seq, FNET_WIDTH), lambda b, r: (b, 0)),
                  pl.BlockSpec((FNET_WIDTH, 2 * FNET_WIDTH), lambda b, r: (0, 0)),
                  pl.BlockSpec((tl, 2 * seq), lambda b, r: (r, 0)),
                  pl.BlockSpec((FNET_WIDTH, FNET_WIDTH), lambda b, r: (0, 0))],
        out_specs=pl.BlockSpec((tl, FNET_WIDTH), lambda b, r: (b * nt + r, 0)),
        scratch_shapes=[pltpu.VMEM((2 * seq, FNET_WIDTH), BF16)],
        compiler_params=_params("arbitrary", "arbitrary"),
        name="fourier_mix",
    )(x, wc, dm, wf)


def _s5_kernel(xf_ref, xb_ref, h0_ref, bm_ref, cm_ref, lam_ref, yf_ref, yb_ref, fin_ref,
               buf, carry, *, steps, nchunks):
    c = pl.program_id(1)

    @pl.when(c == 0)
    def _():
        carry[...] = h0_ref[0]

    buf[0] = _dot(xf_ref[0].astype(BF16), bm_ref[0])
    buf[1] = _dot(xb_ref[0].astype(BF16), bm_ref[1])

    w = SCAN_LANE_CHUNK
    for lc in range(SSM_LANES // w):
        re = slice(lc * w, (lc + 1) * w)
        im = slice(SSM_LANES + lc * w, SSM_LANES + (lc + 1) * w)
        lam = [(lam_ref[d, :, re], lam_ref[d, :, im]) for d in range(2)]

        def body(t, st, re=re, im=im, lam=lam):
            out = []
            for d in range(2):
                tt = t if d == 0 else steps - 1 - t
                r0 = pl.multiple_of(tt * SUBLANES, SUBLANES)
                sr, si = st[2 * d], st[2 * d + 1]
                lr, li = lam[d]
                nr = (lr * sr - li * si) + buf[d, pl.ds(r0, SUBLANES), re]
                ni = (lr * si + li * sr) + buf[d, pl.ds(r0, SUBLANES), im]
                buf[d, pl.ds(r0, SUBLANES), re] = nr
                buf[d, pl.ds(r0, SUBLANES), im] = ni
                out += [nr, ni]
            return tuple(out)

        init = (carry[0, :, re], carry[0, :, im], carry[1, :, re], carry[1, :, im])
        fr, fi, br, bi = lax.fori_loop(0, steps, body, init, unroll=2)
        carry[0, :, re] = fr
        carry[0, :, im] = fi
        carry[1, :, re] = br
        carry[1, :, im] = bi

    yf_ref[0] = _dot(buf[0].astype(BF16), cm_ref[0])
    yb_ref[0] = _dot(buf[1].astype(BF16), cm_ref[1])

    @pl.when(c == nchunks - 1)
    def _():
        fin_ref[0] = carry[...]


def _s5_call(x_tm, h0, bm, cm, lam, groups, seq):
    steps = SCAN_STEPS
    rows = steps * SUBLANES
    nchunks = seq // steps
    width = 2 * SSM_LANES
    return pl.pallas_call(
        functools.partial(_s5_kernel, steps=steps, nchunks=nchunks),
        out_shape=[jax.ShapeDtypeStruct((groups, seq * SUBLANES, SSM_WIDTH), F32),
                   jax.ShapeDtypeStruct((groups, seq * SUBLANES, SSM_WIDTH), F32),
                   jax.ShapeDtypeStruct((groups, 2, SUBLANES, width), F32)],
        grid=(groups, nchunks),
        in_specs=[pl.BlockSpec((1, rows, SSM_WIDTH), lambda g, c: (g, c, 0)),
                  pl.BlockSpec((1, rows, SSM_WIDTH), lambda g, c: (g, nchunks - 1 - c, 0)),
                  pl.BlockSpec((1, 2, SUBLANES, width), lambda g, c: (g, 0, 0, 0)),
                  pl.BlockSpec((2, SSM_WIDTH, width), lambda g, c: (0, 0, 0)),
                  pl.BlockSpec((2, width, SSM_WIDTH), lambda g, c: (0, 0, 0)),
                  pl.BlockSpec((2, SUBLANES, width), lambda g, c: (0, 0, 0))],
        out_specs=[pl.BlockSpec((1, rows, SSM_WIDTH), lambda g, c: (g, c, 0)),
                   pl.BlockSpec((1, rows, SSM_WIDTH), lambda g, c: (g, nchunks - 1 - c, 0)),
                   pl.BlockSpec((1, 2, SUBLANES, width), lambda g, c: (g, 0, 0, 0))],
        scratch_shapes=[pltpu.VMEM((2, rows, width), F32),
                        pltpu.VMEM((2, SUBLANES, width), F32)],
        compiler_params=_params("arbitrary", "arbitrary"),
        name="s5_scan",
    )(x_tm, x_tm, h0, bm, cm, lam)


def _outproj_kernel(x_ref, mod_ref, a_ref, f_ref, us_ref, yf_ref, yb_ref, d_ref, wg_ref, wo_ref,
                    o_ref):
    ys = jax.nn.gelu((d_ref[...] * us_ref[...] + yf_ref[...]) + yb_ref[...])
    gl = _dot(ys.astype(BF16), wg_ref[...])
    so = gl[:, :SSM_WIDTH] * jax.nn.sigmoid(gl[:, SSM_WIDTH:])
    mixed = jnp.concatenate([a_ref[...], f_ref[...], so.astype(BF16)], axis=1)
    y = _dot(mixed, wo_ref[...])
    o_ref[...] = x_ref[...] + mod_ref[0][5:6] * y


def _outproj_call(x, mod, a, f, us, yf, yb, d, wg, wo, rows_per_group):
    n = x.shape[0]
    tm = TOKEN_TILE
    tiles_per_group = rows_per_group // tm

    def row(w):
        return pl.BlockSpec((tm, w), lambda i: (i, 0))

    return pl.pallas_call(
        _outproj_kernel,
        out_shape=jax.ShapeDtypeStruct((n, D_MODEL), F32),
        grid=(n // tm,),
        in_specs=[row(D_MODEL),
                  pl.BlockSpec((1, N_MOD, D_MODEL), lambda i: (i // tiles_per_group, 0, 0)),
                  row(ATTN_WIDTH), row(FNET_WIDTH), row(SSM_WIDTH), row(SSM_WIDTH), row(SSM_WIDTH),
                  _const_spec((1, SSM_WIDTH)),
                  _const_spec((SSM_WIDTH, 2 * SSM_WIDTH)),
                  _const_spec((MIX_WIDTH, D_MODEL))],
        out_specs=row(D_MODEL),
        compiler_params=_params("arbitrary"),
        name="out_proj",
    )(x, mod, a, f, us, yf, yb, d, wg, wo)


def _rope_tables(seq):
    pos = np.arange(seq)
    row, col = pos // GRID_W, pos % GRID_W
    quarter = HEAD_DIM // 4
    inv_freq = 1.0 / (ROPE_BASE ** (np.arange(quarter, dtype=np.float64) * 2.0 / (HEAD_DIM // 2)))
    lane = np.arange(LANES)
    in_head = lane % HEAD_DIM
    p = np.where((in_head < HEAD_DIM // 2)[None, :], row[:, None], col[:, None]).astype(np.float64)
    ang = p * inv_freq[lane % quarter][None, :]
    first = ((lane % (HEAD_DIM // 2)) < quarter)[None, :]
    cos = np.cos(ang)
    sa = np.where(first, -np.sin(ang), 0.0)
    sb = np.where(first, 0.0, np.sin(ang))
    return tuple(jnp.asarray(t, dtype=F32) for t in (cos, sa, sb))


def _dft_tables(seq):
    kl = np.outer(np.arange(seq), np.arange(seq)) % seq
    ang = 2.0 * np.pi * kl / seq
    dm = np.concatenate([np.cos(ang), -np.sin(ang)], axis=1)
    mc = np.outer(np.arange(FNET_HEAD_DIM), np.arange(FNET_HEAD_DIM)) % FNET_HEAD_DIM
    a64 = 2.0 * np.pi * mc / FNET_HEAD_DIM
    eye = np.eye(FNET_HEADS)
    wc = np.concatenate([np.kron(eye, np.cos(a64)), np.kron(eye, np.sin(a64))], axis=1)
    return jnp.asarray(dm, dtype=BF16), jnp.asarray(wc, dtype=BF16)


def _head_mean_matrix():
    return jnp.asarray(np.kron(np.eye(N_HEADS), np.full((HEAD_DIM, HEAD_DIM), 1.0 / HEAD_DIM)), dtype=BF16)


def _s5_matrices(lp):
    eye = jnp.eye(SSM_GROUPS, dtype=F32)
    bms, cms, lams = [], [], []
    for r in range(2):
        lam = lax.complex(lp['ssm_lambda_re'][r], lp['ssm_lambda_im'][r])
        bmat = lax.complex(lp['ssm_b_re'][r], lp['ssm_b_im'][r])
        step = jnp.exp(lp['ssm_log_step'][r])[:, None]
        lam_bar = jnp.exp(lam * step)
        b_bar = ((lam_bar - 1.0) / lam)[..., None] * bmat

        def in_mat(b):
            return jnp.einsum('gph,gk->ghkp', b, eye).reshape(SSM_WIDTH, SSM_LANES)

        def out_mat(cw):
            return jnp.einsum('ghp,gk->kpgh', cw, eye).reshape(SSM_LANES, SSM_WIDTH)

        bms.append(jnp.concatenate([in_mat(b_bar.real), in_mat(b_bar.imag)], axis=1))
        cms.append(jnp.concatenate([out_mat(lp['ssm_c_re'][r]), -out_mat(lp['ssm_c_im'][r])], axis=0))
        lrow = jnp.concatenate([lam_bar.real.reshape(-1), lam_bar.imag.reshape(-1)])
        lams.append(jnp.broadcast_to(lrow[None, :], (SUBLANES, 2 * SSM_LANES)))
    return jnp.stack(bms).astype(BF16), jnp.stack(cms).astype(BF16), jnp.stack(lams)


def _to_time_major(u, batch, seq):
    g = batch // SUBLANES
    return u.reshape(g, SUBLANES, seq, -1).transpose(0, 2, 1, 3).reshape(g, seq * SUBLANES, -1)


def _from_time_major(y, batch, seq):
    g = batch // SUBLANES
    return y.reshape(g, seq, SUBLANES, -1).transpose(0, 2, 1, 3).reshape(batch * seq, -1)


def _stream_layer(x, mod, lw, batch, seq, rows_per_group, tables, ctx_kv, h0):
    latent = ctx_kv is not None
    x = _ffn_call(x, mod, 0, lw['g'][0], lw['w1'][0], lw['w3'][0], lw['w2'][0], rows_per_group)
    q, k, v, f, us = _inproj_call(x, mod, lw['g'][1], lw['w_in'], lw['qg'], lw['kg'], tables['e'],
                                  tables['rope'] if latent else None, rows_per_group, seq)
    if latent:
        pad = ((0, 0), (WINDOW, WINDOW), (0, 0))
        kp = jnp.pad(k.reshape(batch, seq, KV_WIDTH), pad)
        vp = jnp.pad(v.reshape(batch, seq, KV_WIDTH), pad)
        a = _lat_attn_call(q, kp, vp, ctx_kv[0], ctx_kv[1], lw['sink'], batch, seq)
    else:
        a = _ctx_attn_call(q, k, v, lw['sink'], batch, seq)
    dm, wc = tables['dft'][seq]
    fz = _fnet_call(f, wc, dm, lw['w_fnet'], batch, seq)
    yf, yb, fin = _s5_call(_to_time_major(us, batch, seq), h0, lw['bm'], lw['cm'], lw['lam'],
                           batch // SUBLANES, seq)
    yf = _from_time_major(yf, batch, seq)
    yb = _from_time_major(yb, batch, seq)
    x = _outproj_call(x, mod, a, fz, us, yf, yb, lw['d'], lw['w_glu'], lw['w_out'], rows_per_group)
    x = _ffn_call(x, mod, 6, lw['g'][2], lw['w1'][1], lw['w3'][1], lw['w2'][1], rows_per_group)
    return x, (k, v, fin)


def kernel(x_prompt, x_sample, cache_k, cache_v, state_ssm_re, state_ssm_im, c, c_ctx, w_mod, b_mod, norm_g, ffn_w1, ffn_w3, ffn_w2, w_in, w_out, q_norm_g, k_norm_g, attn_sink, w_fnet, ssm_lambda_re, ssm_lambda_im, ssm_b_re, ssm_b_im, ssm_c_re, ssm_c_im, ssm_d, ssm_log_step, ssm_w_glu):
    batch, seq, _ = x_prompt.shape
    dec_batch, dec_seq, _ = x_sample.shape
    past = cache_k.shape[2]

    tables = {'e': _head_mean_matrix(), 'rope': _rope_tables(dec_seq),
              'dft': {s: _dft_tables(s) for s in {seq, dec_seq}}}

    cond = jnp.zeros((COND_ROWS, D_MODEL), F32).at[0].set(c_ctx).at[1:1 + dec_batch].set(c)
    mods = _mod_call(cond, w_mod, b_mod).reshape(DEPTH, COND_ROWS, N_MOD, D_MODEL)

    yp = x_prompt.reshape(batch * seq, D_MODEL)
    ys = x_sample.reshape(dec_batch * dec_seq, D_MODEL)
    ks, vs, sre, sim = [], [], [], []
    for l in range(DEPTH):
        lp = {'ssm_lambda_re': ssm_lambda_re[l], 'ssm_lambda_im': ssm_lambda_im[l],
              'ssm_b_re': ssm_b_re[l], 'ssm_b_im': ssm_b_im[l],
              'ssm_c_re': ssm_c_re[l], 'ssm_c_im': ssm_c_im[l], 'ssm_log_step': ssm_log_step[l]}
        bm, cm, lam = _s5_matrices(lp)
        lw = {'g': norm_g[l].reshape(3, 1, D_MODEL),
              'w1': ffn_w1[l].astype(BF16), 'w3': ffn_w3[l].astype(BF16), 'w2': ffn_w2[l].astype(BF16),
              'w_in': w_in[l].astype(BF16), 'w_out': w_out[l].astype(BF16),
              'qg': jnp.tile(q_norm_g[l], N_HEADS)[None, :], 'kg': jnp.tile(k_norm_g[l], N_KV_HEADS)[None, :],
              'sink': jnp.broadcast_to(attn_sink[l][:, None], (N_HEADS, LANES)),
              'w_fnet': w_fnet[l].astype(BF16), 'bm': bm, 'cm': cm, 'lam': lam,
              'd': ssm_d[l].reshape(1, SSM_WIDTH), 'w_glu': ssm_w_glu[l].astype(BF16)}

        h0_ctx = jnp.zeros((batch // SUBLANES, 2, SUBLANES, 2 * SSM_LANES), F32)
        yp, (k_l, v_l, fin) = _stream_layer(yp, mods[l, 0:1], lw, batch, seq, batch * seq, tables,
                                            None, h0_ctx)
        ks.append(k_l.reshape(batch, seq, N_KV_HEADS, HEAD_DIM))
        vs.append(v_l.reshape(batch, seq, N_KV_HEADS, HEAD_DIM))
        fin = fin.reshape(batch // SUBLANES, 2, SUBLANES, 2, SSM_GROUPS, SSM_STATE)
        fin = fin.transpose(3, 0, 2, 1, 4, 5).reshape(2, batch, 2, SSM_GROUPS, SSM_STATE)
        sre.append(fin[0])
        sim.append(fin[1])

        h0 = jnp.concatenate([state_ssm_re[:, l].reshape(dec_batch, 2, SSM_LANES),
                              state_ssm_im[:, l].reshape(dec_batch, 2, SSM_LANES)], axis=-1)
        h0 = h0.reshape(dec_batch // SUBLANES, SUBLANES, 2, 2 * SSM_LANES).transpose(0, 2, 1, 3)
        ctx_kv = (cache_k[:, l].reshape(dec_batch, past, KV_WIDTH),
                  cache_v[:, l].reshape(dec_batch, past, KV_WIDTH))
        ys, _ = _stream_layer(ys, mods[l, 1:1 + dec_batch], lw, dec_batch, dec_seq, dec_seq, tables,
                              ctx_kv, h0)

    return (yp.reshape(batch, seq, D_MODEL), ys.reshape(dec_batch, dec_seq, D_MODEL),
            jnp.stack(ks, axis=1), jnp.stack(vs, axis=1), jnp.stack(sre, axis=1), jnp.stack(sim, axis=1))
```

```python
import functools
import math

import numpy as np
import jax
import jax.numpy as jnp
from jax import lax
from jax.experimental import pallas as pl
from jax.experimental.pallas import tpu as pltpu

F32 = jnp.float32
BF16 = jnp.bfloat16

D_MODEL = 1024
DEPTH = 2
GRID_W = 64
HEAD_DIM = 64
N_HEADS = 8
N_KV_HEADS = 2
ATTN_WIDTH = N_HEADS * HEAD_DIM
KV_WIDTH = N_KV_HEADS * HEAD_DIM
WINDOW = 128
BLOCK = 128
ATTN_SCALE = HEAD_DIM ** -0.5
ROPE_BASE = 10000.0
NEG_INF = -1e30
LOG2E = math.log2(math.e)
FNET_HEADS = 4
FNET_HEAD_DIM = 64
FNET_WIDTH = FNET_HEADS * FNET_HEAD_DIM
SSM_WIDTH = 256
SSM_GROUP = 16
SSM_GROUPS = SSM_WIDTH // SSM_GROUP
SSM_STATE = 64
SSM_LANES = SSM_GROUPS * SSM_STATE
MIX_WIDTH = ATTN_WIDTH + FNET_WIDTH + SSM_WIDTH
Q_END = ATTN_WIDTH
K_END = Q_END + KV_WIDTH
V_END = K_END + KV_WIDTH
F_END = V_END + FNET_WIDTH
S_END = F_END + SSM_WIDTH
IN_WIDTH = S_END
D_FF = 2816
N_MOD = 9
NORM_EPS = 1e-6

LANES = 128
SUBLANES = 8
VMEM_LIMIT_BYTES = 56 * 1024 * 1024
TOKEN_TILE = 256
FNET_ROW_TILE = 512
SCAN_STEPS = 64
SCAN_LANE_CHUNK = 512
COND_ROWS = 16
KV_VARIANTS = 2 * N_KV_HEADS
LOCAL_SPAN = BLOCK + 2 * WINDOW


def _params(*sem):
    return pltpu.CompilerParams(dimension_semantics=sem, vmem_limit_bytes=VMEM_LIMIT_BYTES)


def _dot(a, b):
    return jnp.dot(a, b, preferred_element_type=F32)


def _norm_mod(x, g, shift, scale):
    y = x * lax.rsqrt(jnp.mean(x * x, axis=-1, keepdims=True) + NORM_EPS)
    return (y * g) * (1.0 + scale) + shift


def _const_spec(shape, lead=()):
    nd = len(shape)
    idx = tuple(lead) + (0,) * nd
    return pl.BlockSpec((None,) * len(lead) + tuple(shape), lambda *_: idx,
                        pipeline_mode=pl.Buffered(1))


def _mod_kernel(c_ref, w_ref, b_ref, o_ref):
    c = c_ref[...]
    a = (c * jax.nn.sigmoid(c)).astype(BF16)
    o_ref[0] = _dot(a, w_ref[0].astype(BF16)) + b_ref[0]


def _mod_call(cond, w_mod, b_mod):
    tn = 1024
    return pl.pallas_call(
        _mod_kernel,
        out_shape=jax.ShapeDtypeStruct((DEPTH, COND_ROWS, N_MOD * D_MODEL), F32),
        grid=(DEPTH, N_MOD * D_MODEL // tn),
        in_specs=[pl.BlockSpec((COND_ROWS, D_MODEL), lambda l, j: (0, 0)),
                  pl.BlockSpec((1, D_MODEL, tn), lambda l, j: (l, 0, j)),
                  pl.BlockSpec((1, 1, tn), lambda l, j: (l, 0, j))],
        out_specs=pl.BlockSpec((1, COND_ROWS, tn), lambda l, j: (l, 0, j)),
        compiler_params=_params("arbitrary", "arbitrary"),
        name="cond_mod",
    )(cond, w_mod, b_mod.reshape(DEPTH, 1, N_MOD * D_MODEL))


def _ffn_kernel(x_ref, mod_ref, g_ref, w1_ref, w3_ref, w2_ref, o_ref, *, k0):
    x = x_ref[...]
    m = mod_ref[0]
    h = _norm_mod(x, g_ref[...], m[k0:k0 + 1], m[k0 + 1:k0 + 2]).astype(BF16)
    a = _dot(h, w1_ref[...])
    b = _dot(h, w3_ref[...])
    t = ((a * jax.nn.sigmoid(a)) * b).astype(BF16)
    y = _dot(t, w2_ref[...])
    o_ref[...] = x + (0.5 * m[k0 + 2:k0 + 3]) * y


def _ffn_call(x, mod, pw, layer, half, rows_per_group):
    n = x.shape[0]
    tm = TOKEN_TILE
    tiles_per_group = rows_per_group // tm
    return pl.pallas_call(
        functools.partial(_ffn_kernel, k0=6 * half),
        out_shape=jax.ShapeDtypeStruct((n, D_MODEL), F32),
        grid=(n // tm,),
        in_specs=[pl.BlockSpec((tm, D_MODEL), lambda i: (i, 0)),
                  pl.BlockSpec((1, N_MOD, D_MODEL), lambda i: (i // tiles_per_group, 0, 0)),
                  _const_spec((1, D_MODEL), (layer, 2 * half)),
                  _const_spec((D_MODEL, D_FF), (layer, half)),
                  _const_spec((D_MODEL, D_FF), (layer, half)),
                  _const_spec((D_FF, D_MODEL), (layer, half))],
        out_specs=pl.BlockSpec((tm, D_MODEL), lambda i: (i, 0)),
        compiler_params=_params("arbitrary"),
        name="half_ffn",
    )(x, mod, pw['g'], pw['w1'], pw['w3'], pw['w2'])


def _head_norm(z, e, g):
    z2 = z * z
    hi = z2.astype(BF16)
    lo = (z2 - hi.astype(F32)).astype(BF16)
    ms = _dot(hi, e) + _dot(lo, e)
    return (z * lax.rsqrt(ms + NORM_EPS)) * g


def _kv_variants(k, v):
    lane = lax.broadcasted_iota(jnp.int32, k.shape, 1)
    low = lane < HEAD_DIM
    kr = pltpu.roll(k, HEAD_DIM, 1)
    vr = pltpu.roll(v, HEAD_DIM, 1)
    zero = jnp.zeros_like(k)
    one_lo = jnp.where(lane == 0, 1.0, 0.0)
    one_hi = jnp.where(lane == HEAD_DIM, 1.0, 0.0)
    kvar = [jnp.where(low, k, zero), jnp.where(low, zero, kr),
            jnp.where(low, kr, zero), jnp.where(low, zero, k)]
    vvar = [jnp.where(low, v, one_hi), jnp.where(low, one_lo, vr),
            jnp.where(low, vr, one_hi), jnp.where(low, one_lo, v)]
    return (jnp.concatenate(kvar, axis=1).astype(BF16), jnp.concatenate(vvar, axis=1).astype(BF16))


def _inproj_kernel(*refs, rope):
    if rope:
        (x_ref, mod_ref, g_ref, w_ref, qg_ref, kg_ref, e_ref, cos_ref, sa_ref, sb_ref,
         q_ref, k_ref, v_ref, kvar_ref, vvar_ref, f_ref, s_ref) = refs
    else:
        (x_ref, mod_ref, g_ref, w_ref, qg_ref, kg_ref, e_ref,
         q_ref, k_ref, v_ref, kvar_ref, vvar_ref, f_ref, s_ref) = refs
    m = mod_ref[0]
    h = _norm_mod(x_ref[...], g_ref[...], m[3:4], m[4:5]).astype(BF16)
    u = _dot(h, w_ref[...])
    q = _head_norm(u[:, :Q_END], e_ref[...], qg_ref[...])
    k = _head_norm(u[:, Q_END:K_END], e_ref[0:KV_WIDTH, 0:KV_WIDTH], kg_ref[...])
    v = u[:, K_END:V_END]
    if rope:
        cos, sa, sb = cos_ref[...], sa_ref[...], sb_ref[...]

        def rot(z):
            return z * cos + pltpu.roll(z, LANES - 16, 1) * sa + pltpu.roll(z, 16, 1) * sb

        q = jnp.concatenate([rot(q[:, j * LANES:(j + 1) * LANES])
                             for j in range(ATTN_WIDTH // LANES)], axis=1)
        k = rot(k)
    q_ref[...] = (q * (ATTN_SCALE * LOG2E)).astype(BF16)
    k_ref[...] = k
    v_ref[...] = v
    kvar_ref[...], vvar_ref[...] = _kv_variants(k, v)
    f_ref[...] = u[:, V_END:F_END].astype(BF16)
    s_ref[...] = u[:, F_END:S_END]


def _tm_spec(tm, tiles_per_seq):
    def index(i):
        b = i // tiles_per_seq
        return (b // SUBLANES, i % tiles_per_seq, b % SUBLANES)
    return pl.BlockSpec((None, tm, SSM_WIDTH), index)


def _inproj_call(x, mod, pw, layer, tables, rope, batch, seq, rows_per_group):
    n = x.shape[0]
    tm = TOKEN_TILE
    tiles_per_group = rows_per_group // tm
    tiles_per_seq = seq // tm
    in_specs = [pl.BlockSpec((tm, D_MODEL), lambda i: (i, 0)),
                pl.BlockSpec((1, N_MOD, D_MODEL), lambda i: (i // tiles_per_group, 0, 0)),
                _const_spec((1, D_MODEL), (layer, 1)),
                _const_spec((D_MODEL, IN_WIDTH), (layer,)),
                _const_spec((1, ATTN_WIDTH), (layer,)),
                _const_spec((1, KV_WIDTH), (layer,)),
                _const_spec((ATTN_WIDTH, ATTN_WIDTH))]
    args = [x, mod, pw['g'], pw['w_in'], pw['qg'], pw['kg'], tables['e']]
    if rope:
        in_specs += [pl.BlockSpec((tm, LANES), lambda i: (i % tiles_per_seq, 0))] * 3
        args += list(tables['rope'])

    def row(w):
        return pl.BlockSpec((tm, w), lambda i: (i, 0))

    kvw = KV_VARIANTS * LANES
    out_shape = [jax.ShapeDtypeStruct((n, ATTN_WIDTH), BF16),
                 jax.ShapeDtypeStruct((n, KV_WIDTH), F32),
                 jax.ShapeDtypeStruct((n, KV_WIDTH), F32),
                 jax.ShapeDtypeStruct((n, kvw), BF16),
                 jax.ShapeDtypeStruct((n, kvw), BF16),
                 jax.ShapeDtypeStruct((n, FNET_WIDTH), BF16),
                 jax.ShapeDtypeStruct((batch // SUBLANES, seq, SUBLANES * SSM_WIDTH), F32)]
    out_specs = [row(ATTN_WIDTH), row(KV_WIDTH), row(KV_WIDTH), row(kvw), row(kvw), row(FNET_WIDTH),
                 _tm_spec(tm, tiles_per_seq)]
    return pl.pallas_call(
        functools.partial(_inproj_kernel, rope=rope),
        out_shape=out_shape,
        grid=(n // tm,),
        in_specs=in_specs,
        out_specs=out_specs,
        compiler_params=_params("arbitrary"),
        name="in_proj",
    )(*args)


def _attend(q_ref, o_ref, sink_ref, kv, bias, nq):
    low = lax.broadcasted_iota(jnp.int32, (2 * nq, LANES), 1) < HEAD_DIM
    for h in range(N_KV_HEADS):
        c0 = h * 2 * LANES
        qst = jnp.concatenate([q_ref[:, c0:c0 + LANES], q_ref[:, c0 + LANES:c0 + 2 * LANES]], axis=0)
        outs = []
        for p in range(2):
            j0 = 4 * h + p
            j1 = 4 * h + 2 + p
            sinkv = jnp.concatenate(
                [jnp.broadcast_to(sink_ref[j0:j0 + 1, 0:1], (nq, 1)),
                 jnp.broadcast_to(sink_ref[j1:j1 + 1, 0:1], (nq, 1))], axis=0) * LOG2E
            kvar, vvar = kv(2 * h + p)
            s = lax.dot_general(qst, kvar, (((1,), (1,)), ((), ())), preferred_element_type=F32)
            if bias is not None:
                s = s + bias
            m = jnp.maximum(jnp.max(s, axis=-1, keepdims=True), sinkv)
            e = jnp.exp2(s - m).astype(BF16)
            o = _dot(e, vvar)
            dl = HEAD_DIM * (1 - p)
            d = o[:, dl:dl + 1] + jnp.exp2(sinkv - m)
            outs.append(o * (1.0 / d))
        acc = jnp.where(low, outs[0], outs[1])
        o_ref[:, c0:c0 + LANES] = acc[0:nq].astype(o_ref.dtype)
        o_ref[:, c0 + LANES:c0 + 2 * LANES] = acc[nq:2 * nq].astype(o_ref.dtype)


def _ctx_attn_kernel(q_ref, kvar_ref, vvar_ref, sink_ref, o_ref, *, seq):
    def kv(i):
        return (kvar_ref[:, i * LANES:(i + 1) * LANES], vvar_ref[:, i * LANES:(i + 1) * LANES])
    _attend(q_ref, o_ref, sink_ref, kv, None, seq)


def _ctx_attn_call(q, kvar, vvar, sink, layer, batch, seq):
    kvw = KV_VARIANTS * LANES
    return pl.pallas_call(
        functools.partial(_ctx_attn_kernel, seq=seq),
        out_shape=jax.ShapeDtypeStruct((batch * seq, ATTN_WIDTH), BF16),
        grid=(batch,),
        in_specs=[pl.BlockSpec((seq, ATTN_WIDTH), lambda b: (b, 0)),
                  pl.BlockSpec((seq, kvw), lambda b: (b, 0)),
                  pl.BlockSpec((seq, kvw), lambda b: (b, 0)),
                  pl.BlockSpec((None, N_HEADS, LANES), lambda b: (layer, 0, 0))],
        out_specs=pl.BlockSpec((seq, ATTN_WIDTH), lambda b: (b, 0)),
        compiler_params=_params("arbitrary"),
        name="ctx_attn",
    )(q, kvar, vvar, sink)


def _lat_attn_kernel(q_ref, kl_ref, vl_ref, kc_ref, vc_ref, bias_ref, sink_ref, o_ref,
                     kc_sc, vc_sc, *, seq):
    n = pl.program_id(1)

    @pl.when(n == 0)
    def _():
        kc_sc[...], vc_sc[...] = _kv_variants(kc_ref[...], vc_ref[...])

    start = pl.multiple_of(jnp.clip(n * BLOCK - WINDOW, 0, seq - LOCAL_SPAN), BLOCK)

    def kv(i):
        cols = slice(i * LANES, (i + 1) * LANES)
        return (jnp.concatenate([kc_sc[:, cols], kl_ref[pl.ds(start, LOCAL_SPAN), cols]], axis=0),
                jnp.concatenate([vc_sc[:, cols], vl_ref[pl.ds(start, LOCAL_SPAN), cols]], axis=0))

    _attend(q_ref, o_ref, sink_ref, kv, bias_ref[...], BLOCK)


def _lat_attn_call(q, kvar, vvar, kctx, vctx, bias, sink, layer, batch, seq):
    past = kctx.shape[2]
    nb = seq // BLOCK
    kvw = KV_VARIANTS * LANES
    nkeys = past + LOCAL_SPAN

    def bias_index(b, n):
        return (jnp.where(n == 0, 0, jnp.where(n == nb - 1, 2, 1)), 0, 0)

    return pl.pallas_call(
        functools.partial(_lat_attn_kernel, seq=seq),
        out_shape=jax.ShapeDtypeStruct((batch * seq, ATTN_WIDTH), BF16),
        grid=(batch, nb),
        in_specs=[pl.BlockSpec((BLOCK, ATTN_WIDTH), lambda b, n: (b * nb + n, 0)),
                  pl.BlockSpec((seq, kvw), lambda b, n: (b, 0)),
                  pl.BlockSpec((seq, kvw), lambda b, n: (b, 0)),
                  pl.BlockSpec((None, None, past, KV_WIDTH), lambda b, n: (b, layer, 0, 0)),
                  pl.BlockSpec((None, None, past, KV_WIDTH), lambda b, n: (b, layer, 0, 0)),
                  pl.BlockSpec((None, 2 * BLOCK, nkeys), bias_index),
                  pl.BlockSpec((None, N_HEADS, LANES), lambda b, n: (layer, 0, 0))],
        out_specs=pl.BlockSpec((BLOCK, ATTN_WIDTH), lambda b, n: (b * nb + n, 0)),
        scratch_shapes=[pltpu.VMEM((past, kvw), BF16), pltpu.VMEM((past, kvw), BF16)],
        compiler_params=_params("arbitrary", "arbitrary"),
        name="lat_attn",
    )(q, kvar, vvar, kctx, vctx, bias, sink)


def _fnet_kernel(x_ref, wc_ref, dm_ref, wf_ref, o_ref, y_sc, dm_sc, *, seq):
    r = pl.program_id(0)
    b = pl.program_id(1)

    @pl.when(r == 0)
    def _():
        y = _dot(x_ref[...], wc_ref[...].astype(BF16))
        y_sc[b, 0:seq, :] = y[:, :FNET_WIDTH].astype(BF16)
        y_sc[b, seq:2 * seq, :] = y[:, FNET_WIDTH:].astype(BF16)

    @pl.when(b == 0)
    def _():
        dm_sc[...] = dm_ref[...].astype(BF16)

    z = _dot(dm_sc[...], y_sc[b]) * ((seq * FNET_HEAD_DIM) ** -0.5)
    o_ref[...] = _dot(z.astype(BF16), wf_ref[...]).astype(o_ref.dtype)


def _fnet_call(x, wc, dm, wf, layer, batch, seq):
    tl = min(seq, FNET_ROW_TILE)
    nt = seq // tl
    return pl.pallas_call(
        functools.partial(_fnet_kernel, seq=seq),
        out_shape=jax.ShapeDtypeStruct((batch * seq, FNET_WIDTH), BF16),
        grid=(nt, batch),
        in_specs=[pl.BlockSpec((seq, FNET_WIDTH), lambda r, b: (jnp.where(r == 0, b, 0), 0)),
                  pl.BlockSpec((FNET_WIDTH, 2 * FNET_WIDTH), lambda r, b: (0, 0)),
                  pl.BlockSpec((tl, 2 * seq), lambda r, b: (r, 0)),
                  pl.BlockSpec((None, FNET_WIDTH, FNET_WIDTH), lambda r, b: (layer, 0, 0))],
        out_specs=pl.BlockSpec((tl, FNET_WIDTH), lambda r, b: (b * nt + r, 0)),
        scratch_shapes=[pltpu.VMEM((batch, 2 * seq, FNET_WIDTH), BF16),
                        pltpu.VMEM((tl, 2 * seq), BF16)],
        compiler_params=_params("arbitrary", "arbitrary"),
        name="fourier_mix",
    )(x, wc, dm, wf)


def _s5_kernel(xf_ref, xb_ref, h0_ref, bm_ref, cm_ref, lam_ref, yf_ref, yb_ref, fin_ref,
               buf, carry, *, steps, nchunks):
    c = pl.program_id(1)

    @pl.when(c == 0)
    def _():
        carry[...] = h0_ref[0]

    buf[0] = _dot(xf_ref[0].astype(BF16), bm_ref[0])
    buf[1] = _dot(xb_ref[0].astype(BF16), bm_ref[1])

    w = SCAN_LANE_CHUNK
    for lc in range(SSM_LANES // w):
        re = slice(lc * w, (lc + 1) * w)
        im = slice(SSM_LANES + lc * w, SSM_LANES + (lc + 1) * w)
        lam = [(lam_ref[d, :, re], lam_ref[d, :, im]) for d in range(2)]

        def body(t, st, re=re, im=im, lam=lam):
            out = []
            for d in range(2):
                tt = t if d == 0 else steps - 1 - t
                r0 = pl.multiple_of(tt * SUBLANES, SUBLANES)
                sr, si = st[2 * d], st[2 * d + 1]
                lr, li = lam[d]
                nr = (lr * sr - li * si) + buf[d, pl.ds(r0, SUBLANES), re]
                ni = (lr * si + li * sr) + buf[d, pl.ds(r0, SUBLANES), im]
                buf[d, pl.ds(r0, SUBLANES), re] = nr
                buf[d, pl.ds(r0, SUBLANES), im] = ni
                out += [nr, ni]
            return tuple(out)

        init = (carry[0, :, re], carry[0, :, im], carry[1, :, re], carry[1, :, im])
        fr, fi, br, bi = lax.fori_loop(0, steps, body, init, unroll=2)
        carry[0, :, re] = fr
        carry[0, :, im] = fi
        carry[1, :, re] = br
        carry[1, :, im] = bi

    yf_ref[0] = _dot(buf[0].astype(BF16), cm_ref[0])
    yb_ref[0] = _dot(buf[1].astype(BF16), cm_ref[1])

    @pl.when(c == nchunks - 1)
    def _():
        fin_ref[0] = carry[...]


def _s5_call(x_tm, h0, pw, layer, groups, seq):
    steps = SCAN_STEPS
    rows = steps * SUBLANES
    nchunks = seq // steps
    width = 2 * SSM_LANES
    return pl.pallas_call(
        functools.partial(_s5_kernel, steps=steps, nchunks=nchunks),
        out_shape=[jax.ShapeDtypeStruct((groups, seq * SUBLANES, SSM_WIDTH), F32),
                   jax.ShapeDtypeStruct((groups, seq * SUBLANES, SSM_WIDTH), F32),
                   jax.ShapeDtypeStruct((groups, 2, SUBLANES, width), F32)],
        grid=(groups, nchunks),
        in_specs=[pl.BlockSpec((1, rows, SSM_WIDTH), lambda g, c: (g, c, 0)),
                  pl.BlockSpec((1, rows, SSM_WIDTH), lambda g, c: (g, nchunks - 1 - c, 0)),
                  pl.BlockSpec((1, 2, SUBLANES, width), lambda g, c: (g, 0, 0, 0)),
                  pl.BlockSpec((None, 2, SSM_WIDTH, width), lambda g, c: (layer, 0, 0, 0)),
                  pl.BlockSpec((None, 2, width, SSM_WIDTH), lambda g, c: (layer, 0, 0, 0)),
                  pl.BlockSpec((None, 2, SUBLANES, width), lambda g, c: (layer, 0, 0, 0))],
        out_specs=[pl.BlockSpec((1, rows, SSM_WIDTH), lambda g, c: (g, c, 0)),
                   pl.BlockSpec((1, rows, SSM_WIDTH), lambda g, c: (g, nchunks - 1 - c, 0)),
                   pl.BlockSpec((1, 2, SUBLANES, width), lambda g, c: (g, 0, 0, 0))],
        scratch_shapes=[pltpu.VMEM((2, rows, width), F32),
                        pltpu.VMEM((2, SUBLANES, width), F32)],
        compiler_params=_params("arbitrary", "arbitrary"),
        name="s5_scan",
    )(x_tm, x_tm, h0, pw['bm'], pw['cm'], pw['lam'])


def _outproj_kernel(x_ref, mod_ref, a_ref, f_ref, us_ref, yf_ref, yb_ref, d_ref, wg_ref, wo_ref,
                    o_ref):
    ys = jax.nn.gelu((d_ref[...] * us_ref[...] + yf_ref[...]) + yb_ref[...])
    gl = _dot(ys.astype(BF16), wg_ref[...])
    so = gl[:, :SSM_WIDTH] * jax.nn.sigmoid(gl[:, SSM_WIDTH:])
    mixed = jnp.concatenate([a_ref[...], f_ref[...], so.astype(BF16)], axis=1)
    y = _dot(mixed, wo_ref[...])
    o_ref[...] = x_ref[...] + mod_ref[0][5:6] * y


def _outproj_call(x, mod, a, f, us, yf, yb, pw, layer, seq, rows_per_group):
    n = x.shape[0]
    tm = TOKEN_TILE
    tiles_per_group = rows_per_group // tm
    tm_spec = _tm_spec(tm, seq // tm)

    def row(w):
        return pl.BlockSpec((tm, w), lambda i: (i, 0))

    return pl.pallas_call(
        _outproj_kernel,
        out_shape=jax.ShapeDtypeStruct((n, D_MODEL), F32),
        grid=(n // tm,),
        in_specs=[row(D_MODEL),
                  pl.BlockSpec((1, N_MOD, D_MODEL), lambda i: (i // tiles_per_group, 0, 0)),
                  row(ATTN_WIDTH), row(FNET_WIDTH), tm_spec, tm_spec, tm_spec,
                  _const_spec((1, SSM_WIDTH), (layer,)),
                  _const_spec((SSM_WIDTH, 2 * SSM_WIDTH), (layer,)),
                  _const_spec((MIX_WIDTH, D_MODEL), (layer,))],
        out_specs=row(D_MODEL),
        compiler_params=_params("arbitrary"),
        name="out_proj",
    )(x, mod, a, f, us, yf, yb, pw['d'], pw['w_glu'], pw['w_out'])


def _rope_tables(seq):
    pos = np.arange(seq)
    row, col = pos // GRID_W, pos % GRID_W
    quarter = HEAD_DIM // 4
    inv_freq = 1.0 / (ROPE_BASE ** (np.arange(quarter, dtype=np.float64) * 2.0 / (HEAD_DIM // 2)))
    lane = np.arange(LANES)
    in_head = lane % HEAD_DIM
    p = np.where((in_head < HEAD_DIM // 2)[None, :], row[:, None], col[:, None]).astype(np.float64)
    ang = p * inv_freq[lane % quarter][None, :]
    first = ((lane % (HEAD_DIM // 2)) < quarter)[None, :]
    cos = np.cos(ang)
    sa = np.where(first, -np.sin(ang), 0.0)
    sb = np.where(first, 0.0, np.sin(ang))
    return tuple(jnp.asarray(t, dtype=F32) for t in (cos, sa, sb))


def _dft_tables(seq):
    kl = np.outer(np.arange(seq), np.arange(seq)) % seq
    ang = 2.0 * np.pi * kl / seq
    dm = np.concatenate([np.cos(ang), -np.sin(ang)], axis=1)
    mc = np.outer(np.arange(FNET_HEAD_DIM), np.arange(FNET_HEAD_DIM)) % FNET_HEAD_DIM
    a64 = 2.0 * np.pi * mc / FNET_HEAD_DIM
    eye = np.eye(FNET_HEADS)
    wc = np.concatenate([np.kron(eye, np.cos(a64)), np.kron(eye, np.sin(a64))], axis=1)
    return jnp.asarray(dm, dtype=F32), jnp.asarray(wc, dtype=F32)


def _head_mean_matrix():
    return jnp.asarray(np.kron(np.eye(N_HEADS), np.full((HEAD_DIM, HEAD_DIM), 1.0 / HEAD_DIM)), dtype=BF16)


def _window_bias(past):
    i = np.arange(2 * BLOCK)[:, None] % BLOCK
    j = np.arange(LOCAL_SPAN)[None, :]
    out = np.zeros((3, 2 * BLOCK, past + LOCAL_SPAN), np.float32)
    for v, off in enumerate((0, WINDOW, 2 * WINDOW)):
        out[v, :, past:] = np.where(np.abs(j - off - i) <= WINDOW, 0.0, NEG_INF)
    return jnp.asarray(out)


def _s5_matrices(lam_re, lam_im, b_re, b_im, c_re, c_im, log_step):
    step = jnp.exp(log_step)[..., None]
    mag = jnp.exp(lam_re * step)
    lr = mag * jnp.cos(lam_im * step)
    li = mag * jnp.sin(lam_im * step)
    den = lam_re * lam_re + lam_im * lam_im
    cr = ((lr - 1.0) * lam_re + li * lam_im) / den
    ci = (li * lam_re - (lr - 1.0) * lam_im) / den
    bbr = cr[..., None] * b_re - ci[..., None] * b_im
    bbi = cr[..., None] * b_im + ci[..., None] * b_re
    eye = jnp.eye(SSM_GROUPS, dtype=F32)

    def in_mat(b):
        return jnp.einsum('drgph,gk->drghkp', b, eye).reshape(DEPTH, 2, SSM_WIDTH, SSM_LANES)

    def out_mat(cw):
        return jnp.einsum('drghp,gk->drkpgh', cw, eye).reshape(DEPTH, 2, SSM_LANES, SSM_WIDTH)

    bm = jnp.concatenate([in_mat(bbr), in_mat(bbi)], axis=-1).astype(BF16)
    cm = jnp.concatenate([out_mat(c_re), -out_mat(c_im)], axis=-2).astype(BF16)
    lrow = jnp.concatenate([lr.reshape(DEPTH, 2, SSM_LANES), li.reshape(DEPTH, 2, SSM_LANES)], axis=-1)
    lam = jnp.broadcast_to(lrow[:, :, None, :], (DEPTH, 2, SUBLANES, 2 * SSM_LANES))
    return bm, cm, lam


def _stream_layer(x, mod, pw, layer, batch, seq, rows_per_group, tables, ctx_kv, h0):
    latent = ctx_kv is not None
    groups = batch // SUBLANES
    x = _ffn_call(x, mod, pw, layer, 0, rows_per_group)
    q, k, v, kvar, vvar, f, us = _inproj_call(x, mod, pw, layer, tables, latent, batch, seq, rows_per_group)
    if latent:
        a = _lat_attn_call(q, kvar, vvar, ctx_kv[0], ctx_kv[1], tables['bias'], pw['sink'], layer, batch, seq)
    else:
        a = _ctx_attn_call(q, kvar, vvar, pw['sink'], layer, batch, seq)
    dm, wc = tables['dft'][seq]
    fz = _fnet_call(f, wc, dm, pw['w_fnet'], layer, batch, seq)
    yf, yb, fin = _s5_call(us.reshape(groups, seq * SUBLANES, SSM_WIDTH), h0, pw, layer, groups, seq)
    wide = (groups, seq, SUBLANES * SSM_WIDTH)
    x = _outproj_call(x, mod, a, fz, us, yf.reshape(wide), yb.reshape(wide), pw, layer, seq, rows_per_group)
    x = _ffn_call(x, mod, pw, layer, 1, rows_per_group)
    return x, (k, v, fin)


def kernel(x_prompt, x_sample, cache_k, cache_v, state_ssm_re, state_ssm_im, c, c_ctx, w_mod, b_mod, norm_g, ffn_w1, ffn_w3, ffn_w2, w_in, w_out, q_norm_g, k_norm_g, attn_sink, w_fnet, ssm_lambda_re, ssm_lambda_im, ssm_b_re, ssm_b_im, ssm_c_re, ssm_c_im, ssm_d, ssm_log_step, ssm_w_glu):
    batch, seq, _ = x_prompt.shape
    dec_batch, dec_seq, _ = x_sample.shape
    past = cache_k.shape[2]

    tables = {'e': _head_mean_matrix(), 'rope': _rope_tables(dec_seq), 'bias': _window_bias(past),
              'dft': {s: _dft_tables(s) for s in {seq, dec_seq}}}

    bm, cm, lam = _s5_matrices(ssm_lambda_re, ssm_lambda_im, ssm_b_re, ssm_b_im, ssm_c_re, ssm_c_im,
                               ssm_log_step)
    pw = {'g': norm_g.reshape(DEPTH, 3, 1, D_MODEL),
          'w1': ffn_w1.astype(BF16), 'w3': ffn_w3.astype(BF16), 'w2': ffn_w2.astype(BF16),
          'w_in': w_in.astype(BF16), 'w_out': w_out.astype(BF16),
          'qg': jnp.tile(q_norm_g, (1, N_HEADS)).reshape(DEPTH, 1, ATTN_WIDTH),
          'kg': jnp.tile(k_norm_g, (1, N_KV_HEADS)).reshape(DEPTH, 1, KV_WIDTH),
          'sink': jnp.broadcast_to(attn_sink[:, :, None], (DEPTH, N_HEADS, LANES)),
          'w_fnet': w_fnet.astype(BF16), 'bm': bm, 'cm': cm, 'lam': lam,
          'd': ssm_d.reshape(DEPTH, 1, SSM_WIDTH), 'w_glu': ssm_w_glu.astype(BF16)}

    cond = jnp.zeros((COND_ROWS, D_MODEL), F32).at[0].set(c_ctx).at[1:1 + dec_batch].set(c)
    mods = _mod_call(cond, w_mod, b_mod).reshape(DEPTH, COND_ROWS, N_MOD, D_MODEL)

    ctx_kv = (cache_k.reshape(dec_batch, DEPTH, past, KV_WIDTH), cache_v.reshape(dec_batch, DEPTH, past, KV_WIDTH))
    h0_lat = jnp.concatenate([state_ssm_re.reshape(dec_batch, DEPTH, 2, SSM_LANES),
                              state_ssm_im.reshape(dec_batch, DEPTH, 2, SSM_LANES)], axis=-1)
    h0_lat = h0_lat.reshape(dec_batch // SUBLANES, SUBLANES, DEPTH, 2, 2 * SSM_LANES).transpose(2, 0, 3, 1, 4)
    h0_ctx = jnp.zeros((batch // SUBLANES, 2, SUBLANES, 2 * SSM_LANES), F32)

    yp = x_prompt.reshape(batch * seq, D_MODEL)
    ys = x_sample.reshape(dec_batch * dec_seq, D_MODEL)
    ks, vs, sre, sim = [], [], [], []
    for l in range(DEPTH):
        yp, (k_l, v_l, fin) = _stream_layer(yp, mods[l, 0:1], pw, l, batch, seq, batch * seq, tables,
                                            None, h0_ctx)
        ks.append(k_l.reshape(batch, seq, N_KV_HEADS, HEAD_DIM))
        vs.append(v_l.reshape(batch, seq, N_KV_HEADS, HEAD_DIM))
        fin = fin.reshape(batch // SUBLANES, 2, SUBLANES, 2, SSM_GROUPS, SSM_STATE)
        fin = fin.transpose(3, 0, 2, 1, 4, 5).reshape(2, batch, 2, SSM_GROUPS, SSM_STATE)
        sre.append(fin[0])
        sim.append(fin[1])
        ys, _ = _stream_layer(ys, mods[l, 1:1 + dec_batch], pw, l, dec_batch, dec_seq, dec_seq, tables,
                              ctx_kv, h0_lat[l])

    return (yp.reshape(batch, seq, D_MODEL), ys.reshape(dec_batch, dec_seq, D_MODEL),
            jnp.stack(ks, axis=1), jnp.stack(vs, axis=1), jnp.stack(sre, axis=1), jnp.stack(sim, axis=1))
```

```python
import functools
import math

import numpy as np
import jax
import jax.numpy as jnp
from jax import lax
from jax.experimental import pallas as pl
from jax.experimental.pallas import tpu as pltpu

F32 = jnp.float32
BF16 = jnp.bfloat16

D_MODEL = 1024
DEPTH = 2
GRID_W = 64
HEAD_DIM = 64
N_HEADS = 8
N_KV_HEADS = 2
GQA_GROUP = N_HEADS // N_KV_HEADS
ATTN_WIDTH = N_HEADS * HEAD_DIM
KV_WIDTH = N_KV_HEADS * HEAD_DIM
WINDOW = 128
BLOCK = 128
ATTN_SCALE = HEAD_DIM ** -0.5
ROPE_BASE = 10000.0
NEG_INF = -1e30
LOG2E = math.log2(math.e)
FNET_HEADS = 4
FNET_HEAD_DIM = 64
FNET_WIDTH = FNET_HEADS * FNET_HEAD_DIM
SSM_WIDTH = 256
SSM_GROUP = 16
SSM_GROUPS = SSM_WIDTH // SSM_GROUP
SSM_STATE = 64
SSM_LANES = SSM_GROUPS * SSM_STATE
MIX_WIDTH = ATTN_WIDTH + FNET_WIDTH + SSM_WIDTH
Q_END = ATTN_WIDTH
K_END = Q_END + KV_WIDTH
V_END = K_END + KV_WIDTH
F_END = V_END + FNET_WIDTH
S_END = F_END + SSM_WIDTH
IN_WIDTH = S_END
D_FF = 2816
N_MOD = 9
NORM_EPS = 1e-6

LANES = 128
SUBLANES = 8
VMEM_LIMIT_BYTES = 56 * 1024 * 1024
TOKEN_TILE = 256
FNET_ROW_TILE = 512
SCAN_STEPS = 64
SCAN_LANE_CHUNK = 512
COND_ROWS = 16
LOCAL_SPAN = BLOCK + 2 * WINDOW
QM_WIDTH = N_HEADS * LANES
KVD_WIDTH = N_KV_HEADS * LANES
WIDE = SUBLANES * SSM_WIDTH


def _params(*sem):
    return pltpu.CompilerParams(dimension_semantics=sem, vmem_limit_bytes=VMEM_LIMIT_BYTES)


def _dot(a, b):
    return jnp.dot(a, b, preferred_element_type=F32)


def _norm_mod(x, g, shift, scale):
    y = x * lax.rsqrt(jnp.mean(x * x, axis=-1, keepdims=True) + NORM_EPS)
    return (y * g) * (1.0 + scale) + shift


def _const_spec(shape, lead=()):
    nd = len(shape)
    idx = tuple(lead) + (0,) * nd
    return pl.BlockSpec((None,) * len(lead) + tuple(shape), lambda *_: idx,
                        pipeline_mode=pl.Buffered(1))


def _mod_kernel(c_ref, w_ref, b_ref, o_ref):
    c = c_ref[...]
    a = (c * jax.nn.sigmoid(c)).astype(BF16)
    o_ref[0] = _dot(a, w_ref[0].astype(BF16)) + b_ref[0]


def _mod_call(cond, w_mod, b_mod):
    tn = 1024
    return pl.pallas_call(
        _mod_kernel,
        out_shape=jax.ShapeDtypeStruct((DEPTH, COND_ROWS, N_MOD * D_MODEL), F32),
        grid=(DEPTH, N_MOD * D_MODEL // tn),
        in_specs=[pl.BlockSpec((COND_ROWS, D_MODEL), lambda l, j: (0, 0)),
                  pl.BlockSpec((1, D_MODEL, tn), lambda l, j: (l, 0, j)),
                  pl.BlockSpec((1, 1, tn), lambda l, j: (l, 0, j))],
        out_specs=pl.BlockSpec((1, COND_ROWS, tn), lambda l, j: (l, 0, j)),
        compiler_params=_params("arbitrary", "arbitrary"),
        name="cond_mod",
    )(cond, w_mod, b_mod.reshape(DEPTH, 1, N_MOD * D_MODEL))


def _ffn_kernel(x_ref, mod_ref, g_ref, w1_ref, w3_ref, w2_ref, o_ref, *, k0):
    x = x_ref[...]
    m = mod_ref[0]
    h = _norm_mod(x, g_ref[...], m[k0:k0 + 1], m[k0 + 1:k0 + 2]).astype(BF16)
    a = _dot(h, w1_ref[...])
    b = _dot(h, w3_ref[...])
    t = ((a * jax.nn.sigmoid(a)) * b).astype(BF16)
    y = _dot(t, w2_ref[...])
    o_ref[...] = x + (0.5 * m[k0 + 2:k0 + 3]) * y


def _ffn_call(x, mod, pw, layer, half, rows_per_group):
    n = x.shape[0]
    tm = TOKEN_TILE
    tiles_per_group = rows_per_group // tm
    return pl.pallas_call(
        functools.partial(_ffn_kernel, k0=6 * half),
        out_shape=jax.ShapeDtypeStruct((n, D_MODEL), F32),
        grid=(n // tm,),
        in_specs=[pl.BlockSpec((tm, D_MODEL), lambda i: (i, 0)),
                  pl.BlockSpec((1, N_MOD, D_MODEL), lambda i: (i // tiles_per_group, 0, 0)),
                  _const_spec((1, D_MODEL), (layer, 2 * half)),
                  _const_spec((D_MODEL, D_FF), (layer, half)),
                  _const_spec((D_MODEL, D_FF), (layer, half)),
                  _const_spec((D_FF, D_MODEL), (layer, half))],
        out_specs=pl.BlockSpec((tm, D_MODEL), lambda i: (i, 0)),
        compiler_params=_params("arbitrary"),
        name="half_ffn",
    )(x, mod, pw['g'], pw['w1'], pw['w3'], pw['w2'])


def _head_norm(z, e, g):
    z2 = z * z
    hi = z2.astype(BF16)
    lo = (z2 - hi.astype(F32)).astype(BF16)
    ms = _dot(hi, e) + _dot(lo, e)
    return (z * lax.rsqrt(ms + NORM_EPS)) * g


def _low_half(shape):
    return lax.broadcasted_iota(jnp.int32, shape, 1) < HEAD_DIM


def _kv_slabs(k, v):
    low = _low_half(k.shape)
    kr = pltpu.roll(k, HEAD_DIM, 1)
    vr = pltpu.roll(v, HEAD_DIM, 1)
    kd = jnp.concatenate([jnp.where(low, k, kr), jnp.where(low, kr, k)], axis=1)
    vd = jnp.concatenate([jnp.where(low, v, 1.0), jnp.where(low, vr, 1.0)], axis=1)
    return kd.astype(BF16), vd.astype(BF16)


def _inproj_kernel(*refs, rope):
    if rope:
        (x_ref, mod_ref, g_ref, w_ref, qg_ref, kg_ref, e_ref, cos_ref, sa_ref, sb_ref,
         qm_ref, k_ref, v_ref, kd_ref, vd_ref, f_ref, s_ref) = refs
    else:
        (x_ref, mod_ref, g_ref, w_ref, qg_ref, kg_ref, e_ref,
         qm_ref, k_ref, v_ref, kd_ref, vd_ref, f_ref, s_ref) = refs
    m = mod_ref[0]
    h = _norm_mod(x_ref[...], g_ref[...], m[3:4], m[4:5]).astype(BF16)
    u = _dot(h, w_ref[...])
    q = _head_norm(u[:, :Q_END], e_ref[...], qg_ref[...])
    k = _head_norm(u[:, Q_END:K_END], e_ref[0:KV_WIDTH, 0:KV_WIDTH], kg_ref[...])
    v = u[:, K_END:V_END]
    if rope:
        cos, sa, sb = cos_ref[...], sa_ref[...], sb_ref[...]

        def rot(z):
            return z * cos + pltpu.roll(z, LANES - 16, 1) * sa + pltpu.roll(z, 16, 1) * sb
    else:
        def rot(z):
            return z

    low = _low_half((q.shape[0], LANES))
    for i in range(ATTN_WIDTH // LANES):
        qs = rot(q[:, i * LANES:(i + 1) * LANES]) * (ATTN_SCALE * LOG2E)
        qm_ref[:, (2 * i) * LANES:(2 * i + 1) * LANES] = jnp.where(low, qs, 0.0).astype(BF16)
        qm_ref[:, (2 * i + 1) * LANES:(2 * i + 2) * LANES] = jnp.where(low, 0.0, qs).astype(BF16)
    k = rot(k)
    k_ref[...] = k
    v_ref[...] = v
    kd_ref[...], vd_ref[...] = _kv_slabs(k, v)
    f_ref[...] = u[:, V_END:F_END].astype(BF16)
    s_ref[...] = u[:, F_END:S_END]


def _wide_spec(tm, tiles_per_seq):
    def index(i):
        b = i // tiles_per_seq
        return (b // SUBLANES, i % tiles_per_seq, b % SUBLANES)
    return pl.BlockSpec((None, tm, SSM_WIDTH), index)


def _inproj_call(x, mod, pw, layer, tables, rope, batch, seq, rows_per_group):
    n = x.shape[0]
    tm = TOKEN_TILE
    tiles_per_group = rows_per_group // tm
    tiles_per_seq = seq // tm
    in_specs = [pl.BlockSpec((tm, D_MODEL), lambda i: (i, 0)),
                pl.BlockSpec((1, N_MOD, D_MODEL), lambda i: (i // tiles_per_group, 0, 0)),
                _const_spec((1, D_MODEL), (layer, 1)),
                _const_spec((D_MODEL, IN_WIDTH), (layer,)),
                _const_spec((1, ATTN_WIDTH), (layer,)),
                _const_spec((1, KV_WIDTH), (layer,)),
                _const_spec((ATTN_WIDTH, ATTN_WIDTH))]
    args = [x, mod, pw['g'], pw['w_in'], pw['qg'], pw['kg'], tables['e']]
    if rope:
        in_specs += [pl.BlockSpec((tm, LANES), lambda i: (i % tiles_per_seq, 0))] * 3
        args += list(tables['rope'])

    def row(w):
        return pl.BlockSpec((tm, w), lambda i: (i, 0))

    out_shape = [jax.ShapeDtypeStruct((n, QM_WIDTH), BF16),
                 jax.ShapeDtypeStruct((n, KV_WIDTH), F32),
                 jax.ShapeDtypeStruct((n, KV_WIDTH), F32),
                 jax.ShapeDtypeStruct((n, KVD_WIDTH), BF16),
                 jax.ShapeDtypeStruct((n, KVD_WIDTH), BF16),
                 jax.ShapeDtypeStruct((n, FNET_WIDTH), BF16),
                 jax.ShapeDtypeStruct((batch // SUBLANES, seq, WIDE), F32)]
    out_specs = [row(QM_WIDTH), row(KV_WIDTH), row(KV_WIDTH), row(KVD_WIDTH), row(KVD_WIDTH),
                 row(FNET_WIDTH), _wide_spec(tm, tiles_per_seq)]
    return pl.pallas_call(
        functools.partial(_inproj_kernel, rope=rope),
        out_shape=out_shape,
        grid=(n // tm,),
        in_specs=in_specs,
        out_specs=out_specs,
        compiler_params=_params("arbitrary"),
        name="in_proj",
    )(*args)


def _attend(qm_ref, o_ref, sink_ref, kv, bias, nq):
    low = _low_half((nq, LANES))
    for h in range(N_KV_HEADS):
        j0 = GQA_GROUP * h
        qst = jnp.concatenate([qm_ref[:, (j0 + g) * LANES:(j0 + g + 1) * LANES]
                               for g in range(GQA_GROUP)], axis=0)
        sink = jnp.concatenate([jnp.broadcast_to(sink_ref[j0 + g:j0 + g + 1, :], (nq, LANES))
                                for g in range(GQA_GROUP)], axis=0) * LOG2E
        kd, vd = kv(h)
        s = lax.dot_general(qst, kd, (((1,), (1,)), ((), ())), preferred_element_type=F32)
        if bias is not None:
            s = s + bias
        m = jnp.maximum(jnp.max(s, axis=-1, keepdims=True), sink)
        e = jnp.exp2(s - jnp.concatenate([m] * (s.shape[1] // LANES), axis=1)).astype(BF16)
        o = _dot(e, vd)
        es = jnp.exp2(sink - m)
        for t in range(GQA_GROUP // 2):
            r0, r1 = (2 * t) * nq, (2 * t + 1) * nq
            oe, oo = o[r0:r0 + nq], o[r1:r1 + nq]
            num = jnp.where(low, oe, pltpu.roll(oo, HEAD_DIM, 1))
            den = jnp.where(low, pltpu.roll(oe, HEAD_DIM, 1), oo) + jnp.where(low, es[r0:r0 + nq], es[r1:r1 + nq])
            c0 = (j0 // 2 + t) * LANES
            o_ref[:, c0:c0 + LANES] = (num * (1.0 / den)).astype(o_ref.dtype)


def _ctx_attn_kernel(qm_ref, kd_ref, vd_ref, sink_ref, o_ref, *, seq):
    def kv(h):
        return (kd_ref[:, h * LANES:(h + 1) * LANES], vd_ref[:, h * LANES:(h + 1) * LANES])
    _attend(qm_ref, o_ref, sink_ref, kv, None, seq)


def _ctx_attn_call(qm, kd, vd, sink, layer, batch, seq):
    return pl.pallas_call(
        functools.partial(_ctx_attn_kernel, seq=seq),
        out_shape=jax.ShapeDtypeStruct((batch * seq, ATTN_WIDTH), BF16),
        grid=(batch,),
        in_specs=[pl.BlockSpec((seq, QM_WIDTH), lambda b: (b, 0)),
                  pl.BlockSpec((seq, KVD_WIDTH), lambda b: (b, 0)),
                  pl.BlockSpec((seq, KVD_WIDTH), lambda b: (b, 0)),
                  pl.BlockSpec((None, N_HEADS, LANES), lambda b: (layer, 0, 0))],
        out_specs=pl.BlockSpec((seq, ATTN_WIDTH), lambda b: (b, 0)),
        compiler_params=_params("arbitrary"),
        name="ctx_attn",
    )(qm, kd, vd, sink)


def _lat_attn_kernel(qm_ref, kl_ref, vl_ref, kc_ref, vc_ref, bias_ref, sink_ref, o_ref,
                     kc_sc, vc_sc, *, seq):
    n = pl.program_id(1)

    @pl.when(n == 0)
    def _():
        kc_sc[...], vc_sc[...] = _kv_slabs(kc_ref[...], vc_ref[...])

    start = pl.multiple_of(jnp.clip(n * BLOCK - WINDOW, 0, seq - LOCAL_SPAN), BLOCK)

    def kv(h):
        cols = slice(h * LANES, (h + 1) * LANES)
        return (jnp.concatenate([kc_sc[:, cols], kl_ref[pl.ds(start, LOCAL_SPAN), cols]], axis=0),
                jnp.concatenate([vc_sc[:, cols], vl_ref[pl.ds(start, LOCAL_SPAN), cols]], axis=0))

    _attend(qm_ref, o_ref, sink_ref, kv, bias_ref[...], BLOCK)


def _lat_attn_call(qm, kd, vd, kctx, vctx, bias, sink, layer, batch, seq):
    past = kctx.shape[2]
    nb = seq // BLOCK
    nkeys = past + LOCAL_SPAN

    def bias_index(b, n):
        return (jnp.where(n == 0, 0, jnp.where(n == nb - 1, 2, 1)), 0, 0)

    return pl.pallas_call(
        functools.partial(_lat_attn_kernel, seq=seq),
        out_shape=jax.ShapeDtypeStruct((batch * seq, ATTN_WIDTH), BF16),
        grid=(batch, nb),
        in_specs=[pl.BlockSpec((BLOCK, QM_WIDTH), lambda b, n: (b * nb + n, 0)),
                  pl.BlockSpec((seq, KVD_WIDTH), lambda b, n: (b, 0)),
                  pl.BlockSpec((seq, KVD_WIDTH), lambda b, n: (b, 0)),
                  pl.BlockSpec((None, None, past, KV_WIDTH), lambda b, n: (b, layer, 0, 0)),
                  pl.BlockSpec((None, None, past, KV_WIDTH), lambda b, n: (b, layer, 0, 0)),
                  pl.BlockSpec((None, GQA_GROUP * BLOCK, nkeys), bias_index),
                  pl.BlockSpec((None, N_HEADS, LANES), lambda b, n: (layer, 0, 0))],
        out_specs=pl.BlockSpec((BLOCK, ATTN_WIDTH), lambda b, n: (b * nb + n, 0)),
        scratch_shapes=[pltpu.VMEM((past, KVD_WIDTH), BF16), pltpu.VMEM((past, KVD_WIDTH), BF16)],
        compiler_params=_params("arbitrary", "arbitrary"),
        name="lat_attn",
    )(qm, kd, vd, kctx, vctx, bias, sink)


def _fnet_kernel(x_ref, wc_ref, dm_ref, wf_ref, o_ref, y_sc, dm_sc, *, seq):
    r = pl.program_id(0)
    b = pl.program_id(1)

    @pl.when(r == 0)
    def _():
        y = _dot(x_ref[...], wc_ref[...].astype(BF16))
        y_sc[b, 0:seq, :] = y[:, :FNET_WIDTH].astype(BF16)
        y_sc[b, seq:2 * seq, :] = y[:, FNET_WIDTH:].astype(BF16)

    @pl.when(b == 0)
    def _():
        dm_sc[...] = dm_ref[...].astype(BF16)

    z = _dot(dm_sc[...], y_sc[b]) * ((seq * FNET_HEAD_DIM) ** -0.5)
    o_ref[...] = _dot(z.astype(BF16), wf_ref[...]).astype(o_ref.dtype)


def _fnet_call(x, wc, dm, wf, layer, batch, seq):
    tl = min(seq, FNET_ROW_TILE)
    nt = seq // tl
    return pl.pallas_call(
        functools.partial(_fnet_kernel, seq=seq),
        out_shape=jax.ShapeDtypeStruct((batch * seq, FNET_WIDTH), BF16),
        grid=(nt, batch),
        in_specs=[pl.BlockSpec((seq, FNET_WIDTH), lambda r, b: (jnp.where(r == 0, b, 0), 0)),
                  pl.BlockSpec((FNET_WIDTH, 2 * FNET_WIDTH), lambda r, b: (0, 0)),
                  pl.BlockSpec((tl, 2 * seq), lambda r, b: (r, 0)),
                  pl.BlockSpec((None, FNET_WIDTH, FNET_WIDTH), lambda r, b: (layer, 0, 0))],
        out_specs=pl.BlockSpec((tl, FNET_WIDTH), lambda r, b: (b * nt + r, 0)),
        scratch_shapes=[pltpu.VMEM((batch, 2 * seq, FNET_WIDTH), BF16),
                        pltpu.VMEM((tl, 2 * seq), BF16)],
        compiler_params=_params("arbitrary", "arbitrary"),
        name="fourier_mix",
    )(x, wc, dm, wf)


def _s5_kernel(xf_ref, xb_ref, h0_ref, bm_ref, cm_ref, lam_ref, yf_ref, yb_ref, fin_ref,
               xs, buf, ys, carry, *, steps, nchunks):
    c = pl.program_id(1)
    nslab = SSM_WIDTH // LANES

    @pl.when(c == 0)
    def _():
        carry[...] = h0_ref[0]

    for d, x_ref in enumerate((xf_ref, xb_ref)):
        for b in range(SUBLANES):
            for sl in range(nslab):
                c0 = b * SSM_WIDTH + sl * LANES
                xs[d * nslab + sl, pl.ds(b, steps, stride=SUBLANES), :] = x_ref[:, c0:c0 + LANES]
        x = jnp.concatenate([xs[d * nslab + sl] for sl in range(nslab)], axis=1)
        buf[d] = _dot(x.astype(BF16), bm_ref[d])

    w = SCAN_LANE_CHUNK
    for lc in range(SSM_LANES // w):
        re = slice(lc * w, (lc + 1) * w)
        im = slice(SSM_LANES + lc * w, SSM_LANES + (lc + 1) * w)
        lam = [(lam_ref[d, :, re], lam_ref[d, :, im]) for d in range(2)]

        def body(t, st, re=re, im=im, lam=lam):
            out = []
            for d in range(2):
                tt = t if d == 0 else steps - 1 - t
                r0 = pl.multiple_of(tt * SUBLANES, SUBLANES)
                sr, si = st[2 * d], st[2 * d + 1]
                lr, li = lam[d]
                nr = (lr * sr - li * si) + buf[d, pl.ds(r0, SUBLANES), re]
                ni = (lr * si + li * sr) + buf[d, pl.ds(r0, SUBLANES), im]
                buf[d, pl.ds(r0, SUBLANES), re] = nr
                buf[d, pl.ds(r0, SUBLANES), im] = ni
                out += [nr, ni]
            return tuple(out)

        init = (carry[0, :, re], carry[0, :, im], carry[1, :, re], carry[1, :, im])
        fr, fi, br, bi = lax.fori_loop(0, steps, body, init, unroll=2)
        carry[0, :, re] = fr
        carry[0, :, im] = fi
        carry[1, :, re] = br
        carry[1, :, im] = bi

    for d, y_ref in enumerate((yf_ref, yb_ref)):
        y = _dot(buf[d].astype(BF16), cm_ref[d])
        for sl in range(nslab):
            ys[d * nslab + sl] = y[:, sl * LANES:(sl + 1) * LANES]
        for b in range(SUBLANES):
            for sl in range(nslab):
                c0 = b * SSM_WIDTH + sl * LANES
                y_ref[:, c0:c0 + LANES] = ys[d * nslab + sl, pl.ds(b, steps, stride=SUBLANES), :]

    @pl.when(c == nchunks - 1)
    def _():
        fin_ref[0] = carry[...]


def _s5_call(x_wide, h0, pw, layer, groups, seq):
    steps = SCAN_STEPS
    rows = steps * SUBLANES
    nchunks = seq // steps
    width = 2 * SSM_LANES
    nslab = SSM_WIDTH // LANES
    return pl.pallas_call(
        functools.partial(_s5_kernel, steps=steps, nchunks=nchunks),
        out_shape=[jax.ShapeDtypeStruct((groups, seq, WIDE), F32),
                   jax.ShapeDtypeStruct((groups, seq, WIDE), F32),
                   jax.ShapeDtypeStruct((groups, 2, SUBLANES, width), F32)],
        grid=(groups, nchunks),
        in_specs=[pl.BlockSpec((None, steps, WIDE), lambda g, c: (g, c, 0)),
                  pl.BlockSpec((None, steps, WIDE), lambda g, c: (g, nchunks - 1 - c, 0)),
                  pl.BlockSpec((1, 2, SUBLANES, width), lambda g, c: (g, 0, 0, 0)),
                  pl.BlockSpec((None, 2, SSM_WIDTH, width), lambda g, c: (layer, 0, 0, 0)),
                  pl.BlockSpec((None, 2, width, SSM_WIDTH), lambda g, c: (layer, 0, 0, 0)),
                  pl.BlockSpec((None, 2, SUBLANES, width), lambda g, c: (layer, 0, 0, 0))],
        out_specs=[pl.BlockSpec((None, steps, WIDE), lambda g, c: (g, c, 0)),
                   pl.BlockSpec((None, steps, WIDE), lambda g, c: (g, nchunks - 1 - c, 0)),
                   pl.BlockSpec((1, 2, SUBLANES, width), lambda g, c: (g, 0, 0, 0))],
        scratch_shapes=[pltpu.VMEM((2 * nslab, rows, LANES), F32),
                        pltpu.VMEM((2, rows, width), F32),
                        pltpu.VMEM((2 * nslab, rows, LANES), F32),
                        pltpu.VMEM((2, SUBLANES, width), F32)],
        compiler_params=_params("arbitrary", "arbitrary"),
        name="s5_scan",
    )(x_wide, x_wide, h0, pw['bm'], pw['cm'], pw['lam'])


def _outproj_kernel(x_ref, mod_ref, a_ref, f_ref, us_ref, yf_ref, yb_ref, d_ref, wg_ref, wo_ref,
                    o_ref):
    ys = jax.nn.gelu((d_ref[...] * us_ref[...] + yf_ref[...]) + yb_ref[...])
    gl = _dot(ys.astype(BF16), wg_ref[...])
    so = gl[:, :SSM_WIDTH] * jax.nn.sigmoid(gl[:, SSM_WIDTH:])
    mixed = jnp.concatenate([a_ref[...], f_ref[...], so.astype(BF16)], axis=1)
    y = _dot(mixed, wo_ref[...])
    o_ref[...] = x_ref[...] + mod_ref[0][5:6] * y


def _outproj_call(x, mod, a, f, us, yf, yb, pw, layer, seq, rows_per_group):
    n = x.shape[0]
    tm = TOKEN_TILE
    tiles_per_group = rows_per_group // tm
    wide = _wide_spec(tm, seq // tm)

    def row(w):
        return pl.BlockSpec((tm, w), lambda i: (i, 0))

    return pl.pallas_call(
        _outproj_kernel,
        out_shape=jax.ShapeDtypeStruct((n, D_MODEL), F32),
        grid=(n // tm,),
        in_specs=[row(D_MODEL),
                  pl.BlockSpec((1, N_MOD, D_MODEL), lambda i: (i // tiles_per_group, 0, 0)),
                  row(ATTN_WIDTH), row(FNET_WIDTH), wide, wide, wide,
                  _const_spec((1, SSM_WIDTH), (layer,)),
                  _const_spec((SSM_WIDTH, 2 * SSM_WIDTH), (layer,)),
                  _const_spec((MIX_WIDTH, D_MODEL), (layer,))],
        out_specs=row(D_MODEL),
        compiler_params=_params("arbitrary"),
        name="out_proj",
    )(x, mod, a, f, us, yf, yb, pw['d'], pw['w_glu'], pw['w_out'])


def _rope_tables(seq):
    pos = np.arange(seq)
    row, col = pos // GRID_W, pos % GRID_W
    quarter = HEAD_DIM // 4
    inv_freq = 1.0 / (ROPE_BASE ** (np.arange(quarter, dtype=np.float64) * 2.0 / (HEAD_DIM // 2)))
    lane = np.arange(LANES)
    in_head = lane % HEAD_DIM
    p = np.where((in_head < HEAD_DIM // 2)[None, :], row[:, None], col[:, None]).astype(np.float64)
    ang = p * inv_freq[lane % quarter][None, :]
    first = ((lane % (HEAD_DIM // 2)) < quarter)[None, :]
    cos = np.cos(ang)
    sa = np.where(first, -np.sin(ang), 0.0)
    sb = np.where(first, 0.0, np.sin(ang))
    return tuple(jnp.asarray(t, dtype=F32) for t in (cos, sa, sb))


def _dft_tables(seq):
    kl = np.outer(np.arange(seq), np.arange(seq)) % seq
    ang = 2.0 * np.pi * kl / seq
    dm = np.concatenate([np.cos(ang), -np.sin(ang)], axis=1)
    mc = np.outer(np.arange(FNET_HEAD_DIM), np.arange(FNET_HEAD_DIM)) % FNET_HEAD_DIM
    a64 = 2.0 * np.pi * mc / FNET_HEAD_DIM
    eye = np.eye(FNET_HEADS)
    wc = np.concatenate([np.kron(eye, np.cos(a64)), np.kron(eye, np.sin(a64))], axis=1)
    return jnp.asarray(dm, dtype=F32), jnp.asarray(wc, dtype=F32)


def _head_mean_matrix():
    return jnp.asarray(np.kron(np.eye(N_HEADS), np.full((HEAD_DIM, HEAD_DIM), 1.0 / HEAD_DIM)), dtype=BF16)


def _window_bias(past):
    i = np.arange(GQA_GROUP * BLOCK)[:, None] % BLOCK
    j = np.arange(LOCAL_SPAN)[None, :]
    out = np.zeros((3, GQA_GROUP * BLOCK, past + LOCAL_SPAN), np.float32)
    for v, off in enumerate((0, WINDOW, 2 * WINDOW)):
        out[v, :, past:] = np.where(np.abs(j - off - i) <= WINDOW, 0.0, NEG_INF)
    return jnp.asarray(out)


def _s5_matrices(lam_re, lam_im, b_re, b_im, c_re, c_im, log_step):
    step = jnp.exp(log_step)[..., None]
    mag = jnp.exp(lam_re * step)
    lr = mag * jnp.cos(lam_im * step)
    li = mag * jnp.sin(lam_im * step)
    den = lam_re * lam_re + lam_im * lam_im
    cr = ((lr - 1.0) * lam_re + li * lam_im) / den
    ci = (li * lam_re - (lr - 1.0) * lam_im) / den
    bbr = cr[..., None] * b_re - ci[..., None] * b_im
    bbi = cr[..., None] * b_im + ci[..., None] * b_re
    eye = jnp.eye(SSM_GROUPS, dtype=F32)

    def in_mat(b):
        return jnp.einsum('drgph,gk->drghkp', b, eye).reshape(DEPTH, 2, SSM_WIDTH, SSM_LANES)

    def out_mat(cw):
        return jnp.einsum('drghp,gk->drkpgh', cw, eye).reshape(DEPTH, 2, SSM_LANES, SSM_WIDTH)

    bm = jnp.concatenate([in_mat(bbr), in_mat(bbi)], axis=-1).astype(BF16)
    cm = jnp.concatenate([out_mat(c_re), -out_mat(c_im)], axis=-2).astype(BF16)
    lrow = jnp.concatenate([lr.reshape(DEPTH, 2, SSM_LANES), li.reshape(DEPTH, 2, SSM_LANES)], axis=-1)
    lam = jnp.broadcast_to(lrow[:, :, None, :], (DEPTH, 2, SUBLANES, 2 * SSM_LANES))
    return bm, cm, lam


def _stream_layer(x, mod, pw, layer, batch, seq, rows_per_group, tables, ctx_kv, h0):
    latent = ctx_kv is not None
    groups = batch // SUBLANES
    x = _ffn_call(x, mod, pw, layer, 0, rows_per_group)
    qm, k, v, kd, vd, f, us = _inproj_call(x, mod, pw, layer, tables, latent, batch, seq, rows_per_group)
    if latent:
        a = _lat_attn_call(qm, kd, vd, ctx_kv[0], ctx_kv[1], tables['bias'], pw['sink'], layer, batch, seq)
    else:
        a = _ctx_attn_call(qm, kd, vd, pw['sink'], layer, batch, seq)
    dm, wc = tables['dft'][seq]
    fz = _fnet_call(f, wc, dm, pw['w_fnet'], layer, batch, seq)
    yf, yb, fin = _s5_call(us, h0, pw, layer, groups, seq)
    x = _outproj_call(x, mod, a, fz, us, yf, yb, pw, layer, seq, rows_per_group)
    x = _ffn_call(x, mod, pw, layer, 1, rows_per_group)
    return x, (k, v, fin)


def kernel(x_prompt, x_sample, cache_k, cache_v, state_ssm_re, state_ssm_im, c, c_ctx, w_mod, b_mod, norm_g, ffn_w1, ffn_w3, ffn_w2, w_in, w_out, q_norm_g, k_norm_g, attn_sink, w_fnet, ssm_lambda_re, ssm_lambda_im, ssm_b_re, ssm_b_im, ssm_c_re, ssm_c_im, ssm_d, ssm_log_step, ssm_w_glu):
    batch, seq, _ = x_prompt.shape
    dec_batch, dec_seq, _ = x_sample.shape
    past = cache_k.shape[2]

    tables = {'e': _head_mean_matrix(), 'rope': _rope_tables(dec_seq), 'bias': _window_bias(past),
              'dft': {s: _dft_tables(s) for s in {seq, dec_seq}}}

    bm, cm, lam = _s5_matrices(ssm_lambda_re, ssm_lambda_im, ssm_b_re, ssm_b_im, ssm_c_re, ssm_c_im,
                               ssm_log_step)
    pw = {'g': norm_g.reshape(DEPTH, 3, 1, D_MODEL),
          'w1': ffn_w1.astype(BF16), 'w3': ffn_w3.astype(BF16), 'w2': ffn_w2.astype(BF16),
          'w_in': w_in.astype(BF16), 'w_out': w_out.astype(BF16),
          'qg': jnp.tile(q_norm_g, (1, N_HEADS)).reshape(DEPTH, 1, ATTN_WIDTH),
          'kg': jnp.tile(k_norm_g, (1, N_KV_HEADS)).reshape(DEPTH, 1, KV_WIDTH),
          'sink': jnp.broadcast_to(attn_sink[:, :, None], (DEPTH, N_HEADS, LANES)),
          'w_fnet': w_fnet.astype(BF16), 'bm': bm, 'cm': cm, 'lam': lam,
          'd': ssm_d.reshape(DEPTH, 1, SSM_WIDTH), 'w_glu': ssm_w_glu.astype(BF16)}

    cond = jnp.zeros((COND_ROWS, D_MODEL), F32).at[0].set(c_ctx).at[1:1 + dec_batch].set(c)
    mods = _mod_call(cond, w_mod, b_mod).reshape(DEPTH, COND_ROWS, N_MOD, D_MODEL)

    ctx_kv = (cache_k.reshape(dec_batch, DEPTH, past, KV_WIDTH), cache_v.reshape(dec_batch, DEPTH, past, KV_WIDTH))
    h0_lat = jnp.concatenate([state_ssm_re.reshape(dec_batch, DEPTH, 2, SSM_LANES),
                              state_ssm_im.reshape(dec_batch, DEPTH, 2, SSM_LANES)], axis=-1)
    h0_lat = h0_lat.reshape(dec_batch // SUBLANES, SUBLANES, DEPTH, 2, 2 * SSM_LANES).transpose(2, 0, 3, 1, 4)
    h0_ctx = jnp.zeros((batch // SUBLANES, 2, SUBLANES, 2 * SSM_LANES), F32)

    yp = x_prompt.reshape(batch * seq, D_MODEL)
    ys = x_sample.reshape(dec_batch * dec_seq, D_MODEL)
    ks, vs, sre, sim = [], [], [], []
    for l in range(DEPTH):
        yp, (k_l, v_l, fin) = _stream_layer(yp, mods[l, 0:1], pw, l, batch, seq, batch * seq, tables,
                                            None, h0_ctx)
        ks.append(k_l.reshape(batch, seq, N_KV_HEADS, HEAD_DIM))
        vs.append(v_l.reshape(batch, seq, N_KV_HEADS, HEAD_DIM))
        fin = fin.reshape(batch // SUBLANES, 2, SUBLANES, 2, SSM_GROUPS, SSM_STATE)
        fin = fin.transpose(3, 0, 2, 1, 4, 5).reshape(2, batch, 2, SSM_GROUPS, SSM_STATE)
        sre.append(fin[0])
        sim.append(fin[1])
        ys, _ = _stream_layer(ys, mods[l, 1:1 + dec_batch], pw, l, dec_batch, dec_seq, dec_seq, tables,
                              ctx_kv, h0_lat[l])

    return (yp.reshape(batch, seq, D_MODEL), ys.reshape(dec_batch, dec_seq, D_MODEL),
            jnp.stack(ks, axis=1), jnp.stack(vs, axis=1), jnp.stack(sre, axis=1), jnp.stack(sim, axis=1))
```

```python
import functools
import math

import numpy as np
import jax
import jax.numpy as jnp
from jax import lax
from jax.experimental import pallas as pl
from jax.experimental.pallas import tpu as pltpu

F32 = jnp.float32
BF16 = jnp.bfloat16

D_MODEL = 1024
DEPTH = 2
GRID_W = 64
HEAD_DIM = 64
N_HEADS = 8
N_KV_HEADS = 2
GQA_GROUP = N_HEADS // N_KV_HEADS
ATTN_WIDTH = N_HEADS * HEAD_DIM
KV_WIDTH = N_KV_HEADS * HEAD_DIM
WINDOW = 128
BLOCK = 128
ATTN_SCALE = HEAD_DIM ** -0.5
ROPE_BASE = 10000.0
NEG_INF = -1e30
LOG2E = math.log2(math.e)
FNET_HEADS = 4
FNET_HEAD_DIM = 64
FNET_WIDTH = FNET_HEADS * FNET_HEAD_DIM
SSM_WIDTH = 256
SSM_GROUP = 16
SSM_GROUPS = SSM_WIDTH // SSM_GROUP
SSM_STATE = 64
SSM_LANES = SSM_GROUPS * SSM_STATE
MIX_WIDTH = ATTN_WIDTH + FNET_WIDTH + SSM_WIDTH
Q_END = ATTN_WIDTH
K_END = Q_END + KV_WIDTH
V_END = K_END + KV_WIDTH
F_END = V_END + FNET_WIDTH
S_END = F_END + SSM_WIDTH
IN_WIDTH = S_END
D_FF = 2816
N_MOD = 9
NORM_EPS = 1e-6

LANES = 128
SUBLANES = 8
VMEM_LIMIT_BYTES = 56 * 1024 * 1024
TOKEN_TILE = 512
FF_CHUNKS = 2
FF_CHUNK = D_FF // FF_CHUNKS
ATTN_BLOCKS_PER_STEP = 2
SCAN_STEPS = 64
SCAN_LANE_CHUNK = 512
COND_ROWS = 16
LOCAL_SPAN = BLOCK + 2 * WINDOW
QM_WIDTH = N_HEADS * LANES
KD_WIDTH = N_KV_HEADS * LANES
VD_WIDTH = N_KV_HEADS * 2 * LANES
WIDE = SUBLANES * SSM_WIDTH


def _params(*sem):
    return pltpu.CompilerParams(dimension_semantics=sem, vmem_limit_bytes=VMEM_LIMIT_BYTES)


def _dot(a, b):
    return jnp.dot(a, b, preferred_element_type=F32)


def _norm_mod(x, g, shift, scale):
    y = x * lax.rsqrt(jnp.mean(x * x, axis=-1, keepdims=True) + NORM_EPS)
    return (y * g) * (1.0 + scale) + shift


def _const_spec(shape, lead=()):
    nd = len(shape)
    idx = tuple(lead) + (0,) * nd
    return pl.BlockSpec((None,) * len(lead) + tuple(shape), lambda *_: idx,
                        pipeline_mode=pl.Buffered(1))


def _row_spec(tm, width):
    return pl.BlockSpec((tm, width), lambda i: (i, 0))


def _mod_spec(tiles_per_group):
    return pl.BlockSpec((1, N_MOD, D_MODEL), lambda i: (i // tiles_per_group, 0, 0))


def _wide_spec(tm, tiles_per_seq):
    def index(i):
        b = i // tiles_per_seq
        return (b // SUBLANES, i % tiles_per_seq, b % SUBLANES)
    return pl.BlockSpec((None, tm, SSM_WIDTH), index)


def _mod_kernel(c_ref, w_ref, b_ref, o_ref):
    c = c_ref[...]
    a = (c * jax.nn.sigmoid(c)).astype(BF16)
    o_ref[0] = _dot(a, w_ref[0].astype(BF16)) + b_ref[0]


def _mod_call(cond, w_mod, b_mod):
    tn = 1024
    return pl.pallas_call(
        _mod_kernel,
        out_shape=jax.ShapeDtypeStruct((DEPTH, COND_ROWS, N_MOD * D_MODEL), F32),
        grid=(DEPTH, N_MOD * D_MODEL // tn),
        in_specs=[pl.BlockSpec((COND_ROWS, D_MODEL), lambda l, j: (0, 0)),
                  pl.BlockSpec((1, D_MODEL, tn), lambda l, j: (l, 0, j)),
                  pl.BlockSpec((1, 1, tn), lambda l, j: (l, 0, j))],
        out_specs=pl.BlockSpec((1, COND_ROWS, tn), lambda l, j: (l, 0, j)),
        compiler_params=_params("arbitrary", "arbitrary"),
        name="cond_mod",
    )(cond, w_mod, b_mod.reshape(DEPTH, 1, N_MOD * D_MODEL))


def _half_ffn(x, m, k0, g, w1_ref, w3_ref, w2_ref):
    h = _norm_mod(x, g, m[k0:k0 + 1], m[k0 + 1:k0 + 2]).astype(BF16)
    y = None
    for c in range(FF_CHUNKS):
        cols = slice(c * FF_CHUNK, (c + 1) * FF_CHUNK)
        a = _dot(h, w1_ref[:, cols])
        b = _dot(h, w3_ref[:, cols])
        t = ((a * jax.nn.sigmoid(a)) * b).astype(BF16)
        yc = _dot(t, w2_ref[cols, :])
        y = yc if y is None else y + yc
    return x + (0.5 * m[k0 + 2:k0 + 3]) * y


def _head_norm(z, e, g):
    z2 = z * z
    hi = z2.astype(BF16)
    lo = (z2 - hi.astype(F32)).astype(BF16)
    ms = _dot(hi, e) + _dot(lo, e)
    return (z * lax.rsqrt(ms + NORM_EPS)) * g


def _low_half(shape):
    return lax.broadcasted_iota(jnp.int32, shape, 1) < HEAD_DIM


def _kv_slabs(k, v):
    low = _low_half(k.shape)
    kr = pltpu.roll(k, HEAD_DIM, 1)
    vr = pltpu.roll(v, HEAD_DIM, 1)
    kd = jnp.concatenate([jnp.where(low, k, kr), jnp.where(low, kr, k)], axis=1)
    vd = jnp.concatenate([jnp.where(low, v, 1.0), jnp.where(low, 1.0, vr),
                          jnp.where(low, vr, 1.0), jnp.where(low, 1.0, v)], axis=1)
    return kd.astype(BF16), vd.astype(BF16)


def _ffn_in_kernel(*refs, rope):
    if rope:
        (x_ref, mod_ref, g0_ref, g1_ref, w1_ref, w3_ref, w2_ref, w_ref, qg_ref, kg_ref, e_ref,
         cos_ref, sa_ref, sb_ref, x1_ref, qm_ref, k_ref, v_ref, kd_ref, vd_ref, f_ref, s_ref) = refs
    else:
        (x_ref, mod_ref, g0_ref, g1_ref, w1_ref, w3_ref, w2_ref, w_ref, qg_ref, kg_ref, e_ref,
         x1_ref, qm_ref, k_ref, v_ref, kd_ref, vd_ref, f_ref, s_ref) = refs
    m = mod_ref[0]
    x1 = _half_ffn(x_ref[...], m, 0, g0_ref[...], w1_ref, w3_ref, w2_ref)
    x1_ref[...] = x1
    h = _norm_mod(x1, g1_ref[...], m[3:4], m[4:5]).astype(BF16)
    u = _dot(h, w_ref[...])
    q = _head_norm(u[:, :Q_END], e_ref[...], qg_ref[...])
    k = _head_norm(u[:, Q_END:K_END], e_ref[0:KV_WIDTH, 0:KV_WIDTH], kg_ref[...])
    v = u[:, K_END:V_END]
    if rope:
        cos, sa, sb = cos_ref[...], sa_ref[...], sb_ref[...]

        def rot(z):
            return z * cos + pltpu.roll(z, LANES - 16, 1) * sa + pltpu.roll(z, 16, 1) * sb
    else:
        def rot(z):
            return z

    low = _low_half((q.shape[0], LANES))
    for i in range(ATTN_WIDTH // LANES):
        qs = rot(q[:, i * LANES:(i + 1) * LANES]) * (ATTN_SCALE * LOG2E)
        qm_ref[:, (2 * i) * LANES:(2 * i + 1) * LANES] = jnp.where(low, qs, 0.0).astype(BF16)
        qm_ref[:, (2 * i + 1) * LANES:(2 * i + 2) * LANES] = jnp.where(low, 0.0, qs).astype(BF16)
    k = rot(k)
    k_ref[...] = k
    v_ref[...] = v
    kd_ref[...], vd_ref[...] = _kv_slabs(k, v)
    f_ref[...] = u[:, V_END:F_END].astype(BF16)
    s_ref[...] = u[:, F_END:S_END]


def _ffn_in_call(x, mod, pw, layer, tables, rope, batch, seq, rows_per_group):
    n = x.shape[0]
    tm = min(TOKEN_TILE, seq)
    tiles_per_seq = seq // tm
    in_specs = [_row_spec(tm, D_MODEL), _mod_spec(rows_per_group // tm),
                _const_spec((1, D_MODEL), (layer, 0)),
                _const_spec((1, D_MODEL), (layer, 1)),
                _const_spec((D_MODEL, D_FF), (layer, 0)),
                _const_spec((D_MODEL, D_FF), (layer, 0)),
                _const_spec((D_FF, D_MODEL), (layer, 0)),
                _const_spec((D_MODEL, IN_WIDTH), (layer,)),
                _const_spec((1, ATTN_WIDTH), (layer,)),
                _const_spec((1, KV_WIDTH), (layer,)),
                _const_spec((ATTN_WIDTH, ATTN_WIDTH))]
    args = [x, mod, pw['g'], pw['g'], pw['w1'], pw['w3'], pw['w2'], pw['w_in'], pw['qg'], pw['kg'],
            tables['e']]
    if rope:
        in_specs += [pl.BlockSpec((tm, LANES), lambda i: (i % tiles_per_seq, 0))] * 3
        args += list(tables['rope'])
    widths = (D_MODEL, QM_WIDTH, KV_WIDTH, KV_WIDTH, KD_WIDTH, VD_WIDTH, FNET_WIDTH)
    dtypes = (F32, BF16, F32, F32, BF16, BF16, BF16)
    out_shape = [jax.ShapeDtypeStruct((n, w), d) for w, d in zip(widths, dtypes)]
    out_shape.append(jax.ShapeDtypeStruct((batch // SUBLANES, seq, WIDE), F32))
    out_specs = [_row_spec(tm, w) for w in widths] + [_wide_spec(tm, tiles_per_seq)]
    return pl.pallas_call(
        functools.partial(_ffn_in_kernel, rope=rope),
        out_shape=out_shape,
        grid=(n // tm,),
        in_specs=in_specs,
        out_specs=out_specs,
        compiler_params=_params("arbitrary"),
        name="ffn_in_proj",
    )(*args)


def _attend(qm_ref, o_ref, rows, sink_ref, kv, bias):
    nq = rows.stop - rows.start
    low = _low_half((nq, LANES))
    for h in range(N_KV_HEADS):
        j0 = GQA_GROUP * h
        qst = jnp.concatenate([qm_ref[rows, (j0 + g) * LANES:(j0 + g + 1) * LANES]
                               for g in range(GQA_GROUP)], axis=0)
        sink = jnp.concatenate([jnp.broadcast_to(sink_ref[j0 + g:j0 + g + 1, :], (nq, LANES))
                                for g in range(GQA_GROUP)], axis=0) * LOG2E
        kd, vd = kv(h)
        s = lax.dot_general(qst, kd, (((1,), (1,)), ((), ())), preferred_element_type=F32)
        if bias is not None:
            s = s + bias
        m = jnp.maximum(jnp.max(s, axis=-1, keepdims=True), sink)
        e = jnp.exp2(s - jnp.concatenate([m] * (s.shape[1] // LANES), axis=1)).astype(BF16)
        o = _dot(e, vd)
        es = jnp.exp2(sink - m)
        for t in range(GQA_GROUP // 2):
            r0, r1 = (2 * t) * nq, (2 * t + 1) * nq
            num = jnp.where(low, o[r0:r0 + nq, :LANES], o[r1:r1 + nq, LANES:])
            den = (jnp.where(low, o[r0:r0 + nq, LANES:], o[r1:r1 + nq, :LANES])
                   + jnp.where(low, es[r0:r0 + nq], es[r1:r1 + nq]))
            c0 = (j0 // 2 + t) * LANES
            o_ref[rows, c0:c0 + LANES] = (num * (1.0 / den)).astype(o_ref.dtype)


def _ctx_attn_kernel(qm_ref, kd_ref, vd_ref, sink_ref, o_ref, *, seq, nseq):
    for i in range(nseq):
        rows = slice(i * seq, (i + 1) * seq)

        def kv(h, rows=rows):
            return (kd_ref[rows, h * LANES:(h + 1) * LANES], vd_ref[rows, 2 * h * LANES:2 * (h + 1) * LANES])

        _attend(qm_ref, o_ref, rows, sink_ref, kv, None)


def _ctx_attn_call(qm, kd, vd, sink, layer, batch, seq):
    nseq = ATTN_BLOCKS_PER_STEP
    rows = nseq * seq
    return pl.pallas_call(
        functools.partial(_ctx_attn_kernel, seq=seq, nseq=nseq),
        out_shape=jax.ShapeDtypeStruct((batch * seq, ATTN_WIDTH), BF16),
        grid=(batch // nseq,),
        in_specs=[_row_spec(rows, QM_WIDTH), _row_spec(rows, KD_WIDTH), _row_spec(rows, VD_WIDTH),
                  pl.BlockSpec((None, N_HEADS, LANES), lambda b: (layer, 0, 0))],
        out_specs=_row_spec(rows, ATTN_WIDTH),
        compiler_params=_params("arbitrary"),
        name="ctx_attn",
    )(qm, kd, vd, sink)


def _lat_attn_kernel(qm_ref, kl_ref, vl_ref, kc_ref, vc_ref, bias_ref, sink_ref, o_ref,
                     kc_sc, vc_sc, *, seq, nblk):
    j = pl.program_id(1)
    nb = seq // BLOCK

    @pl.when(j == 0)
    def _():
        kc_sc[...], vc_sc[...] = _kv_slabs(kc_ref[...], vc_ref[...])

    for i in range(nblk):
        n = j * nblk + i
        start = pl.multiple_of(jnp.clip(n * BLOCK - WINDOW, 0, seq - LOCAL_SPAN), BLOCK)
        variant = jnp.where(n == 0, 0, jnp.where(n == nb - 1, 2, 1))

        def kv(h, start=start):
            kc, vc = slice(h * LANES, (h + 1) * LANES), slice(2 * h * LANES, 2 * (h + 1) * LANES)
            return (jnp.concatenate([kc_sc[:, kc], kl_ref[pl.ds(start, LOCAL_SPAN), kc]], axis=0),
                    jnp.concatenate([vc_sc[:, vc], vl_ref[pl.ds(start, LOCAL_SPAN), vc]], axis=0))

        _attend(qm_ref, o_ref, slice(i * BLOCK, (i + 1) * BLOCK), sink_ref, kv, bias_ref[variant])


def _lat_attn_call(qm, kd, vd, kctx, vctx, bias, sink, layer, batch, seq):
    past = kctx.shape[2]
    nblk = ATTN_BLOCKS_PER_STEP
    steps = seq // (BLOCK * nblk)
    rows = BLOCK * nblk
    return pl.pallas_call(
        functools.partial(_lat_attn_kernel, seq=seq, nblk=nblk),
        out_shape=jax.ShapeDtypeStruct((batch * seq, ATTN_WIDTH), BF16),
        grid=(batch, steps),
        in_specs=[pl.BlockSpec((rows, QM_WIDTH), lambda b, j: (b * steps + j, 0)),
                  pl.BlockSpec((seq, KD_WIDTH), lambda b, j: (b, 0)),
                  pl.BlockSpec((seq, VD_WIDTH), lambda b, j: (b, 0)),
                  pl.BlockSpec((None, None, past, KV_WIDTH), lambda b, j: (b, layer, 0, 0)),
                  pl.BlockSpec((None, None, past, KV_WIDTH), lambda b, j: (b, layer, 0, 0)),
                  pl.BlockSpec(bias.shape, lambda b, j: (0, 0, 0)),
                  pl.BlockSpec((None, N_HEADS, LANES), lambda b, j: (layer, 0, 0))],
        out_specs=pl.BlockSpec((rows, ATTN_WIDTH), lambda b, j: (b * steps + j, 0)),
        scratch_shapes=[pltpu.VMEM((past, KD_WIDTH), BF16), pltpu.VMEM((past, VD_WIDTH), BF16)],
        compiler_params=_params("arbitrary", "arbitrary"),
        name="lat_attn",
    )(qm, kd, vd, kctx, vctx, bias, sink)


def _fnet_kernel(x_ref, wc_ref, dm_ref, wf_ref, o_ref, *, seq):
    y = _dot(x_ref[...], wc_ref[...])
    yst = jnp.concatenate([y[:, :FNET_WIDTH], y[:, FNET_WIDTH:]], axis=0).astype(BF16)
    z = _dot(dm_ref[...], yst) * ((seq * FNET_HEAD_DIM) ** -0.5)
    o_ref[...] = _dot(z.astype(BF16), wf_ref[...]).astype(o_ref.dtype)


def _fnet_call(x, wc, dm, wf, layer, batch, seq):
    return pl.pallas_call(
        functools.partial(_fnet_kernel, seq=seq),
        out_shape=jax.ShapeDtypeStruct((batch * seq, FNET_WIDTH), BF16),
        grid=(batch,),
        in_specs=[_row_spec(seq, FNET_WIDTH),
                  _const_spec((FNET_WIDTH, 2 * FNET_WIDTH)),
                  _const_spec((seq, 2 * seq)),
                  _const_spec((FNET_WIDTH, FNET_WIDTH), (layer,))],
        out_specs=_row_spec(seq, FNET_WIDTH),
        compiler_params=_params("arbitrary"),
        name="fourier_mix",
    )(x, wc, dm, wf)


def _s5_kernel(xf_ref, xb_ref, h0_ref, bm_ref, cm_ref, lam_ref, yf_ref, yb_ref, fin_ref,
               xs, buf, ys, carry, *, steps, nchunks):
    c = pl.program_id(1)
    nslab = SSM_WIDTH // LANES

    @pl.when(c == 0)
    def _():
        carry[...] = h0_ref[0]

    for d, x_ref in enumerate((xf_ref, xb_ref)):
        for b in range(SUBLANES):
            for sl in range(nslab):
                c0 = b * SSM_WIDTH + sl * LANES
                xs[d * nslab + sl, pl.ds(b, steps, stride=SUBLANES), :] = x_ref[:, c0:c0 + LANES]
        x = jnp.concatenate([xs[d * nslab + sl] for sl in range(nslab)], axis=1)
        buf[d] = _dot(x.astype(BF16), bm_ref[d])

    w = SCAN_LANE_CHUNK
    for lc in range(SSM_LANES // w):
        re = slice(lc * w, (lc + 1) * w)
        im = slice(SSM_LANES + lc * w, SSM_LANES + (lc + 1) * w)
        lam = [(lam_ref[d, :, re], lam_ref[d, :, im]) for d in range(2)]

        def body(t, st, re=re, im=im, lam=lam):
            out = []
            for d in range(2):
                tt = t if d == 0 else steps - 1 - t
                r0 = pl.multiple_of(tt * SUBLANES, SUBLANES)
                sr, si = st[2 * d], st[2 * d + 1]
                lr, li = lam[d]
                nr = (lr * sr - li * si) + buf[d, pl.ds(r0, SUBLANES), re]
                ni = (lr * si + li * sr) + buf[d, pl.ds(r0, SUBLANES), im]
                buf[d, pl.ds(r0, SUBLANES), re] = nr
                buf[d, pl.ds(r0, SUBLANES), im] = ni
                out += [nr, ni]
            return tuple(out)

        init = (carry[0, :, re], carry[0, :, im], carry[1, :, re], carry[1, :, im])
        fr, fi, br, bi = lax.fori_loop(0, steps, body, init, unroll=2)
        carry[0, :, re] = fr
        carry[0, :, im] = fi
        carry[1, :, re] = br
        carry[1, :, im] = bi

    for d, y_ref in enumerate((yf_ref, yb_ref)):
        y = _dot(buf[d].astype(BF16), cm_ref[d])
        for sl in range(nslab):
            ys[d * nslab + sl] = y[:, sl * LANES:(sl + 1) * LANES]
        for b in range(SUBLANES):
            for sl in range(nslab):
                c0 = b * SSM_WIDTH + sl * LANES
                y_ref[:, c0:c0 + LANES] = ys[d * nslab + sl, pl.ds(b, steps, stride=SUBLANES), :]

    @pl.when(c == nchunks - 1)
    def _():
        fin_ref[0] = carry[...]


def _s5_call(x_wide, h0, pw, layer, groups, seq):
    steps = SCAN_STEPS
    rows = steps * SUBLANES
    nchunks = seq // steps
    width = 2 * SSM_LANES
    nslab = SSM_WIDTH // LANES
    return pl.pallas_call(
        functools.partial(_s5_kernel, steps=steps, nchunks=nchunks),
        out_shape=[jax.ShapeDtypeStruct((groups, seq, WIDE), F32),
                   jax.ShapeDtypeStruct((groups, seq, WIDE), F32),
                   jax.ShapeDtypeStruct((groups, 2, SUBLANES, width), F32)],
        grid=(groups, nchunks),
        in_specs=[pl.BlockSpec((None, steps, WIDE), lambda g, c: (g, c, 0)),
                  pl.BlockSpec((None, steps, WIDE), lambda g, c: (g, nchunks - 1 - c, 0)),
                  pl.BlockSpec((1, 2, SUBLANES, width), lambda g, c: (g, 0, 0, 0)),
                  pl.BlockSpec((None, 2, SSM_WIDTH, width), lambda g, c: (layer, 0, 0, 0)),
                  pl.BlockSpec((None, 2, width, SSM_WIDTH), lambda g, c: (layer, 0, 0, 0)),
                  pl.BlockSpec((None, 2, SUBLANES, width), lambda g, c: (layer, 0, 0, 0))],
        out_specs=[pl.BlockSpec((None, steps, WIDE), lambda g, c: (g, c, 0)),
                   pl.BlockSpec((None, steps, WIDE), lambda g, c: (g, nchunks - 1 - c, 0)),
                   pl.BlockSpec((1, 2, SUBLANES, width), lambda g, c: (g, 0, 0, 0))],
        scratch_shapes=[pltpu.VMEM((2 * nslab, rows, LANES), F32),
                        pltpu.VMEM((2, rows, width), F32),
                        pltpu.VMEM((2 * nslab, rows, LANES), F32),
                        pltpu.VMEM((2, SUBLANES, width), F32)],
        compiler_params=_params("arbitrary", "arbitrary"),
        name="s5_scan",
    )(x_wide, x_wide, h0, pw['bm'], pw['cm'], pw['lam'])


def _out_ffn_kernel(x_ref, mod_ref, a_ref, f_ref, us_ref, yf_ref, yb_ref, d_ref, wg_ref, wo_ref,
                    g_ref, w1_ref, w3_ref, w2_ref, o_ref):
    m = mod_ref[0]
    ys = jax.nn.gelu((d_ref[...] * us_ref[...] + yf_ref[...]) + yb_ref[...])
    gl = _dot(ys.astype(BF16), wg_ref[...])
    so = gl[:, :SSM_WIDTH] * jax.nn.sigmoid(gl[:, SSM_WIDTH:])
    mixed = jnp.concatenate([a_ref[...], f_ref[...], so.astype(BF16)], axis=1)
    x2 = x_ref[...] + m[5:6] * _dot(mixed, wo_ref[...])
    o_ref[...] = _half_ffn(x2, m, 6, g_ref[...], w1_ref, w3_ref, w2_ref)


def _out_ffn_call(x, mod, a, f, us, yf, yb, pw, layer, seq, rows_per_group):
    n = x.shape[0]
    tm = min(TOKEN_TILE, seq)
    wide = _wide_spec(tm, seq // tm)
    return pl.pallas_call(
        _out_ffn_kernel,
        out_shape=jax.ShapeDtypeStruct((n, D_MODEL), F32),
        grid=(n // tm,),
        in_specs=[_row_spec(tm, D_MODEL), _mod_spec(rows_per_group // tm),
                  _row_spec(tm, ATTN_WIDTH), _row_spec(tm, FNET_WIDTH), wide, wide, wide,
                  _const_spec((1, SSM_WIDTH), (layer,)),
                  _const_spec((SSM_WIDTH, 2 * SSM_WIDTH), (layer,)),
                  _const_spec((MIX_WIDTH, D_MODEL), (layer,)),
                  _const_spec((1, D_MODEL), (layer, 2)),
                  _const_spec((D_MODEL, D_FF), (layer, 1)),
                  _const_spec((D_MODEL, D_FF), (layer, 1)),
                  _const_spec((D_FF, D_MODEL), (layer, 1))],
        out_specs=_row_spec(tm, D_MODEL),
        compiler_params=_params("arbitrary"),
        name="out_proj_ffn",
    )(x, mod, a, f, us, yf, yb, pw['d'], pw['w_glu'], pw['w_out'], pw['g'], pw['w1'], pw['w3'], pw['w2'])


def _rope_tables(seq):
    pos = np.arange(seq)
    row, col = pos // GRID_W, pos % GRID_W
    quarter = HEAD_DIM // 4
    inv_freq = 1.0 / (ROPE_BASE ** (np.arange(quarter, dtype=np.float64) * 2.0 / (HEAD_DIM // 2)))
    lane = np.arange(LANES)
    in_head = lane % HEAD_DIM
    p = np.where((in_head < HEAD_DIM // 2)[None, :], row[:, None], col[:, None]).astype(np.float64)
    ang = p * inv_freq[lane % quarter][None, :]
    first = ((lane % (HEAD_DIM // 2)) < quarter)[None, :]
    cos = np.cos(ang)
    sa = np.where(first, -np.sin(ang), 0.0)
    sb = np.where(first, 0.0, np.sin(ang))
    return tuple(jnp.asarray(t, dtype=F32) for t in (cos, sa, sb))


def _dft_tables(seq):
    kl = np.outer(np.arange(seq), np.arange(seq)) % seq
    ang = 2.0 * np.pi * kl / seq
    dm = np.concatenate([np.cos(ang), -np.sin(ang)], axis=1)
    mc = np.outer(np.arange(FNET_HEAD_DIM), np.arange(FNET_HEAD_DIM)) % FNET_HEAD_DIM
    a64 = 2.0 * np.pi * mc / FNET_HEAD_DIM
    eye = np.eye(FNET_HEADS)
    wc = np.concatenate([np.kron(eye, np.cos(a64)), np.kron(eye, np.sin(a64))], axis=1)
    return jnp.asarray(dm, dtype=F32).astype(BF16), jnp.asarray(wc, dtype=F32).astype(BF16)


def _head_mean_matrix():
    return jnp.asarray(np.kron(np.eye(N_HEADS), np.full((HEAD_DIM, HEAD_DIM), 1.0 / HEAD_DIM)), dtype=BF16)


def _window_bias(past):
    i = np.arange(GQA_GROUP * BLOCK)[:, None] % BLOCK
    j = np.arange(LOCAL_SPAN)[None, :]
    out = np.zeros((3, GQA_GROUP * BLOCK, past + LOCAL_SPAN), np.float32)
    for v, off in enumerate((0, WINDOW, 2 * WINDOW)):
        out[v, :, past:] = np.where(np.abs(j - off - i) <= WINDOW, 0.0, NEG_INF)
    return jnp.asarray(out)


def _s5_matrices(lam_re, lam_im, b_re, b_im, c_re, c_im, log_step):
    step = jnp.exp(log_step)[..., None]
    mag = jnp.exp(lam_re * step)
    lr = mag * jnp.cos(lam_im * step)
    li = mag * jnp.sin(lam_im * step)
    den = lam_re * lam_re + lam_im * lam_im
    cr = ((lr - 1.0) * lam_re + li * lam_im) / den
    ci = (li * lam_re - (lr - 1.0) * lam_im) / den
    bbr = cr[..., None] * b_re - ci[..., None] * b_im
    bbi = cr[..., None] * b_im + ci[..., None] * b_re
    eye = jnp.eye(SSM_GROUPS, dtype=F32)

    def in_mat(b):
        return jnp.einsum('drgph,gk->drghkp', b, eye).reshape(DEPTH, 2, SSM_WIDTH, SSM_LANES)

    def out_mat(cw):
        return jnp.einsum('drghp,gk->drkpgh', cw, eye).reshape(DEPTH, 2, SSM_LANES, SSM_WIDTH)

    bm = jnp.concatenate([in_mat(bbr), in_mat(bbi)], axis=-1).astype(BF16)
    cm = jnp.concatenate([out_mat(c_re), -out_mat(c_im)], axis=-2).astype(BF16)
    lrow = jnp.concatenate([lr.reshape(DEPTH, 2, SSM_LANES), li.reshape(DEPTH, 2, SSM_LANES)], axis=-1)
    lam = jnp.broadcast_to(lrow[:, :, None, :], (DEPTH, 2, SUBLANES, 2 * SSM_LANES))
    return bm, cm, lam


def _stream_layer(x, mod, pw, layer, batch, seq, rows_per_group, tables, ctx_kv, h0):
    latent = ctx_kv is not None
    groups = batch // SUBLANES
    x1, qm, k, v, kd, vd, f, us = _ffn_in_call(x, mod, pw, layer, tables, latent, batch, seq, rows_per_group)
    if latent:
        a = _lat_attn_call(qm, kd, vd, ctx_kv[0], ctx_kv[1], tables['bias'], pw['sink'], layer, batch, seq)
    else:
        a = _ctx_attn_call(qm, kd, vd, pw['sink'], layer, batch, seq)
    dm, wc = tables['dft'][seq]
    fz = _fnet_call(f, wc, dm, pw['w_fnet'], layer, batch, seq)
    yf, yb, fin = _s5_call(us, h0, pw, layer, groups, seq)
    x = _out_ffn_call(x1, mod, a, fz, us, yf, yb, pw, layer, seq, rows_per_group)
    return x, (k, v, fin)


def kernel(x_prompt, x_sample, cache_k, cache_v, state_ssm_re, state_ssm_im, c, c_ctx, w_mod, b_mod, norm_g, ffn_w1, ffn_w3, ffn_w2, w_in, w_out, q_norm_g, k_norm_g, attn_sink, w_fnet, ssm_lambda_re, ssm_lambda_im, ssm_b_re, ssm_b_im, ssm_c_re, ssm_c_im, ssm_d, ssm_log_step, ssm_w_glu):
    batch, seq, _ = x_prompt.shape
    dec_batch, dec_seq, _ = x_sample.shape
    past = cache_k.shape[2]

    tables = {'e': _head_mean_matrix(), 'rope': _rope_tables(dec_seq), 'bias': _window_bias(past),
              'dft': {s: _dft_tables(s) for s in {seq, dec_seq}}}

    bm, cm, lam = _s5_matrices(ssm_lambda_re, ssm_lambda_im, ssm_b_re, ssm_b_im, ssm_c_re, ssm_c_im,
                               ssm_log_step)
    pw = {'g': norm_g.reshape(DEPTH, 3, 1, D_MODEL),
          'w1': ffn_w1.astype(BF16), 'w3': ffn_w3.astype(BF16), 'w2': ffn_w2.astype(BF16),
          'w_in': w_in.astype(BF16), 'w_out': w_out.astype(BF16),
          'qg': jnp.tile(q_norm_g, (1, N_HEADS)).reshape(DEPTH, 1, ATTN_WIDTH),
          'kg': jnp.tile(k_norm_g, (1, N_KV_HEADS)).reshape(DEPTH, 1, KV_WIDTH),
          'sink': jnp.broadcast_to(attn_sink[:, :, None], (DEPTH, N_HEADS, LANES)),
          'w_fnet': w_fnet.astype(BF16), 'bm': bm, 'cm': cm, 'lam': lam,
          'd': ssm_d.reshape(DEPTH, 1, SSM_WIDTH), 'w_glu': ssm_w_glu.astype(BF16)}

    cond = jnp.zeros((COND_ROWS, D_MODEL), F32).at[0].set(c_ctx).at[1:1 + dec_batch].set(c)
    mods = _mod_call(cond, w_mod, b_mod).reshape(DEPTH, COND_ROWS, N_MOD, D_MODEL)

    ctx_kv = (cache_k.reshape(dec_batch, DEPTH, past, KV_WIDTH), cache_v.reshape(dec_batch, DEPTH, past, KV_WIDTH))
    h0_lat = jnp.concatenate([state_ssm_re.reshape(dec_batch, DEPTH, 2, SSM_LANES),
                              state_ssm_im.reshape(dec_batch, DEPTH, 2, SSM_LANES)], axis=-1)
    h0_lat = h0_lat.reshape(dec_batch // SUBLANES, SUBLANES, DEPTH, 2, 2 * SSM_LANES).transpose(2, 0, 3, 1, 4)
    h0_ctx = jnp.zeros((batch // SUBLANES, 2, SUBLANES, 2 * SSM_LANES), F32)

    yp = x_prompt.reshape(batch * seq, D_MODEL)
    ys = x_sample.reshape(dec_batch * dec_seq, D_MODEL)
    ks, vs, sre, sim = [], [], [], []
    for l in range(DEPTH):
        yp, (k_l, v_l, fin) = _stream_layer(yp, mods[l, 0:1], pw, l, batch, seq, batch * seq, tables,
                                            None, h0_ctx)
        ks.append(k_l.reshape(batch, seq, N_KV_HEADS, HEAD_DIM))
        vs.append(v_l.reshape(batch, seq, N_KV_HEADS, HEAD_DIM))
        fin = fin.reshape(batch // SUBLANES, 2, SUBLANES, 2, SSM_GROUPS, SSM_STATE)
        fin = fin.transpose(3, 0, 2, 1, 4, 5).reshape(2, batch, 2, SSM_GROUPS, SSM_STATE)
        sre.append(fin[0])
        sim.append(fin[1])
        ys, _ = _stream_layer(ys, mods[l, 1:1 + dec_batch], pw, l, dec_batch, dec_seq, dec_seq, tables,
                              ctx_kv, h0_lat[l])

    return (yp.reshape(batch, seq, D_MODEL), ys.reshape(dec_batch, dec_seq, D_MODEL),
            jnp.stack(ks, axis=1), jnp.stack(vs, axis=1), jnp.stack(sre, axis=1), jnp.stack(sim, axis=1))
```

```python
import functools
import math

import numpy as np
import jax
import jax.numpy as jnp
from jax import lax
from jax.experimental import pallas as pl
from jax.experimental.pallas import tpu as pltpu

F32 = jnp.float32
BF16 = jnp.bfloat16

D_MODEL = 1024
DEPTH = 2
GRID_W = 64
HEAD_DIM = 64
N_HEADS = 8
N_KV_HEADS = 2
GQA_GROUP = N_HEADS // N_KV_HEADS
ATTN_WIDTH = N_HEADS * HEAD_DIM
KV_WIDTH = N_KV_HEADS * HEAD_DIM
WINDOW = 128
BLOCK = 128
ATTN_SCALE = HEAD_DIM ** -0.5
ROPE_BASE = 10000.0
NEG_INF = -1e30
LOG2E = math.log2(math.e)
FNET_HEADS = 4
FNET_HEAD_DIM = 64
FNET_WIDTH = FNET_HEADS * FNET_HEAD_DIM
SSM_WIDTH = 256
SSM_GROUP = 16
SSM_GROUPS = SSM_WIDTH // SSM_GROUP
SSM_STATE = 64
SSM_LANES = SSM_GROUPS * SSM_STATE
MIX_WIDTH = ATTN_WIDTH + FNET_WIDTH + SSM_WIDTH
Q_END = ATTN_WIDTH
K_END = Q_END + KV_WIDTH
V_END = K_END + KV_WIDTH
F_END = V_END + FNET_WIDTH
S_END = F_END + SSM_WIDTH
IN_WIDTH = S_END
D_FF = 2816
N_MOD = 9
NORM_EPS = 1e-6

LANES = 128
SUBLANES = 8
VMEM_LIMIT_BYTES = 60 * 1024 * 1024
TOKEN_TILE = 1024
FF_CHUNKS = 11
FF_CHUNK = D_FF // FF_CHUNKS
ATTN_BLOCKS_PER_STEP = 2
SCAN_STEPS = 64
SCAN_LANE_CHUNK = 256
COND_ROWS = 16
LOCAL_SPAN = BLOCK + 2 * WINDOW
QM_WIDTH = N_HEADS * LANES
KD_WIDTH = N_KV_HEADS * LANES
VD_WIDTH = N_KV_HEADS * 2 * LANES
WIDE = SUBLANES * SSM_WIDTH


def _params(*sem):
    return pltpu.CompilerParams(dimension_semantics=sem, vmem_limit_bytes=VMEM_LIMIT_BYTES)


def _dot(a, b):
    return jnp.dot(a, b, preferred_element_type=F32)


def _norm_mod(x, g, shift, scale):
    y = x * lax.rsqrt(jnp.mean(x * x, axis=-1, keepdims=True) + NORM_EPS)
    return (y * g) * (1.0 + scale) + shift


def _const_spec(shape, lead=()):
    nd = len(shape)
    idx = tuple(lead) + (0,) * nd
    return pl.BlockSpec((None,) * len(lead) + tuple(shape), lambda *_: idx,
                        pipeline_mode=pl.Buffered(1))


def _row_spec(tm, width):
    return pl.BlockSpec((tm, width), lambda i: (i, 0))


def _mod_spec(tiles_per_group):
    return pl.BlockSpec((1, N_MOD, D_MODEL), lambda i: (i // tiles_per_group, 0, 0))


def _wide_spec(tm, seq):
    if tm <= seq:
        tiles_per_seq = seq // tm

        def index(i):
            b = i // tiles_per_seq
            return (b // SUBLANES, i % tiles_per_seq, b % SUBLANES)
        return pl.BlockSpec((None, tm, SSM_WIDTH), index)
    nseq = tm // seq

    def index(i):
        b = i * nseq
        return (b // SUBLANES, 0, (b % SUBLANES) // nseq)
    return pl.BlockSpec((None, seq, nseq * SSM_WIDTH), index)


def _wide_store(ref, x, seq):
    for j in range(x.shape[0] // seq if x.shape[0] > seq else 1):
        rows = slice(j * seq, (j + 1) * seq) if x.shape[0] > seq else slice(None)
        ref[:, j * SSM_WIDTH:(j + 1) * SSM_WIDTH] = x[rows]


def _wide_load(ref, seq):
    ncol = ref.shape[1] // SSM_WIDTH
    if ncol == 1:
        return ref[...]
    return jnp.concatenate([ref[:, j * SSM_WIDTH:(j + 1) * SSM_WIDTH] for j in range(ncol)], axis=0)


def _mod_kernel(c_ref, w_ref, b_ref, o_ref):
    c = c_ref[...]
    a = (c * jax.nn.sigmoid(c)).astype(BF16)
    o_ref[0] = _dot(a, w_ref[0].astype(BF16)) + b_ref[0]


def _mod_call(cond, w_mod, b_mod):
    tn = 1024
    return pl.pallas_call(
        _mod_kernel,
        out_shape=jax.ShapeDtypeStruct((DEPTH, COND_ROWS, N_MOD * D_MODEL), F32),
        grid=(DEPTH, N_MOD * D_MODEL // tn),
        in_specs=[pl.BlockSpec((COND_ROWS, D_MODEL), lambda l, j: (0, 0)),
                  pl.BlockSpec((1, D_MODEL, tn), lambda l, j: (l, 0, j)),
                  pl.BlockSpec((1, 1, tn), lambda l, j: (l, 0, j))],
        out_specs=pl.BlockSpec((1, COND_ROWS, tn), lambda l, j: (l, 0, j)),
        compiler_params=_params("arbitrary", "arbitrary"),
        name="cond_mod",
    )(cond, w_mod, b_mod.reshape(DEPTH, 1, N_MOD * D_MODEL))


def _half_ffn(x, m, k0, g, w1_ref, w3_ref, w2_ref):
    h = _norm_mod(x, g, m[k0:k0 + 1], m[k0 + 1:k0 + 2]).astype(BF16)
    y = None
    for c in range(FF_CHUNKS):
        cols = slice(c * FF_CHUNK, (c + 1) * FF_CHUNK)
        a = _dot(h, w1_ref[:, cols])
        b = _dot(h, w3_ref[:, cols])
        t = ((a * jax.nn.sigmoid(a)) * b).astype(BF16)
        yc = _dot(t, w2_ref[cols, :])
        y = yc if y is None else y + yc
    return x + (0.5 * m[k0 + 2:k0 + 3]) * y


def _head_norm(z, e, g):
    ms = _dot((z * z).astype(BF16), e)
    return (z * lax.rsqrt(ms + NORM_EPS)) * g


def _low_half(shape):
    return lax.broadcasted_iota(jnp.int32, shape, 1) < HEAD_DIM


def _kv_slabs(k, v):
    low = _low_half(k.shape)
    kr = pltpu.roll(k, HEAD_DIM, 1)
    vr = pltpu.roll(v, HEAD_DIM, 1)
    kd = jnp.concatenate([jnp.where(low, k, kr), jnp.where(low, kr, k)], axis=1)
    vd = jnp.concatenate([jnp.where(low, v, 1.0), jnp.where(low, 1.0, vr),
                          jnp.where(low, vr, 1.0), jnp.where(low, 1.0, v)], axis=1)
    return kd.astype(BF16), vd.astype(BF16)


def _ffn_in_kernel(*refs, rope, seq):
    if rope:
        (x_ref, mod_ref, g0_ref, g1_ref, w1_ref, w3_ref, w2_ref, w_ref, qg_ref, kg_ref, e_ref,
         cos_ref, sa_ref, sb_ref, x1_ref, qm_ref, k_ref, v_ref, kd_ref, vd_ref, f_ref, s_ref) = refs
    else:
        (x_ref, mod_ref, g0_ref, g1_ref, w1_ref, w3_ref, w2_ref, w_ref, qg_ref, kg_ref, e_ref,
         x1_ref, qm_ref, k_ref, v_ref, kd_ref, vd_ref, f_ref, s_ref) = refs
    m = mod_ref[0]
    x1 = _half_ffn(x_ref[...], m, 0, g0_ref[...], w1_ref, w3_ref, w2_ref)
    x1_ref[...] = x1
    h = _norm_mod(x1, g1_ref[...], m[3:4], m[4:5]).astype(BF16)
    u = _dot(h, w_ref[...])
    q = _head_norm(u[:, :Q_END], e_ref[...], qg_ref[...])
    k = _head_norm(u[:, Q_END:K_END], e_ref[0:KV_WIDTH, 0:KV_WIDTH], kg_ref[...])
    v = u[:, K_END:V_END]
    if rope:
        cos, sa, sb = cos_ref[...], sa_ref[...], sb_ref[...]

        def rot(z):
            return z * cos + pltpu.roll(z, LANES - 16, 1) * sa + pltpu.roll(z, 16, 1) * sb
    else:
        def rot(z):
            return z

    low = _low_half((q.shape[0], LANES))
    for i in range(ATTN_WIDTH // LANES):
        qs = rot(q[:, i * LANES:(i + 1) * LANES]) * (ATTN_SCALE * LOG2E)
        qm_ref[:, (2 * i) * LANES:(2 * i + 1) * LANES] = jnp.where(low, qs, 0.0).astype(BF16)
        qm_ref[:, (2 * i + 1) * LANES:(2 * i + 2) * LANES] = jnp.where(low, 0.0, qs).astype(BF16)
    k = rot(k)
    k_ref[...] = k
    v_ref[...] = v
    kd_ref[...], vd_ref[...] = _kv_slabs(k, v)
    f_ref[...] = u[:, V_END:F_END].astype(BF16)
    _wide_store(s_ref, u[:, F_END:S_END], seq)


def _token_tile(seq):
    return TOKEN_TILE if TOKEN_TILE <= seq else min(TOKEN_TILE, SUBLANES * seq // 2)


def _ffn_in_call(x, mod, pw, layer, tables, rope, batch, seq, rows_per_group):
    n = x.shape[0]
    tm = _token_tile(seq)
    tiles_per_seq = max(seq // tm, 1)
    in_specs = [_row_spec(tm, D_MODEL), _mod_spec(rows_per_group // tm),
                _const_spec((1, D_MODEL), (layer, 0)),
                _const_spec((1, D_MODEL), (layer, 1)),
                _const_spec((D_MODEL, D_FF), (layer, 0)),
                _const_spec((D_MODEL, D_FF), (layer, 0)),
                _const_spec((D_FF, D_MODEL), (layer, 0)),
                _const_spec((D_MODEL, IN_WIDTH), (layer,)),
                _const_spec((1, ATTN_WIDTH), (layer,)),
                _const_spec((1, KV_WIDTH), (layer,)),
                _const_spec((ATTN_WIDTH, ATTN_WIDTH))]
    args = [x, mod, pw['g'], pw['g'], pw['w1'], pw['w3'], pw['w2'], pw['w_in'], pw['qg'], pw['kg'],
            tables['e']]
    if rope:
        in_specs += [pl.BlockSpec((tm, LANES), lambda i: (i % tiles_per_seq, 0))] * 3
        args += list(tables['rope'])
    widths = (D_MODEL, QM_WIDTH, KV_WIDTH, KV_WIDTH, KD_WIDTH, VD_WIDTH, FNET_WIDTH)
    dtypes = (F32, BF16, F32, F32, BF16, BF16, BF16)
    out_shape = [jax.ShapeDtypeStruct((n, w), d) for w, d in zip(widths, dtypes)]
    out_shape.append(jax.ShapeDtypeStruct((batch // SUBLANES, seq, WIDE), F32))
    out_specs = [_row_spec(tm, w) for w in widths] + [_wide_spec(tm, seq)]
    return pl.pallas_call(
        functools.partial(_ffn_in_kernel, rope=rope, seq=seq),
        out_shape=out_shape,
        grid=(n // tm,),
        in_specs=in_specs,
        out_specs=out_specs,
        compiler_params=_params("arbitrary"),
        name="ffn_in_proj",
    )(*args)


def _attend(qm_ref, o_ref, rows, sink_ref, kv, bias):
    nq = rows.stop - rows.start
    low = _low_half((nq, LANES))
    for h in range(N_KV_HEADS):
        j0 = GQA_GROUP * h
        qst = jnp.concatenate([qm_ref[rows, (j0 + g) * LANES:(j0 + g + 1) * LANES]
                               for g in range(GQA_GROUP)], axis=0)
        sink = jnp.concatenate([jnp.broadcast_to(sink_ref[j0 + g:j0 + g + 1, :], (nq, LANES))
                                for g in range(GQA_GROUP)], axis=0) * LOG2E
        kd, vd = kv(h)
        s = lax.dot_general(qst, kd, (((1,), (1,)), ((), ())), preferred_element_type=F32)
        if bias is not None:
            s = s + bias
        m = jnp.maximum(jnp.max(s, axis=-1, keepdims=True), sink)
        e = jnp.exp2(s - jnp.concatenate([m] * (s.shape[1] // LANES), axis=1)).astype(BF16)
        o = _dot(e, vd)
        es = jnp.exp2(sink - m)
        for t in range(GQA_GROUP // 2):
            r0, r1 = (2 * t) * nq, (2 * t + 1) * nq
            num = jnp.where(low, o[r0:r0 + nq, :LANES], o[r1:r1 + nq, LANES:])
            den = (jnp.where(low, o[r0:r0 + nq, LANES:], o[r1:r1 + nq, :LANES])
                   + jnp.where(low, es[r0:r0 + nq], es[r1:r1 + nq]))
            c0 = (j0 // 2 + t) * LANES
            o_ref[rows, c0:c0 + LANES] = (num * (1.0 / den)).astype(o_ref.dtype)


def _ctx_attn_kernel(qm_ref, kd_ref, vd_ref, sink_ref, o_ref, *, seq, nseq):
    for i in range(nseq):
        rows = slice(i * seq, (i + 1) * seq)

        def kv(h, rows=rows):
            return (kd_ref[rows, h * LANES:(h + 1) * LANES], vd_ref[rows, 2 * h * LANES:2 * (h + 1) * LANES])

        _attend(qm_ref, o_ref, rows, sink_ref, kv, None)


def _ctx_attn_call(qm, kd, vd, sink, layer, batch, seq):
    nseq = ATTN_BLOCKS_PER_STEP
    rows = nseq * seq
    return pl.pallas_call(
        functools.partial(_ctx_attn_kernel, seq=seq, nseq=nseq),
        out_shape=jax.ShapeDtypeStruct((batch * seq, ATTN_WIDTH), BF16),
        grid=(batch // nseq,),
        in_specs=[_row_spec(rows, QM_WIDTH), _row_spec(rows, KD_WIDTH), _row_spec(rows, VD_WIDTH),
                  pl.BlockSpec((None, N_HEADS, LANES), lambda b: (layer, 0, 0))],
        out_specs=_row_spec(rows, ATTN_WIDTH),
        compiler_params=_params("arbitrary"),
        name="ctx_attn",
    )(qm, kd, vd, sink)


def _lat_attn_kernel(qm_ref, kl_ref, vl_ref, kc_ref, vc_ref, bias_ref, sink_ref, o_ref,
                     kc_sc, vc_sc, *, seq, nblk):
    j = pl.program_id(1)
    nb = seq // BLOCK

    @pl.when(j == 0)
    def _():
        kc_sc[...], vc_sc[...] = _kv_slabs(kc_ref[...], vc_ref[...])

    for i in range(nblk):
        n = j * nblk + i
        start = pl.multiple_of(jnp.clip(n * BLOCK - WINDOW, 0, seq - LOCAL_SPAN), BLOCK)
        variant = jnp.where(n == 0, 0, jnp.where(n == nb - 1, 2, 1))

        def kv(h, start=start):
            kc, vc = slice(h * LANES, (h + 1) * LANES), slice(2 * h * LANES, 2 * (h + 1) * LANES)
            return (jnp.concatenate([kc_sc[:, kc], kl_ref[pl.ds(start, LOCAL_SPAN), kc]], axis=0),
                    jnp.concatenate([vc_sc[:, vc], vl_ref[pl.ds(start, LOCAL_SPAN), vc]], axis=0))

        _attend(qm_ref, o_ref, slice(i * BLOCK, (i + 1) * BLOCK), sink_ref, kv, bias_ref[variant])


def _lat_attn_call(qm, kd, vd, kctx, vctx, bias, sink, layer, batch, seq):
    past = kctx.shape[2]
    nblk = ATTN_BLOCKS_PER_STEP
    steps = seq // (BLOCK * nblk)
    rows = BLOCK * nblk
    return pl.pallas_call(
        functools.partial(_lat_attn_kernel, seq=seq, nblk=nblk),
        out_shape=jax.ShapeDtypeStruct((batch * seq, ATTN_WIDTH), BF16),
        grid=(batch, steps),
        in_specs=[pl.BlockSpec((rows, QM_WIDTH), lambda b, j: (b * steps + j, 0)),
                  pl.BlockSpec((seq, KD_WIDTH), lambda b, j: (b, 0)),
                  pl.BlockSpec((seq, VD_WIDTH), lambda b, j: (b, 0)),
                  pl.BlockSpec((None, None, past, KV_WIDTH), lambda b, j: (b, layer, 0, 0)),
                  pl.BlockSpec((None, None, past, KV_WIDTH), lambda b, j: (b, layer, 0, 0)),
                  pl.BlockSpec(bias.shape, lambda b, j: (0, 0, 0)),
                  pl.BlockSpec((None, N_HEADS, LANES), lambda b, j: (layer, 0, 0))],
        out_specs=pl.BlockSpec((rows, ATTN_WIDTH), lambda b, j: (b * steps + j, 0)),
        scratch_shapes=[pltpu.VMEM((past, KD_WIDTH), BF16), pltpu.VMEM((past, VD_WIDTH), BF16)],
        compiler_params=_params("arbitrary", "arbitrary"),
        name="lat_attn",
    )(qm, kd, vd, kctx, vctx, bias, sink)


def _fnet_kernel(x_ref, wc_ref, dm_ref, wf_ref, o_ref, *, seq):
    y = _dot(x_ref[...], wc_ref[...])
    yst = jnp.concatenate([y[:, :FNET_WIDTH], y[:, FNET_WIDTH:]], axis=0).astype(BF16)
    z = _dot(dm_ref[...], yst) * ((seq * FNET_HEAD_DIM) ** -0.5)
    o_ref[...] = _dot(z.astype(BF16), wf_ref[...]).astype(o_ref.dtype)


def _fnet_call(x, wc, dm, wf, layer, batch, seq):
    return pl.pallas_call(
        functools.partial(_fnet_kernel, seq=seq),
        out_shape=jax.ShapeDtypeStruct((batch * seq, FNET_WIDTH), BF16),
        grid=(batch,),
        in_specs=[_row_spec(seq, FNET_WIDTH),
                  _const_spec((FNET_WIDTH, 2 * FNET_WIDTH)),
                  _const_spec((seq, 2 * seq)),
                  _const_spec((FNET_WIDTH, FNET_WIDTH), (layer,))],
        out_specs=_row_spec(seq, FNET_WIDTH),
        compiler_params=_params("arbitrary"),
        name="fourier_mix",
    )(x, wc, dm, wf)


def _s5_kernel(xf_ref, xb_ref, h0_ref, bm_ref, cm_ref, lam_ref, yf_ref, yb_ref, fin_ref,
               xs, buf, ys, carry, *, steps, nchunks):
    c = pl.program_id(1)
    nslab = SSM_WIDTH // LANES

    @pl.when(c == 0)
    def _():
        carry[...] = h0_ref[0]

    xin = []
    for d, x_ref in enumerate((xf_ref, xb_ref)):
        for b in range(SUBLANES):
            for sl in range(nslab):
                c0 = b * SSM_WIDTH + sl * LANES
                xs[d * nslab + sl, pl.ds(b, steps, stride=SUBLANES), :] = x_ref[:, c0:c0 + LANES]
        xin.append(jnp.concatenate([xs[d * nslab + sl] for sl in range(nslab)], axis=1).astype(BF16))

    w = SCAN_LANE_CHUNK
    yacc = [None, None]
    for lc in range(SSM_LANES // w):
        re = slice(lc * w, (lc + 1) * w)
        im = slice(SSM_LANES + lc * w, SSM_LANES + (lc + 1) * w)
        for d in range(2):
            buf[d, :, re] = _dot(xin[d], bm_ref[d, :, re])
            buf[d, :, im] = _dot(xin[d], bm_ref[d, :, im])
        lam = [(lam_ref[d, :, re], lam_ref[d, :, im]) for d in range(2)]
        st = [(carry[d, :, re], carry[d, :, im]) for d in range(2)]
        for t in range(steps):
            for d in range(2):
                r0 = (t if d == 0 else steps - 1 - t) * SUBLANES
                rows = slice(r0, r0 + SUBLANES)
                (lr, li), (sr, si) = lam[d], st[d]
                nr = (lr * sr - li * si) + buf[d, rows, re]
                ni = (lr * si + li * sr) + buf[d, rows, im]
                buf[d, rows, re] = nr
                buf[d, rows, im] = ni
                st[d] = (nr, ni)
        for d in range(2):
            carry[d, :, re], carry[d, :, im] = st[d]
        for d in range(2):
            part = (_dot(buf[d, :, re].astype(BF16), cm_ref[d, re, :])
                    + _dot(buf[d, :, im].astype(BF16), cm_ref[d, im, :]))
            yacc[d] = part if yacc[d] is None else yacc[d] + part

    for d, y_ref in enumerate((yf_ref, yb_ref)):
        y = yacc[d]
        for sl in range(nslab):
            ys[d * nslab + sl] = y[:, sl * LANES:(sl + 1) * LANES]
        for b in range(SUBLANES):
            for sl in range(nslab):
                c0 = b * SSM_WIDTH + sl * LANES
                y_ref[:, c0:c0 + LANES] = ys[d * nslab + sl, pl.ds(b, steps, stride=SUBLANES), :]

    @pl.when(c == nchunks - 1)
    def _():
        fin_ref[0] = carry[...]


def _s5_call(x_wide, h0, pw, layer, groups, seq):
    steps = SCAN_STEPS
    rows = steps * SUBLANES
    nchunks = seq // steps
    width = 2 * SSM_LANES
    nslab = SSM_WIDTH // LANES
    return pl.pallas_call(
        functools.partial(_s5_kernel, steps=steps, nchunks=nchunks),
        out_shape=[jax.ShapeDtypeStruct((groups, seq, WIDE), F32),
                   jax.ShapeDtypeStruct((groups, seq, WIDE), F32),
                   jax.ShapeDtypeStruct((groups, 2, SUBLANES, width), F32)],
        grid=(groups, nchunks),
        in_specs=[pl.BlockSpec((None, steps, WIDE), lambda g, c: (g, c, 0)),
                  pl.BlockSpec((None, steps, WIDE), lambda g, c: (g, nchunks - 1 - c, 0)),
                  pl.BlockSpec((1, 2, SUBLANES, width), lambda g, c: (g, 0, 0, 0)),
                  pl.BlockSpec((None, 2, SSM_WIDTH, width), lambda g, c: (layer, 0, 0, 0)),
                  pl.BlockSpec((None, 2, width, SSM_WIDTH), lambda g, c: (layer, 0, 0, 0)),
                  pl.BlockSpec((None, 2, SUBLANES, width), lambda g, c: (layer, 0, 0, 0))],
        out_specs=[pl.BlockSpec((None, steps, WIDE), lambda g, c: (g, c, 0)),
                   pl.BlockSpec((None, steps, WIDE), lambda g, c: (g, nchunks - 1 - c, 0)),
                   pl.BlockSpec((1, 2, SUBLANES, width), lambda g, c: (g, 0, 0, 0))],
        scratch_shapes=[pltpu.VMEM((2 * nslab, rows, LANES), F32),
                        pltpu.VMEM((2, rows, width), F32),
                        pltpu.VMEM((2 * nslab, rows, LANES), F32),
                        pltpu.VMEM((2, SUBLANES, width), F32)],
        compiler_params=_params("arbitrary", "arbitrary"),
        name="s5_scan",
    )(x_wide, x_wide, h0, pw['bm'], pw['cm'], pw['lam'])


def _out_ffn_kernel(x_ref, mod_ref, a_ref, f_ref, us_ref, yf_ref, yb_ref, d_ref, wg_ref, wo_ref,
                    g_ref, w1_ref, w3_ref, w2_ref, o_ref, *, seq):
    m = mod_ref[0]
    us, yf, yb = (_wide_load(r, seq) for r in (us_ref, yf_ref, yb_ref))
    ys = jax.nn.gelu((d_ref[...] * us + yf) + yb)
    gl = _dot(ys.astype(BF16), wg_ref[...])
    so = gl[:, :SSM_WIDTH] * jax.nn.sigmoid(gl[:, SSM_WIDTH:])
    mixed = jnp.concatenate([a_ref[...], f_ref[...], so.astype(BF16)], axis=1)
    x2 = x_ref[...] + m[5:6] * _dot(mixed, wo_ref[...])
    o_ref[...] = _half_ffn(x2, m, 6, g_ref[...], w1_ref, w3_ref, w2_ref)


def _out_ffn_call(x, mod, a, f, us, yf, yb, pw, layer, seq, rows_per_group):
    n = x.shape[0]
    tm = _token_tile(seq)
    wide = _wide_spec(tm, seq)
    return pl.pallas_call(
        functools.partial(_out_ffn_kernel, seq=seq),
        out_shape=jax.ShapeDtypeStruct((n, D_MODEL), F32),
        grid=(n // tm,),
        in_specs=[_row_spec(tm, D_MODEL), _mod_spec(rows_per_group // tm),
                  _row_spec(tm, ATTN_WIDTH), _row_spec(tm, FNET_WIDTH), wide, wide, wide,
                  _const_spec((1, SSM_WIDTH), (layer,)),
                  _const_spec((SSM_WIDTH, 2 * SSM_WIDTH), (layer,)),
                  _const_spec((MIX_WIDTH, D_MODEL), (layer,)),
                  _const_spec((1, D_MODEL), (layer, 2)),
                  _const_spec((D_MODEL, D_FF), (layer, 1)),
                  _const_spec((D_MODEL, D_FF), (layer, 1)),
                  _const_spec((D_FF, D_MODEL), (layer, 1))],
        out_specs=_row_spec(tm, D_MODEL),
        compiler_params=_params("arbitrary"),
        name="out_proj_ffn",
    )(x, mod, a, f, us, yf, yb, pw['d'], pw['w_glu'], pw['w_out'], pw['g'], pw['w1'], pw['w3'], pw['w2'])


def _rope_tables(seq):
    pos = np.arange(seq)
    row, col = pos // GRID_W, pos % GRID_W
    quarter = HEAD_DIM // 4
    inv_freq = 1.0 / (ROPE_BASE ** (np.arange(quarter, dtype=np.float64) * 2.0 / (HEAD_DIM // 2)))
    lane = np.arange(LANES)
    in_head = lane % HEAD_DIM
    p = np.where((in_head < HEAD_DIM // 2)[None, :], row[:, None], col[:, None]).astype(np.float64)
    ang = p * inv_freq[lane % quarter][None, :]
    first = ((lane % (HEAD_DIM // 2)) < quarter)[None, :]
    cos = np.cos(ang)
    sa = np.where(first, -np.sin(ang), 0.0)
    sb = np.where(first, 0.0, np.sin(ang))
    return tuple(jnp.asarray(t, dtype=F32) for t in (cos, sa, sb))


def _dft_tables(seq):
    kl = np.outer(np.arange(seq), np.arange(seq)) % seq
    ang = 2.0 * np.pi * kl / seq
    dm = np.concatenate([np.cos(ang), -np.sin(ang)], axis=1)
    mc = np.outer(np.arange(FNET_HEAD_DIM), np.arange(FNET_HEAD_DIM)) % FNET_HEAD_DIM
    a64 = 2.0 * np.pi * mc / FNET_HEAD_DIM
    eye = np.eye(FNET_HEADS)
    wc = np.concatenate([np.kron(eye, np.cos(a64)), np.kron(eye, np.sin(a64))], axis=1)
    return jnp.asarray(dm, dtype=F32).astype(BF16), jnp.asarray(wc, dtype=F32).astype(BF16)


def _head_mean_matrix():
    return jnp.asarray(np.kron(np.eye(N_HEADS), np.full((HEAD_DIM, HEAD_DIM), 1.0 / HEAD_DIM)), dtype=BF16)


def _window_bias(past):
    i = np.arange(GQA_GROUP * BLOCK)[:, None] % BLOCK
    j = np.arange(LOCAL_SPAN)[None, :]
    out = np.zeros((3, GQA_GROUP * BLOCK, past + LOCAL_SPAN), np.float32)
    for v, off in enumerate((0, WINDOW, 2 * WINDOW)):
        out[v, :, past:] = np.where(np.abs(j - off - i) <= WINDOW, 0.0, NEG_INF)
    return jnp.asarray(out)


def _s5_matrices(lam_re, lam_im, b_re, b_im, c_re, c_im, log_step):
    step = jnp.exp(log_step)[..., None]
    mag = jnp.exp(lam_re * step)
    lr = mag * jnp.cos(lam_im * step)
    li = mag * jnp.sin(lam_im * step)
    den = lam_re * lam_re + lam_im * lam_im
    cr = ((lr - 1.0) * lam_re + li * lam_im) / den
    ci = (li * lam_re - (lr - 1.0) * lam_im) / den
    bbr = cr[..., None] * b_re - ci[..., None] * b_im
    bbi = cr[..., None] * b_im + ci[..., None] * b_re
    eye = jnp.eye(SSM_GROUPS, dtype=F32)

    def in_mat(b):
        return jnp.einsum('drgph,gk->drghkp', b, eye).reshape(DEPTH, 2, SSM_WIDTH, SSM_LANES)

    def out_mat(cw):
        return jnp.einsum('drghp,gk->drkpgh', cw, eye).reshape(DEPTH, 2, SSM_LANES, SSM_WIDTH)

    bm = jnp.concatenate([in_mat(bbr), in_mat(bbi)], axis=-1).astype(BF16)
    cm = jnp.concatenate([out_mat(c_re), -out_mat(c_im)], axis=-2).astype(BF16)
    lrow = jnp.concatenate([lr.reshape(DEPTH, 2, SSM_LANES), li.reshape(DEPTH, 2, SSM_LANES)], axis=-1)
    lam = jnp.broadcast_to(lrow[:, :, None, :], (DEPTH, 2, SUBLANES, 2 * SSM_LANES))
    return bm, cm, lam


def _stream_layer(x, mod, pw, layer, batch, seq, rows_per_group, tables, ctx_kv, h0):
    latent = ctx_kv is not None
    groups = batch // SUBLANES
    x1, qm, k, v, kd, vd, f, us = _ffn_in_call(x, mod, pw, layer, tables, latent, batch, seq, rows_per_group)
    if latent:
        a = _lat_attn_call(qm, kd, vd, ctx_kv[0], ctx_kv[1], tables['bias'], pw['sink'], layer, batch, seq)
    else:
        a = _ctx_attn_call(qm, kd, vd, pw['sink'], layer, batch, seq)
    dm, wc = tables['dft'][seq]
    fz = _fnet_call(f, wc, dm, pw['w_fnet'], layer, batch, seq)
    yf, yb, fin = _s5_call(us, h0, pw, layer, groups, seq)
    x = _out_ffn_call(x1, mod, a, fz, us, yf, yb, pw, layer, seq, rows_per_group)
    return x, (k, v, fin)


def kernel(x_prompt, x_sample, cache_k, cache_v, state_ssm_re, state_ssm_im, c, c_ctx, w_mod, b_mod, norm_g, ffn_w1, ffn_w3, ffn_w2, w_in, w_out, q_norm_g, k_norm_g, attn_sink, w_fnet, ssm_lambda_re, ssm_lambda_im, ssm_b_re, ssm_b_im, ssm_c_re, ssm_c_im, ssm_d, ssm_log_step, ssm_w_glu):
    batch, seq, _ = x_prompt.shape
    dec_batch, dec_seq, _ = x_sample.shape
    past = cache_k.shape[2]

    tables = {'e': _head_mean_matrix(), 'rope': _rope_tables(dec_seq), 'bias': _window_bias(past),
              'dft': {s: _dft_tables(s) for s in {seq, dec_seq}}}

    bm, cm, lam = _s5_matrices(ssm_lambda_re, ssm_lambda_im, ssm_b_re, ssm_b_im, ssm_c_re, ssm_c_im,
                               ssm_log_step)
    pw = {'g': norm_g.reshape(DEPTH, 3, 1, D_MODEL),
          'w1': ffn_w1.astype(BF16), 'w3': ffn_w3.astype(BF16), 'w2': ffn_w2.astype(BF16),
          'w_in': w_in.astype(BF16), 'w_out': w_out.astype(BF16),
          'qg': jnp.tile(q_norm_g, (1, N_HEADS)).reshape(DEPTH, 1, ATTN_WIDTH),
          'kg': jnp.tile(k_norm_g, (1, N_KV_HEADS)).reshape(DEPTH, 1, KV_WIDTH),
          'sink': jnp.broadcast_to(attn_sink[:, :, None], (DEPTH, N_HEADS, LANES)),
          'w_fnet': w_fnet.astype(BF16), 'bm': bm, 'cm': cm, 'lam': lam,
          'd': ssm_d.reshape(DEPTH, 1, SSM_WIDTH), 'w_glu': ssm_w_glu.astype(BF16)}

    cond = jnp.zeros((COND_ROWS, D_MODEL), F32).at[0].set(c_ctx).at[1:1 + dec_batch].set(c)
    mods = _mod_call(cond, w_mod, b_mod).reshape(DEPTH, COND_ROWS, N_MOD, D_MODEL)

    ctx_kv = (cache_k.reshape(dec_batch, DEPTH, past, KV_WIDTH), cache_v.reshape(dec_batch, DEPTH, past, KV_WIDTH))
    h0_lat = jnp.concatenate([state_ssm_re.reshape(dec_batch, DEPTH, 2, SSM_LANES),
                              state_ssm_im.reshape(dec_batch, DEPTH, 2, SSM_LANES)], axis=-1)
    h0_lat = h0_lat.reshape(dec_batch // SUBLANES, SUBLANES, DEPTH, 2, 2 * SSM_LANES).transpose(2, 0, 3, 1, 4)
    h0_ctx = jnp.zeros((batch // SUBLANES, 2, SUBLANES, 2 * SSM_LANES), F32)

    yp = x_prompt.reshape(batch * seq, D_MODEL)
    ys = x_sample.reshape(dec_batch * dec_seq, D_MODEL)
    ks, vs, sre, sim = [], [], [], []
    for l in range(DEPTH):
        yp, (k_l, v_l, fin) = _stream_layer(yp, mods[l, 0:1], pw, l, batch, seq, batch * seq, tables,
                                            None, h0_ctx)
        ks.append(k_l.reshape(batch, seq, N_KV_HEADS, HEAD_DIM))
        vs.append(v_l.reshape(batch, seq, N_KV_HEADS, HEAD_DIM))
        fin = fin.reshape(batch // SUBLANES, 2, SUBLANES, 2, SSM_GROUPS, SSM_STATE)
        fin = fin.transpose(3, 0, 2, 1, 4, 5).reshape(2, batch, 2, SSM_GROUPS, SSM_STATE)
        sre.append(fin[0])
        sim.append(fin[1])
        ys, _ = _stream_layer(ys, mods[l, 1:1 + dec_batch], pw, l, dec_batch, dec_seq, dec_seq, tables,
                              ctx_kv, h0_lat[l])

    return (yp.reshape(batch, seq, D_MODEL), ys.reshape(dec_batch, dec_seq, D_MODEL),
            jnp.stack(ks, axis=1), jnp.stack(vs, axis=1), jnp.stack(sre, axis=1), jnp.stack(sim, axis=1))
```

```python
import functools
import math

import numpy as np
import jax
import jax.numpy as jnp
from jax import lax
from jax.experimental import pallas as pl
from jax.experimental.pallas import tpu as pltpu

F32 = jnp.float32
BF16 = jnp.bfloat16

D_MODEL = 1024
DEPTH = 2
GRID_W = 64
HEAD_DIM = 64
N_HEADS = 8
N_KV_HEADS = 2
GQA_GROUP = N_HEADS // N_KV_HEADS
ATTN_WIDTH = N_HEADS * HEAD_DIM
KV_WIDTH = N_KV_HEADS * HEAD_DIM
WINDOW = 128
BLOCK = 128
ATTN_SCALE = HEAD_DIM ** -0.5
ROPE_BASE = 10000.0
NEG_INF = -1e30
LOG2E = math.log2(math.e)
FNET_HEADS = 4
FNET_HEAD_DIM = 64
FNET_WIDTH = FNET_HEADS * FNET_HEAD_DIM
SSM_WIDTH = 256
SSM_GROUP = 16
SSM_GROUPS = SSM_WIDTH // SSM_GROUP
SSM_STATE = 64
SSM_LANES = SSM_GROUPS * SSM_STATE
MIX_WIDTH = ATTN_WIDTH + FNET_WIDTH + SSM_WIDTH
Q_END = ATTN_WIDTH
K_END = Q_END + KV_WIDTH
V_END = K_END + KV_WIDTH
F_END = V_END + FNET_WIDTH
S_END = F_END + SSM_WIDTH
IN_WIDTH = S_END
D_FF = 2816
N_MOD = 9
NORM_EPS = 1e-6

LANES = 128
SUBLANES = 8
VMEM_LIMIT_BYTES = 60 * 1024 * 1024
TOKEN_TILE = 1024
FF_CHUNKS = 11
FF_CHUNK = D_FF // FF_CHUNKS
ATTN_BLOCKS_PER_STEP = 4
SCAN_STEPS = 64
SCAN_LANE_CHUNK = 256
COND_ROWS = 16
LOCAL_SPAN = BLOCK + 2 * WINDOW
QM_WIDTH = N_HEADS * LANES
KD_WIDTH = N_KV_HEADS * LANES
VD_WIDTH = N_KV_HEADS * 2 * LANES
WIDE = SUBLANES * SSM_WIDTH


def _params(*sem):
    return pltpu.CompilerParams(dimension_semantics=sem, vmem_limit_bytes=VMEM_LIMIT_BYTES)


def _dot(a, b):
    return jnp.dot(a, b, preferred_element_type=F32)


def _norm_mod(x, g, shift, scale):
    y = x * lax.rsqrt(jnp.mean(x * x, axis=-1, keepdims=True) + NORM_EPS)
    return (y * g) * (1.0 + scale) + shift


def _const_spec(shape, lead=()):
    nd = len(shape)
    idx = tuple(lead) + (0,) * nd
    return pl.BlockSpec((None,) * len(lead) + tuple(shape), lambda *_: idx,
                        pipeline_mode=pl.Buffered(1))


def _row_spec(tm, width):
    return pl.BlockSpec((tm, width), lambda i: (i, 0))


def _mod_spec(tiles_per_group):
    return pl.BlockSpec((1, N_MOD, D_MODEL), lambda i: (i // tiles_per_group, 0, 0))


def _wide_spec(tm, seq):
    if tm <= seq:
        tiles_per_seq = seq // tm

        def index(i):
            b = i // tiles_per_seq
            return (b // SUBLANES, i % tiles_per_seq, b % SUBLANES)
        return pl.BlockSpec((None, tm, SSM_WIDTH), index)
    nseq = tm // seq

    def index(i):
        b = i * nseq
        return (b // SUBLANES, 0, (b % SUBLANES) // nseq)
    return pl.BlockSpec((None, seq, nseq * SSM_WIDTH), index)


def _wide_store(ref, x, seq):
    for j in range(x.shape[0] // seq if x.shape[0] > seq else 1):
        rows = slice(j * seq, (j + 1) * seq) if x.shape[0] > seq else slice(None)
        ref[:, j * SSM_WIDTH:(j + 1) * SSM_WIDTH] = x[rows]


def _wide_load(ref, seq):
    ncol = ref.shape[1] // SSM_WIDTH
    if ncol == 1:
        return ref[...]
    return jnp.concatenate([ref[:, j * SSM_WIDTH:(j + 1) * SSM_WIDTH] for j in range(ncol)], axis=0)


def _mod_kernel(c_ref, w_ref, b_ref, o_ref):
    c = c_ref[...]
    a = (c * jax.nn.sigmoid(c)).astype(BF16)
    o_ref[0] = _dot(a, w_ref[0].astype(BF16)) + b_ref[0]


def _mod_call(cond, w_mod, b_mod):
    tn = 1024
    return pl.pallas_call(
        _mod_kernel,
        out_shape=jax.ShapeDtypeStruct((DEPTH, COND_ROWS, N_MOD * D_MODEL), F32),
        grid=(DEPTH, N_MOD * D_MODEL // tn),
        in_specs=[pl.BlockSpec((COND_ROWS, D_MODEL), lambda l, j: (0, 0)),
                  pl.BlockSpec((1, D_MODEL, tn), lambda l, j: (l, 0, j)),
                  pl.BlockSpec((1, 1, tn), lambda l, j: (l, 0, j))],
        out_specs=pl.BlockSpec((1, COND_ROWS, tn), lambda l, j: (l, 0, j)),
        compiler_params=_params("arbitrary", "arbitrary"),
        name="cond_mod",
    )(cond, w_mod, b_mod.reshape(DEPTH, 1, N_MOD * D_MODEL))


def _half_ffn(x, m, k0, g, w1_ref, w3_ref, w2_ref):
    h = _norm_mod(x, g, m[k0:k0 + 1], m[k0 + 1:k0 + 2]).astype(BF16)
    y = None
    for c in range(FF_CHUNKS):
        cols = slice(c * FF_CHUNK, (c + 1) * FF_CHUNK)
        a = _dot(h, w1_ref[:, cols])
        b = _dot(h, w3_ref[:, cols])
        t = ((a * jax.nn.sigmoid(a)) * b).astype(BF16)
        yc = _dot(t, w2_ref[cols, :])
        y = yc if y is None else y + yc
    return x + (0.5 * m[k0 + 2:k0 + 3]) * y


def _head_norm(z, e, g):
    ms = _dot((z * z).astype(BF16), e)
    return (z * lax.rsqrt(ms + NORM_EPS)) * g


def _low_half(shape):
    return lax.broadcasted_iota(jnp.int32, shape, 1) < HEAD_DIM


def _kv_slabs(k, v):
    low = _low_half(k.shape)
    kr = pltpu.roll(k, HEAD_DIM, 1)
    vr = pltpu.roll(v, HEAD_DIM, 1)
    kd = jnp.concatenate([jnp.where(low, k, kr), jnp.where(low, kr, k)], axis=1)
    vd = jnp.concatenate([jnp.where(low, v, 1.0), jnp.where(low, 1.0, vr),
                          jnp.where(low, vr, 1.0), jnp.where(low, 1.0, v)], axis=1)
    return kd.astype(BF16), vd.astype(BF16)


def _ffn_in_kernel(*refs, rope, seq):
    if rope:
        (x_ref, mod_ref, g0_ref, g1_ref, w1_ref, w3_ref, w2_ref, w_ref, qg_ref, kg_ref, e_ref,
         cos_ref, sa_ref, sb_ref, x1_ref, qm_ref, k_ref, v_ref, kd_ref, vd_ref, f_ref, s_ref) = refs
    else:
        (x_ref, mod_ref, g0_ref, g1_ref, w1_ref, w3_ref, w2_ref, w_ref, qg_ref, kg_ref, e_ref,
         x1_ref, qm_ref, k_ref, v_ref, kd_ref, vd_ref, f_ref, s_ref) = refs
    m = mod_ref[0]
    x1 = _half_ffn(x_ref[...], m, 0, g0_ref[...], w1_ref, w3_ref, w2_ref)
    x1_ref[...] = x1
    h = _norm_mod(x1, g1_ref[...], m[3:4], m[4:5]).astype(BF16)
    u = _dot(h, w_ref[...])
    q = _head_norm(u[:, :Q_END], e_ref[...], qg_ref[...])
    k = _head_norm(u[:, Q_END:K_END], e_ref[0:KV_WIDTH, 0:KV_WIDTH], kg_ref[...])
    v = u[:, K_END:V_END]
    if rope:
        cos, sa, sb = cos_ref[...], sa_ref[...], sb_ref[...]

        def rot(z):
            return z * cos + pltpu.roll(z, LANES - 16, 1) * sa + pltpu.roll(z, 16, 1) * sb
    else:
        def rot(z):
            return z

    low = _low_half((q.shape[0], LANES))
    for i in range(ATTN_WIDTH // LANES):
        qs = rot(q[:, i * LANES:(i + 1) * LANES]) * (ATTN_SCALE * LOG2E)
        qm_ref[:, (2 * i) * LANES:(2 * i + 1) * LANES] = jnp.where(low, qs, 0.0).astype(BF16)
        qm_ref[:, (2 * i + 1) * LANES:(2 * i + 2) * LANES] = jnp.where(low, 0.0, qs).astype(BF16)
    k = rot(k)
    k_ref[...] = k
    v_ref[...] = v
    kd_ref[...], vd_ref[...] = _kv_slabs(k, v)
    f_ref[...] = u[:, V_END:F_END].astype(BF16)
    _wide_store(s_ref, u[:, F_END:S_END], seq)


def _token_tile(seq):
    return TOKEN_TILE if TOKEN_TILE <= seq else min(TOKEN_TILE, SUBLANES * seq // 2)


def _ffn_in_call(x, mod, pw, layer, tables, rope, batch, seq, rows_per_group):
    n = x.shape[0]
    tm = _token_tile(seq)
    tiles_per_seq = max(seq // tm, 1)
    in_specs = [_row_spec(tm, D_MODEL), _mod_spec(rows_per_group // tm),
                _const_spec((1, D_MODEL), (layer, 0)),
                _const_spec((1, D_MODEL), (layer, 1)),
                _const_spec((D_MODEL, D_FF), (layer, 0)),
                _const_spec((D_MODEL, D_FF), (layer, 0)),
                _const_spec((D_FF, D_MODEL), (layer, 0)),
                _const_spec((D_MODEL, IN_WIDTH), (layer,)),
                _const_spec((1, ATTN_WIDTH), (layer,)),
                _const_spec((1, KV_WIDTH), (layer,)),
                _const_spec((ATTN_WIDTH, ATTN_WIDTH))]
    args = [x, mod, pw['g'], pw['g'], pw['w1'], pw['w3'], pw['w2'], pw['w_in'], pw['qg'], pw['kg'],
            tables['e']]
    if rope:
        in_specs += [pl.BlockSpec((tm, LANES), lambda i: (i % tiles_per_seq, 0))] * 3
        args += list(tables['rope'])
    widths = (D_MODEL, QM_WIDTH, KV_WIDTH, KV_WIDTH, KD_WIDTH, VD_WIDTH, FNET_WIDTH)
    dtypes = (F32, BF16, F32, F32, BF16, BF16, BF16)
    out_shape = [jax.ShapeDtypeStruct((n, w), d) for w, d in zip(widths, dtypes)]
    out_shape.append(jax.ShapeDtypeStruct((batch // SUBLANES, seq, WIDE), F32))
    out_specs = [_row_spec(tm, w) for w in widths] + [_wide_spec(tm, seq)]
    return pl.pallas_call(
        functools.partial(_ffn_in_kernel, rope=rope, seq=seq),
        out_shape=out_shape,
        grid=(n // tm,),
        in_specs=in_specs,
        out_specs=out_specs,
        compiler_params=_params("arbitrary"),
        name="ffn_in_proj",
    )(*args)


def _attend(qm_ref, o_ref, rows, sink_ref, kv, bias):
    nq = rows.stop - rows.start
    low = _low_half((nq, LANES))
    for h in range(N_KV_HEADS):
        j0 = GQA_GROUP * h
        qst = jnp.concatenate([qm_ref[rows, (j0 + g) * LANES:(j0 + g + 1) * LANES]
                               for g in range(GQA_GROUP)], axis=0)
        sink = jnp.concatenate([jnp.broadcast_to(sink_ref[j0 + g:j0 + g + 1, :], (nq, LANES))
                                for g in range(GQA_GROUP)], axis=0) * LOG2E
        kd, vd = kv(h)
        s = lax.dot_general(qst, kd, (((1,), (1,)), ((), ())), preferred_element_type=F32)
        if bias is not None:
            s = s + bias
        m = jnp.maximum(jnp.max(s, axis=-1, keepdims=True), sink)
        e = jnp.exp2(s - jnp.concatenate([m] * (s.shape[1] // LANES), axis=1)).astype(BF16)
        o = _dot(e, vd)
        es = jnp.exp2(sink - m)
        for t in range(GQA_GROUP // 2):
            r0, r1 = (2 * t) * nq, (2 * t + 1) * nq
            num = jnp.where(low, o[r0:r0 + nq, :LANES], o[r1:r1 + nq, LANES:])
            den = (jnp.where(low, o[r0:r0 + nq, LANES:], o[r1:r1 + nq, :LANES])
                   + jnp.where(low, es[r0:r0 + nq], es[r1:r1 + nq]))
            c0 = (j0 // 2 + t) * LANES
            o_ref[rows, c0:c0 + LANES] = (num * (1.0 / den)).astype(o_ref.dtype)


def _ctx_attn_kernel(qm_ref, kd_ref, vd_ref, sink_ref, o_ref, *, seq, nseq):
    for i in range(nseq):
        rows = slice(i * seq, (i + 1) * seq)

        def kv(h, rows=rows):
            return (kd_ref[rows, h * LANES:(h + 1) * LANES], vd_ref[rows, 2 * h * LANES:2 * (h + 1) * LANES])

        _attend(qm_ref, o_ref, rows, sink_ref, kv, None)


def _ctx_attn_call(qm, kd, vd, sink, layer, batch, seq):
    nseq = ATTN_BLOCKS_PER_STEP
    rows = nseq * seq
    return pl.pallas_call(
        functools.partial(_ctx_attn_kernel, seq=seq, nseq=nseq),
        out_shape=jax.ShapeDtypeStruct((batch * seq, ATTN_WIDTH), BF16),
        grid=(batch // nseq,),
        in_specs=[_row_spec(rows, QM_WIDTH), _row_spec(rows, KD_WIDTH), _row_spec(rows, VD_WIDTH),
                  pl.BlockSpec((None, N_HEADS, LANES), lambda b: (layer, 0, 0))],
        out_specs=_row_spec(rows, ATTN_WIDTH),
        compiler_params=_params("arbitrary"),
        name="ctx_attn",
    )(qm, kd, vd, sink)


def _lat_attn_kernel(qm_ref, kl_ref, vl_ref, kc_ref, vc_ref, bias_ref, sink_ref, o_ref,
                     kc_sc, vc_sc, *, seq, nblk):
    j = pl.program_id(1)
    nb = seq // BLOCK

    @pl.when(j == 0)
    def _():
        kc_sc[...], vc_sc[...] = _kv_slabs(kc_ref[...], vc_ref[...])

    for i in range(nblk):
        n = j * nblk + i
        start = pl.multiple_of(jnp.clip(n * BLOCK - WINDOW, 0, seq - LOCAL_SPAN), BLOCK)
        variant = jnp.where(n == 0, 0, jnp.where(n == nb - 1, 2, 1))

        def kv(h, start=start):
            kc, vc = slice(h * LANES, (h + 1) * LANES), slice(2 * h * LANES, 2 * (h + 1) * LANES)
            return (jnp.concatenate([kc_sc[:, kc], kl_ref[pl.ds(start, LOCAL_SPAN), kc]], axis=0),
                    jnp.concatenate([vc_sc[:, vc], vl_ref[pl.ds(start, LOCAL_SPAN), vc]], axis=0))

        _attend(qm_ref, o_ref, slice(i * BLOCK, (i + 1) * BLOCK), sink_ref, kv, bias_ref[variant])


def _lat_attn_call(qm, kd, vd, kctx, vctx, bias, sink, layer, batch, seq):
    past = kctx.shape[2]
    nblk = ATTN_BLOCKS_PER_STEP
    steps = seq // (BLOCK * nblk)
    rows = BLOCK * nblk
    return pl.pallas_call(
        functools.partial(_lat_attn_kernel, seq=seq, nblk=nblk),
        out_shape=jax.ShapeDtypeStruct((batch * seq, ATTN_WIDTH), BF16),
        grid=(batch, steps),
        in_specs=[pl.BlockSpec((rows, QM_WIDTH), lambda b, j: (b * steps + j, 0)),
                  pl.BlockSpec((seq, KD_WIDTH), lambda b, j: (b, 0)),
                  pl.BlockSpec((seq, VD_WIDTH), lambda b, j: (b, 0)),
                  pl.BlockSpec((None, None, past, KV_WIDTH), lambda b, j: (b, layer, 0, 0)),
                  pl.BlockSpec((None, None, past, KV_WIDTH), lambda b, j: (b, layer, 0, 0)),
                  pl.BlockSpec(bias.shape, lambda b, j: (0, 0, 0)),
                  pl.BlockSpec((None, N_HEADS, LANES), lambda b, j: (layer, 0, 0))],
        out_specs=pl.BlockSpec((rows, ATTN_WIDTH), lambda b, j: (b * steps + j, 0)),
        scratch_shapes=[pltpu.VMEM((past, KD_WIDTH), BF16), pltpu.VMEM((past, VD_WIDTH), BF16)],
        compiler_params=_params("arbitrary", "arbitrary"),
        name="lat_attn",
    )(qm, kd, vd, kctx, vctx, bias, sink)


def _fnet_kernel(x_ref, wc_ref, dm_ref, wf_ref, o_ref, *, seq):
    y = _dot(x_ref[...], wc_ref[...])
    yst = jnp.concatenate([y[:, :FNET_WIDTH], y[:, FNET_WIDTH:]], axis=0).astype(BF16)
    z = _dot(dm_ref[...], yst) * ((seq * FNET_HEAD_DIM) ** -0.5)
    o_ref[...] = _dot(z.astype(BF16), wf_ref[...]).astype(o_ref.dtype)


def _fnet_call(x, wc, dm, wf, layer, batch, seq):
    return pl.pallas_call(
        functools.partial(_fnet_kernel, seq=seq),
        out_shape=jax.ShapeDtypeStruct((batch * seq, FNET_WIDTH), BF16),
        grid=(batch,),
        in_specs=[_row_spec(seq, FNET_WIDTH),
                  _const_spec((FNET_WIDTH, 2 * FNET_WIDTH)),
                  _const_spec((seq, 2 * seq)),
                  _const_spec((FNET_WIDTH, FNET_WIDTH), (layer,))],
        out_specs=_row_spec(seq, FNET_WIDTH),
        compiler_params=_params("arbitrary"),
        name="fourier_mix",
    )(x, wc, dm, wf)


def _s5_kernel(xf_ref, xb_ref, xfn_ref, xbn_ref, h0_ref, bm_ref, cm_ref, lam_ref,
               yf_ref, yb_ref, fin_ref, xs, buf0, buf1, ys, carry, *, steps, nchunks):
    first = (pl.program_id(0) == 0) & (pl.program_id(1) == 0)
    c = pl.program_id(1)
    nslab = SSM_WIDTH // LANES
    w = SCAN_LANE_CHUNK

    def relayout_in(x_refs, base):
        xin = []
        for d, x_ref in enumerate(x_refs):
            for b in range(SUBLANES):
                for sl in range(nslab):
                    c0 = b * SSM_WIDTH + sl * LANES
                    xs[base + d * nslab + sl, pl.ds(b, steps, stride=SUBLANES), :] = x_ref[:, c0:c0 + LANES]
            xin.append(jnp.concatenate([xs[base + d * nslab + sl] for sl in range(nslab)],
                                       axis=1).astype(BF16))
        return xin

    def step(cur, nxt):
        xin = relayout_in((xfn_ref, xbn_ref), 2 * nslab)
        yacc = [None, None]
        for lc in range(SSM_LANES // w):
            re = slice(lc * w, (lc + 1) * w)
            im = slice(SSM_LANES + lc * w, SSM_LANES + (lc + 1) * w)
            for d in range(2):
                nxt[d, :, re] = _dot(xin[d], bm_ref[d, :, re])
                nxt[d, :, im] = _dot(xin[d], bm_ref[d, :, im])
            lam = [(lam_ref[d, :, re], lam_ref[d, :, im]) for d in range(2)]
            st = [(carry[d, :, re], carry[d, :, im]) for d in range(2)]
            for t in range(steps):
                for d in range(2):
                    r0 = (t if d == 0 else steps - 1 - t) * SUBLANES
                    rows = slice(r0, r0 + SUBLANES)
                    (lr, li), (sr, si) = lam[d], st[d]
                    nr = (lr * sr - li * si) + cur[d, rows, re]
                    ni = (lr * si + li * sr) + cur[d, rows, im]
                    cur[d, rows, re] = nr
                    cur[d, rows, im] = ni
                    st[d] = (nr, ni)
            for d in range(2):
                carry[d, :, re], carry[d, :, im] = st[d]
            for d in range(2):
                part = (_dot(cur[d, :, re].astype(BF16), cm_ref[d, re, :])
                        + _dot(cur[d, :, im].astype(BF16), cm_ref[d, im, :]))
                yacc[d] = part if yacc[d] is None else yacc[d] + part
        for d, y_ref in enumerate((yf_ref, yb_ref)):
            for sl in range(nslab):
                ys[d * nslab + sl] = yacc[d][:, sl * LANES:(sl + 1) * LANES]
            for b in range(SUBLANES):
                for sl in range(nslab):
                    c0 = b * SSM_WIDTH + sl * LANES
                    y_ref[:, c0:c0 + LANES] = ys[d * nslab + sl, pl.ds(b, steps, stride=SUBLANES), :]

    @pl.when(c == 0)
    def _():
        carry[...] = h0_ref[0]

    @pl.when(first)
    def _():
        xin = relayout_in((xf_ref, xb_ref), 0)
        for d in range(2):
            buf0[d] = _dot(xin[d], bm_ref[d])

    @pl.when(c % 2 == 0)
    def _():
        step(buf0, buf1)

    @pl.when(c % 2 == 1)
    def _():
        step(buf1, buf0)

    @pl.when(c == nchunks - 1)
    def _():
        fin_ref[0] = carry[...]


def _s5_call(x_wide, h0, pw, layer, groups, seq):
    steps = SCAN_STEPS
    rows = steps * SUBLANES
    nchunks = seq // steps
    assert nchunks % 2 == 0
    width = 2 * SSM_LANES
    nslab = SSM_WIDTH // LANES
    last = groups * nchunks - 1

    def nxt(g, c):
        s = jnp.minimum(g * nchunks + c + 1, last)
        return s // nchunks, s % nchunks

    def fwd_next(g, c):
        g2, c2 = nxt(g, c)
        return (g2, c2, 0)

    def bwd_next(g, c):
        g2, c2 = nxt(g, c)
        return (g2, nchunks - 1 - c2, 0)

    return pl.pallas_call(
        functools.partial(_s5_kernel, steps=steps, nchunks=nchunks),
        out_shape=[jax.ShapeDtypeStruct((groups, seq, WIDE), F32),
                   jax.ShapeDtypeStruct((groups, seq, WIDE), F32),
                   jax.ShapeDtypeStruct((groups, 2, SUBLANES, width), F32)],
        grid=(groups, nchunks),
        in_specs=[pl.BlockSpec((None, steps, WIDE), lambda g, c: (0, 0, 0)),
                  pl.BlockSpec((None, steps, WIDE), lambda g, c: (0, nchunks - 1, 0)),
                  pl.BlockSpec((None, steps, WIDE), fwd_next),
                  pl.BlockSpec((None, steps, WIDE), bwd_next),
                  pl.BlockSpec((1, 2, SUBLANES, width), lambda g, c: (g, 0, 0, 0)),
                  pl.BlockSpec((None, 2, SSM_WIDTH, width), lambda g, c: (layer, 0, 0, 0)),
                  pl.BlockSpec((None, 2, width, SSM_WIDTH), lambda g, c: (layer, 0, 0, 0)),
                  pl.BlockSpec((None, 2, SUBLANES, width), lambda g, c: (layer, 0, 0, 0))],
        out_specs=[pl.BlockSpec((None, steps, WIDE), lambda g, c: (g, c, 0)),
                   pl.BlockSpec((None, steps, WIDE), lambda g, c: (g, nchunks - 1 - c, 0)),
                   pl.BlockSpec((1, 2, SUBLANES, width), lambda g, c: (g, 0, 0, 0))],
        scratch_shapes=[pltpu.VMEM((4 * nslab, rows, LANES), F32),
                        pltpu.VMEM((2, rows, width), F32),
                        pltpu.VMEM((2, rows, width), F32),
                        pltpu.VMEM((2 * nslab, rows, LANES), F32),
                        pltpu.VMEM((2, SUBLANES, width), F32)],
        compiler_params=_params("arbitrary", "arbitrary"),
        name="s5_scan",
    )(x_wide, x_wide, x_wide, x_wide, h0, pw['bm'], pw['cm'], pw['lam'])


def _out_ffn_kernel(x_ref, mod_ref, a_ref, f_ref, us_ref, yf_ref, yb_ref, d_ref, wg_ref, wo_ref,
                    g_ref, w1_ref, w3_ref, w2_ref, o_ref, *, seq):
    m = mod_ref[0]
    us, yf, yb = (_wide_load(r, seq) for r in (us_ref, yf_ref, yb_ref))
    ys = jax.nn.gelu((d_ref[...] * us + yf) + yb)
    gl = _dot(ys.astype(BF16), wg_ref[...])
    so = gl[:, :SSM_WIDTH] * jax.nn.sigmoid(gl[:, SSM_WIDTH:])
    mixed = jnp.concatenate([a_ref[...], f_ref[...], so.astype(BF16)], axis=1)
    x2 = x_ref[...] + m[5:6] * _dot(mixed, wo_ref[...])
    o_ref[...] = _half_ffn(x2, m, 6, g_ref[...], w1_ref, w3_ref, w2_ref)


def _out_ffn_call(x, mod, a, f, us, yf, yb, pw, layer, seq, rows_per_group):
    n = x.shape[0]
    tm = _token_tile(seq)
    wide = _wide_spec(tm, seq)
    return pl.pallas_call(
        functools.partial(_out_ffn_kernel, seq=seq),
        out_shape=jax.ShapeDtypeStruct((n, D_MODEL), F32),
        grid=(n // tm,),
        in_specs=[_row_spec(tm, D_MODEL), _mod_spec(rows_per_group // tm),
                  _row_spec(tm, ATTN_WIDTH), _row_spec(tm, FNET_WIDTH), wide, wide, wide,
                  _const_spec((1, SSM_WIDTH), (layer,)),
                  _const_spec((SSM_WIDTH, 2 * SSM_WIDTH), (layer,)),
                  _const_spec((MIX_WIDTH, D_MODEL), (layer,)),
                  _const_spec((1, D_MODEL), (layer, 2)),
                  _const_spec((D_MODEL, D_FF), (layer, 1)),
                  _const_spec((D_MODEL, D_FF), (layer, 1)),
                  _const_spec((D_FF, D_MODEL), (layer, 1))],
        out_specs=_row_spec(tm, D_MODEL),
        compiler_params=_params("arbitrary"),
        name="out_proj_ffn",
    )(x, mod, a, f, us, yf, yb, pw['d'], pw['w_glu'], pw['w_out'], pw['g'], pw['w1'], pw['w3'], pw['w2'])


def _rope_tables(seq):
    pos = np.arange(seq)
    row, col = pos // GRID_W, pos % GRID_W
    quarter = HEAD_DIM // 4
    inv_freq = 1.0 / (ROPE_BASE ** (np.arange(quarter, dtype=np.float64) * 2.0 / (HEAD_DIM // 2)))
    lane = np.arange(LANES)
    in_head = lane % HEAD_DIM
    p = np.where((in_head < HEAD_DIM // 2)[None, :], row[:, None], col[:, None]).astype(np.float64)
    ang = p * inv_freq[lane % quarter][None, :]
    first = ((lane % (HEAD_DIM // 2)) < quarter)[None, :]
    cos = np.cos(ang)
    sa = np.where(first, -np.sin(ang), 0.0)
    sb = np.where(first, 0.0, np.sin(ang))
    return tuple(jnp.asarray(t, dtype=F32) for t in (cos, sa, sb))


def _dft_tables(seq):
    kl = np.outer(np.arange(seq), np.arange(seq)) % seq
    ang = 2.0 * np.pi * kl / seq
    dm = np.concatenate([np.cos(ang), -np.sin(ang)], axis=1)
    mc = np.outer(np.arange(FNET_HEAD_DIM), np.arange(FNET_HEAD_DIM)) % FNET_HEAD_DIM
    a64 = 2.0 * np.pi * mc / FNET_HEAD_DIM
    eye = np.eye(FNET_HEADS)
    wc = np.concatenate([np.kron(eye, np.cos(a64)), np.kron(eye, np.sin(a64))], axis=1)
    return jnp.asarray(dm, dtype=F32).astype(BF16), jnp.asarray(wc, dtype=F32).astype(BF16)


def _head_mean_matrix():
    return jnp.asarray(np.kron(np.eye(N_HEADS), np.full((HEAD_DIM, HEAD_DIM), 1.0 / HEAD_DIM)), dtype=BF16)


def _window_bias(past):
    i = np.arange(GQA_GROUP * BLOCK)[:, None] % BLOCK
    j = np.arange(LOCAL_SPAN)[None, :]
    out = np.zeros((3, GQA_GROUP * BLOCK, past + LOCAL_SPAN), np.float32)
    for v, off in enumerate((0, WINDOW, 2 * WINDOW)):
        out[v, :, past:] = np.where(np.abs(j - off - i) <= WINDOW, 0.0, NEG_INF)
    return jnp.asarray(out)


def _s5_matrices(lam_re, lam_im, b_re, b_im, c_re, c_im, log_step):
    step = jnp.exp(log_step)[..., None]
    mag = jnp.exp(lam_re * step)
    lr = mag * jnp.cos(lam_im * step)
    li = mag * jnp.sin(lam_im * step)
    den = lam_re * lam_re + lam_im * lam_im
    cr = ((lr - 1.0) * lam_re + li * lam_im) / den
    ci = (li * lam_re - (lr - 1.0) * lam_im) / den
    bbr = cr[..., None] * b_re - ci[..., None] * b_im
    bbi = cr[..., None] * b_im + ci[..., None] * b_re
    eye = jnp.eye(SSM_GROUPS, dtype=F32)

    def in_mat(b):
        return jnp.einsum('drgph,gk->drghkp', b, eye).reshape(DEPTH, 2, SSM_WIDTH, SSM_LANES)

    def out_mat(cw):
        return jnp.einsum('drghp,gk->drkpgh', cw, eye).reshape(DEPTH, 2, SSM_LANES, SSM_WIDTH)

    bm = jnp.concatenate([in_mat(bbr), in_mat(bbi)], axis=-1).astype(BF16)
    cm = jnp.concatenate([out_mat(c_re), -out_mat(c_im)], axis=-2).astype(BF16)
    lrow = jnp.concatenate([lr.reshape(DEPTH, 2, SSM_LANES), li.reshape(DEPTH, 2, SSM_LANES)], axis=-1)
    lam = jnp.broadcast_to(lrow[:, :, None, :], (DEPTH, 2, SUBLANES, 2 * SSM_LANES))
    return bm, cm, lam


def _stream_layer(x, mod, pw, layer, batch, seq, rows_per_group, tables, ctx_kv, h0):
    latent = ctx_kv is not None
    groups = batch // SUBLANES
    x1, qm, k, v, kd, vd, f, us = _ffn_in_call(x, mod, pw, layer, tables, latent, batch, seq, rows_per_group)
    if latent:
        a = _lat_attn_call(qm, kd, vd, ctx_kv[0], ctx_kv[1], tables['bias'], pw['sink'], layer, batch, seq)
    else:
        a = _ctx_attn_call(qm, kd, vd, pw['sink'], layer, batch, seq)
    dm, wc = tables['dft'][seq]
    fz = _fnet_call(f, wc, dm, pw['w_fnet'], layer, batch, seq)
    yf, yb, fin = _s5_call(us, h0, pw, layer, groups, seq)
    x = _out_ffn_call(x1, mod, a, fz, us, yf, yb, pw, layer, seq, rows_per_group)
    return x, (k, v, fin)


def kernel(x_prompt, x_sample, cache_k, cache_v, state_ssm_re, state_ssm_im, c, c_ctx, w_mod, b_mod, norm_g, ffn_w1, ffn_w3, ffn_w2, w_in, w_out, q_norm_g, k_norm_g, attn_sink, w_fnet, ssm_lambda_re, ssm_lambda_im, ssm_b_re, ssm_b_im, ssm_c_re, ssm_c_im, ssm_d, ssm_log_step, ssm_w_glu):
    batch, seq, _ = x_prompt.shape
    dec_batch, dec_seq, _ = x_sample.shape
    past = cache_k.shape[2]

    tables = {'e': _head_mean_matrix(), 'rope': _rope_tables(dec_seq), 'bias': _window_bias(past),
              'dft': {s: _dft_tables(s) for s in {seq, dec_seq}}}

    bm, cm, lam = _s5_matrices(ssm_lambda_re, ssm_lambda_im, ssm_b_re, ssm_b_im, ssm_c_re, ssm_c_im,
                               ssm_log_step)
    pw = {'g': norm_g.reshape(DEPTH, 3, 1, D_MODEL),
          'w1': ffn_w1.astype(BF16), 'w3': ffn_w3.astype(BF16), 'w2': ffn_w2.astype(BF16),
          'w_in': w_in.astype(BF16), 'w_out': w_out.astype(BF16),
          'qg': jnp.tile(q_norm_g, (1, N_HEADS)).reshape(DEPTH, 1, ATTN_WIDTH),
          'kg': jnp.tile(k_norm_g, (1, N_KV_HEADS)).reshape(DEPTH, 1, KV_WIDTH),
          'sink': jnp.broadcast_to(attn_sink[:, :, None], (DEPTH, N_HEADS, LANES)),
          'w_fnet': w_fnet.astype(BF16), 'bm': bm, 'cm': cm, 'lam': lam,
          'd': ssm_d.reshape(DEPTH, 1, SSM_WIDTH), 'w_glu': ssm_w_glu.astype(BF16)}

    cond = jnp.zeros((COND_ROWS, D_MODEL), F32).at[0].set(c_ctx).at[1:1 + dec_batch].set(c)
    mods = _mod_call(cond, w_mod, b_mod).reshape(DEPTH, COND_ROWS, N_MOD, D_MODEL)

    ctx_kv = (cache_k.reshape(dec_batch, DEPTH, past, KV_WIDTH), cache_v.reshape(dec_batch, DEPTH, past, KV_WIDTH))
    h0_lat = jnp.concatenate([state_ssm_re.reshape(dec_batch, DEPTH, 2, SSM_LANES),
                              state_ssm_im.reshape(dec_batch, DEPTH, 2, SSM_LANES)], axis=-1)
    h0_lat = h0_lat.reshape(dec_batch // SUBLANES, SUBLANES, DEPTH, 2, 2 * SSM_LANES).transpose(2, 0, 3, 1, 4)
    h0_ctx = jnp.zeros((batch // SUBLANES, 2, SUBLANES, 2 * SSM_LANES), F32)

    yp = x_prompt.reshape(batch * seq, D_MODEL)
    ys = x_sample.reshape(dec_batch * dec_seq, D_MODEL)
    ks, vs, sre, sim = [], [], [], []
    for l in range(DEPTH):
        yp, (k_l, v_l, fin) = _stream_layer(yp, mods[l, 0:1], pw, l, batch, seq, batch * seq, tables,
                                            None, h0_ctx)
        ks.append(k_l.reshape(batch, seq, N_KV_HEADS, HEAD_DIM))
        vs.append(v_l.reshape(batch, seq, N_KV_HEADS, HEAD_DIM))
        fin = fin.reshape(batch // SUBLANES, 2, SUBLANES, 2, SSM_GROUPS, SSM_STATE)
        fin = fin.transpose(3, 0, 2, 1, 4, 5).reshape(2, batch, 2, SSM_GROUPS, SSM_STATE)
        sre.append(fin[0])
        sim.append(fin[1])
        ys, _ = _stream_layer(ys, mods[l, 1:1 + dec_batch], pw, l, dec_batch, dec_seq, dec_seq, tables,
                              ctx_kv, h0_lat[l])

    return (yp.reshape(batch, seq, D_MODEL), ys.reshape(dec_batch, dec_seq, D_MODEL),
            jnp.stack(ks, axis=1), jnp.stack(vs, axis=1), jnp.stack(sre, axis=1), jnp.stack(sim, axis=1))
```

```python
import functools
import math

import numpy as np
import jax
import jax.numpy as jnp
from jax import lax
from jax.experimental import pallas as pl
from jax.experimental.pallas import tpu as pltpu

F32 = jnp.float32
BF16 = jnp.bfloat16

D_MODEL = 1024
DEPTH = 2
GRID_W = 64
HEAD_DIM = 64
N_HEADS = 8
N_KV_HEADS = 2
GQA_GROUP = N_HEADS // N_KV_HEADS
ATTN_WIDTH = N_HEADS * HEAD_DIM
KV_WIDTH = N_KV_HEADS * HEAD_DIM
WINDOW = 128
BLOCK = 128
ATTN_SCALE = HEAD_DIM ** -0.5
ROPE_BASE = 10000.0
NEG_INF = -1e30
LOG2E = math.log2(math.e)
FNET_HEADS = 4
FNET_HEAD_DIM = 64
FNET_WIDTH = FNET_HEADS * FNET_HEAD_DIM
SSM_WIDTH = 256
SSM_GROUP = 16
SSM_GROUPS = SSM_WIDTH // SSM_GROUP
SSM_STATE = 64
SSM_LANES = SSM_GROUPS * SSM_STATE
MIX_WIDTH = ATTN_WIDTH + FNET_WIDTH + SSM_WIDTH
Q_END = ATTN_WIDTH
K_END = Q_END + KV_WIDTH
V_END = K_END + KV_WIDTH
F_END = V_END + FNET_WIDTH
S_END = F_END + SSM_WIDTH
IN_WIDTH = S_END
D_FF = 2816
N_MOD = 9
NORM_EPS = 1e-6

LANES = 128
SUBLANES = 8
VMEM_LIMIT_BYTES = 60 * 1024 * 1024
TOKEN_TILE = 1024
FF_CHUNKS = 11
FF_CHUNK = D_FF // FF_CHUNKS
ATTN_BLOCKS_PER_STEP = 4
SCAN_STEPS = 64
SCAN_LANE_CHUNK = 256
COND_ROWS = 16
LOCAL_SPAN = BLOCK + 2 * WINDOW
QM_WIDTH = N_HEADS * LANES
KD_WIDTH = N_KV_HEADS * LANES
VD_WIDTH = N_KV_HEADS * 2 * LANES
WIDE = SUBLANES * SSM_WIDTH


def _params(*sem):
    return pltpu.CompilerParams(dimension_semantics=sem, vmem_limit_bytes=VMEM_LIMIT_BYTES)


def _dot(a, b):
    return jnp.dot(a, b, preferred_element_type=F32)


def _norm_mod(x, g, shift, scale):
    y = x * lax.rsqrt(jnp.mean(x * x, axis=-1, keepdims=True) + NORM_EPS)
    return (y * g) * (1.0 + scale) + shift


def _const_spec(shape, lead=()):
    nd = len(shape)
    idx = tuple(lead) + (0,) * nd
    return pl.BlockSpec((None,) * len(lead) + tuple(shape), lambda *_: idx,
                        pipeline_mode=pl.Buffered(1))


def _row_spec(tm, width):
    return pl.BlockSpec((tm, width), lambda i: (i, 0))


def _mod_spec(tiles_per_group):
    return pl.BlockSpec((1, N_MOD, D_MODEL), lambda i: (i // tiles_per_group, 0, 0))


def _wide_spec(tm, seq):
    if tm <= seq:
        tiles_per_seq = seq // tm

        def index(i):
            b = i // tiles_per_seq
            return (b // SUBLANES, i % tiles_per_seq, b % SUBLANES)
        return pl.BlockSpec((None, tm, SSM_WIDTH), index)
    nseq = tm // seq

    def index(i):
        b = i * nseq
        return (b // SUBLANES, 0, (b % SUBLANES) // nseq)
    return pl.BlockSpec((None, seq, nseq * SSM_WIDTH), index)


def _wide_store(ref, x, seq):
    for j in range(x.shape[0] // seq if x.shape[0] > seq else 1):
        rows = slice(j * seq, (j + 1) * seq) if x.shape[0] > seq else slice(None)
        ref[:, j * SSM_WIDTH:(j + 1) * SSM_WIDTH] = x[rows]


def _wide_load(ref, seq):
    ncol = ref.shape[1] // SSM_WIDTH
    if ncol == 1:
        return ref[...]
    return jnp.concatenate([ref[:, j * SSM_WIDTH:(j + 1) * SSM_WIDTH] for j in range(ncol)], axis=0)


def _mod_kernel(c_ref, w_ref, b_ref, o_ref):
    c = c_ref[...]
    a = (c * jax.nn.sigmoid(c)).astype(BF16)
    o_ref[0] = _dot(a, w_ref[0].astype(BF16)) + b_ref[0]


def _mod_call(cond, w_mod, b_mod):
    tn = 1024
    return pl.pallas_call(
        _mod_kernel,
        out_shape=jax.ShapeDtypeStruct((DEPTH, COND_ROWS, N_MOD * D_MODEL), F32),
        grid=(DEPTH, N_MOD * D_MODEL // tn),
        in_specs=[pl.BlockSpec((COND_ROWS, D_MODEL), lambda l, j: (0, 0)),
                  pl.BlockSpec((1, D_MODEL, tn), lambda l, j: (l, 0, j)),
                  pl.BlockSpec((1, 1, tn), lambda l, j: (l, 0, j))],
        out_specs=pl.BlockSpec((1, COND_ROWS, tn), lambda l, j: (l, 0, j)),
        compiler_params=_params("arbitrary", "arbitrary"),
        name="cond_mod",
    )(cond, w_mod, b_mod.reshape(DEPTH, 1, N_MOD * D_MODEL))


def _half_ffn(x, m, k0, g, w1_ref, w3_ref, w2_ref):
    h = _norm_mod(x, g, m[k0:k0 + 1], m[k0 + 1:k0 + 2]).astype(BF16)
    y = None
    for c in range(FF_CHUNKS):
        cols = slice(c * FF_CHUNK, (c + 1) * FF_CHUNK)
        a = _dot(h, w1_ref[:, cols])
        b = _dot(h, w3_ref[:, cols])
        t = ((a * jax.nn.sigmoid(a)) * b).astype(BF16)
        yc = _dot(t, w2_ref[cols, :])
        y = yc if y is None else y + yc
    return x + (0.5 * m[k0 + 2:k0 + 3]) * y


def _head_norm(z, e, g):
    ms = _dot((z * z).astype(BF16), e)
    return (z * lax.rsqrt(ms + NORM_EPS)) * g


def _low_half(shape):
    return lax.broadcasted_iota(jnp.int32, shape, 1) < HEAD_DIM


def _kv_slabs(k, v):
    low = _low_half(k.shape)
    kr = pltpu.roll(k, HEAD_DIM, 1)
    vr = pltpu.roll(v, HEAD_DIM, 1)
    kd = jnp.concatenate([jnp.where(low, k, kr), jnp.where(low, kr, k)], axis=1)
    vd = jnp.concatenate([jnp.where(low, v, 1.0), jnp.where(low, 1.0, vr),
                          jnp.where(low, vr, 1.0), jnp.where(low, 1.0, v)], axis=1)
    return kd.astype(BF16), vd.astype(BF16)


def _ffn_in_kernel(*refs, rope, seq):
    if rope:
        (x_ref, mod_ref, g0_ref, g1_ref, w1_ref, w3_ref, w2_ref, w_ref, qg_ref, kg_ref, e_ref,
         cos_ref, sa_ref, sb_ref, x1_ref, qm_ref, k_ref, v_ref, kd_ref, vd_ref, f_ref, s_ref) = refs
    else:
        (x_ref, mod_ref, g0_ref, g1_ref, w1_ref, w3_ref, w2_ref, w_ref, qg_ref, kg_ref, e_ref,
         x1_ref, qm_ref, k_ref, v_ref, kd_ref, vd_ref, f_ref, s_ref) = refs
    m = mod_ref[0]
    x1 = _half_ffn(x_ref[...], m, 0, g0_ref[...], w1_ref, w3_ref, w2_ref)
    x1_ref[...] = x1
    h = _norm_mod(x1, g1_ref[...], m[3:4], m[4:5]).astype(BF16)
    u = _dot(h, w_ref[...])
    q = _head_norm(u[:, :Q_END], e_ref[...], qg_ref[...])
    k = _head_norm(u[:, Q_END:K_END], e_ref[0:KV_WIDTH, 0:KV_WIDTH], kg_ref[...])
    v = u[:, K_END:V_END]
    if rope:
        cos, sa, sb = cos_ref[...], sa_ref[...], sb_ref[...]

        def rot(z):
            return z * cos + pltpu.roll(z, LANES - 16, 1) * sa + pltpu.roll(z, 16, 1) * sb
    else:
        def rot(z):
            return z

    low = _low_half((q.shape[0], LANES))
    for i in range(ATTN_WIDTH // LANES):
        qs = rot(q[:, i * LANES:(i + 1) * LANES]) * (ATTN_SCALE * LOG2E)
        qm_ref[:, (2 * i) * LANES:(2 * i + 1) * LANES] = jnp.where(low, qs, 0.0).astype(BF16)
        qm_ref[:, (2 * i + 1) * LANES:(2 * i + 2) * LANES] = jnp.where(low, 0.0, qs).astype(BF16)
    k = rot(k)
    k_ref[...] = k
    v_ref[...] = v
    kd_ref[...], vd_ref[...] = _kv_slabs(k, v)
    f_ref[...] = u[:, V_END:F_END].astype(BF16)
    _wide_store(s_ref, u[:, F_END:S_END], seq)


def _token_tile(seq):
    return TOKEN_TILE if TOKEN_TILE <= seq else min(TOKEN_TILE, SUBLANES * seq // 2)


def _ffn_in_call(x, mod, pw, layer, tables, rope, batch, seq, rows_per_group):
    n = x.shape[0]
    tm = _token_tile(seq)
    tiles_per_seq = max(seq // tm, 1)
    in_specs = [_row_spec(tm, D_MODEL), _mod_spec(rows_per_group // tm),
                _const_spec((1, D_MODEL), (layer, 0)),
                _const_spec((1, D_MODEL), (layer, 1)),
                _const_spec((D_MODEL, D_FF), (layer, 0)),
                _const_spec((D_MODEL, D_FF), (layer, 0)),
                _const_spec((D_FF, D_MODEL), (layer, 0)),
                _const_spec((D_MODEL, IN_WIDTH), (layer,)),
                _const_spec((1, ATTN_WIDTH), (layer,)),
                _const_spec((1, KV_WIDTH), (layer,)),
                _const_spec((ATTN_WIDTH, ATTN_WIDTH))]
    args = [x, mod, pw['g'], pw['g'], pw['w1'], pw['w3'], pw['w2'], pw['w_in'], pw['qg'], pw['kg'],
            tables['e']]
    if rope:
        in_specs += [pl.BlockSpec((tm, LANES), lambda i: (i % tiles_per_seq, 0))] * 3
        args += list(tables['rope'])
    widths = (D_MODEL, QM_WIDTH, KV_WIDTH, KV_WIDTH, KD_WIDTH, VD_WIDTH, FNET_WIDTH)
    dtypes = (F32, BF16, F32, F32, BF16, BF16, BF16)
    out_shape = [jax.ShapeDtypeStruct((n, w), d) for w, d in zip(widths, dtypes)]
    out_shape.append(jax.ShapeDtypeStruct((batch // SUBLANES, seq, WIDE), F32))
    out_specs = [_row_spec(tm, w) for w in widths] + [_wide_spec(tm, seq)]
    return pl.pallas_call(
        functools.partial(_ffn_in_kernel, rope=rope, seq=seq),
        out_shape=out_shape,
        grid=(n // tm,),
        in_specs=in_specs,
        out_specs=out_specs,
        compiler_params=_params("arbitrary"),
        name="ffn_in_proj",
    )(*args)


def _attend(qm_ref, o_ref, rows, sink_ref, kv, bias):
    nq = rows.stop - rows.start
    low = _low_half((nq, LANES))
    for h in range(N_KV_HEADS):
        j0 = GQA_GROUP * h
        qst = jnp.concatenate([qm_ref[rows, (j0 + g) * LANES:(j0 + g + 1) * LANES]
                               for g in range(GQA_GROUP)], axis=0)
        sink = jnp.concatenate([jnp.broadcast_to(sink_ref[j0 + g:j0 + g + 1, :], (nq, LANES))
                                for g in range(GQA_GROUP)], axis=0) * LOG2E
        kd, vd = kv(h)
        s = lax.dot_general(qst, kd, (((1,), (1,)), ((), ())), preferred_element_type=F32)
        if bias is not None:
            s = s + bias
        m = jnp.maximum(jnp.max(s, axis=-1, keepdims=True), sink)
        e = jnp.exp2(s - jnp.concatenate([m] * (s.shape[1] // LANES), axis=1)).astype(BF16)
        o = _dot(e, vd)
        es = jnp.exp2(sink - m)
        for t in range(GQA_GROUP // 2):
            r0, r1 = (2 * t) * nq, (2 * t + 1) * nq
            num = jnp.where(low, o[r0:r0 + nq, :LANES], o[r1:r1 + nq, LANES:])
            den = (jnp.where(low, o[r0:r0 + nq, LANES:], o[r1:r1 + nq, :LANES])
                   + jnp.where(low, es[r0:r0 + nq], es[r1:r1 + nq]))
            c0 = (j0 // 2 + t) * LANES
            o_ref[rows, c0:c0 + LANES] = (num * (1.0 / den)).astype(o_ref.dtype)


def _ctx_attn_kernel(qm_ref, kd_ref, vd_ref, sink_ref, o_ref, *, seq, nseq):
    for i in range(nseq):
        rows = slice(i * seq, (i + 1) * seq)

        def kv(h, rows=rows):
            return (kd_ref[rows, h * LANES:(h + 1) * LANES], vd_ref[rows, 2 * h * LANES:2 * (h + 1) * LANES])

        _attend(qm_ref, o_ref, rows, sink_ref, kv, None)


def _ctx_attn_call(qm, kd, vd, sink, layer, batch, seq):
    nseq = ATTN_BLOCKS_PER_STEP
    rows = nseq * seq
    return pl.pallas_call(
        functools.partial(_ctx_attn_kernel, seq=seq, nseq=nseq),
        out_shape=jax.ShapeDtypeStruct((batch * seq, ATTN_WIDTH), BF16),
        grid=(batch // nseq,),
        in_specs=[_row_spec(rows, QM_WIDTH), _row_spec(rows, KD_WIDTH), _row_spec(rows, VD_WIDTH),
                  pl.BlockSpec((None, N_HEADS, LANES), lambda b: (layer, 0, 0))],
        out_specs=_row_spec(rows, ATTN_WIDTH),
        compiler_params=_params("arbitrary"),
        name="ctx_attn",
    )(qm, kd, vd, sink)


def _lat_attn_kernel(qm_ref, kl_ref, vl_ref, kc_ref, vc_ref, bias_ref, sink_ref, o_ref,
                     kc_sc, vc_sc, *, seq, nblk):
    j = pl.program_id(1)
    nb = seq // BLOCK

    @pl.when(j == 0)
    def _():
        kc_sc[...], vc_sc[...] = _kv_slabs(kc_ref[...], vc_ref[...])

    for i in range(nblk):
        n = j * nblk + i
        start = pl.multiple_of(jnp.clip(n * BLOCK - WINDOW, 0, seq - LOCAL_SPAN), BLOCK)
        variant = jnp.where(n == 0, 0, jnp.where(n == nb - 1, 2, 1))

        def kv(h, start=start):
            kc, vc = slice(h * LANES, (h + 1) * LANES), slice(2 * h * LANES, 2 * (h + 1) * LANES)
            return (jnp.concatenate([kc_sc[:, kc], kl_ref[pl.ds(start, LOCAL_SPAN), kc]], axis=0),
                    jnp.concatenate([vc_sc[:, vc], vl_ref[pl.ds(start, LOCAL_SPAN), vc]], axis=0))

        _attend(qm_ref, o_ref, slice(i * BLOCK, (i + 1) * BLOCK), sink_ref, kv, bias_ref[variant])


def _lat_attn_call(qm, kd, vd, kctx, vctx, bias, sink, layer, batch, seq):
    past = kctx.shape[2]
    nblk = ATTN_BLOCKS_PER_STEP
    steps = seq // (BLOCK * nblk)
    rows = BLOCK * nblk
    return pl.pallas_call(
        functools.partial(_lat_attn_kernel, seq=seq, nblk=nblk),
        out_shape=jax.ShapeDtypeStruct((batch * seq, ATTN_WIDTH), BF16),
        grid=(batch, steps),
        in_specs=[pl.BlockSpec((rows, QM_WIDTH), lambda b, j: (b * steps + j, 0)),
                  pl.BlockSpec((seq, KD_WIDTH), lambda b, j: (b, 0)),
                  pl.BlockSpec((seq, VD_WIDTH), lambda b, j: (b, 0)),
                  pl.BlockSpec((None, None, past, KV_WIDTH), lambda b, j: (b, layer, 0, 0)),
                  pl.BlockSpec((None, None, past, KV_WIDTH), lambda b, j: (b, layer, 0, 0)),
                  pl.BlockSpec(bias.shape, lambda b, j: (0, 0, 0)),
                  pl.BlockSpec((None, N_HEADS, LANES), lambda b, j: (layer, 0, 0))],
        out_specs=pl.BlockSpec((rows, ATTN_WIDTH), lambda b, j: (b * steps + j, 0)),
        scratch_shapes=[pltpu.VMEM((past, KD_WIDTH), BF16), pltpu.VMEM((past, VD_WIDTH), BF16)],
        compiler_params=_params("arbitrary", "arbitrary"),
        name="lat_attn",
    )(qm, kd, vd, kctx, vctx, bias, sink)


def _fnet_kernel(x_ref, wc_ref, dm_ref, wf_ref, o_ref, *, seq):
    for i in range(x_ref.shape[0] // seq):
        rows = slice(i * seq, (i + 1) * seq)
        y = _dot(x_ref[rows, :], wc_ref[...])
        yst = jnp.concatenate([y[:, :FNET_WIDTH], y[:, FNET_WIDTH:]], axis=0).astype(BF16)
        z = _dot(dm_ref[...], yst) * ((seq * FNET_HEAD_DIM) ** -0.5)
        o_ref[rows, :] = _dot(z.astype(BF16), wf_ref[...]).astype(o_ref.dtype)


def _fnet_call(x, wc, dm, wf, layer, batch, seq):
    rows = max(seq, TOKEN_TILE)
    return pl.pallas_call(
        functools.partial(_fnet_kernel, seq=seq),
        out_shape=jax.ShapeDtypeStruct((batch * seq, FNET_WIDTH), BF16),
        grid=(batch * seq // rows,),
        in_specs=[_row_spec(rows, FNET_WIDTH),
                  _const_spec((FNET_WIDTH, 2 * FNET_WIDTH)),
                  _const_spec((seq, 2 * seq)),
                  _const_spec((FNET_WIDTH, FNET_WIDTH), (layer,))],
        out_specs=_row_spec(rows, FNET_WIDTH),
        compiler_params=_params("arbitrary"),
        name="fourier_mix",
    )(x, wc, dm, wf)


def _s5_kernel(xf_ref, xb_ref, xfn_ref, xbn_ref, h0_ref, bm_ref, cm_ref, lam_ref,
               yf_ref, yb_ref, fin_ref, xs, buf0, buf1, sb, ys, carry, *, steps, nchunks):
    first = (pl.program_id(0) == 0) & (pl.program_id(1) == 0)
    c = pl.program_id(1)
    nslab = SSM_WIDTH // LANES
    w = SCAN_LANE_CHUNK

    def relayout_in(x_refs, base):
        xin = []
        for d, x_ref in enumerate(x_refs):
            for b in range(SUBLANES):
                for sl in range(nslab):
                    c0 = b * SSM_WIDTH + sl * LANES
                    xs[base + d * nslab + sl, pl.ds(b, steps, stride=SUBLANES), :] = x_ref[:, c0:c0 + LANES]
            xin.append(jnp.concatenate([xs[base + d * nslab + sl] for sl in range(nslab)],
                                       axis=1).astype(BF16))
        return xin

    def step(cur, nxt):
        xin = relayout_in((xfn_ref, xbn_ref), 2 * nslab)
        yacc = [None, None]
        for lc in range(SSM_LANES // w):
            re = slice(lc * w, (lc + 1) * w)
            im = slice(SSM_LANES + lc * w, SSM_LANES + (lc + 1) * w)
            for d in range(2):
                nxt[d, :, re] = _dot(xin[d], bm_ref[d, :, re])
                nxt[d, :, im] = _dot(xin[d], bm_ref[d, :, im])
            lam = [(lam_ref[d, :, re], lam_ref[d, :, im]) for d in range(2)]
            st = [(carry[d, :, re], carry[d, :, im]) for d in range(2)]
            prev = [None, None]
            for t in range(steps):
                for d in range(2):
                    tt = t if d == 0 else steps - 1 - t
                    rows = slice(tt * SUBLANES, (tt + 1) * SUBLANES)
                    (lr, li), (sr, si) = lam[d], st[d]
                    nr = (lr * sr - li * si) + cur[d, rows, re]
                    ni = (lr * si + li * sr) + cur[d, rows, im]
                    st[d] = (nr, ni)
                    if t % 2 == 0:
                        prev[d] = (nr, ni)
                    else:
                        lo = min(tt, tt + (1 if d == 1 else -1)) * SUBLANES
                        pair = slice(lo, lo + 2 * SUBLANES)
                        first, second = (prev[d], (nr, ni)) if d == 0 else ((nr, ni), prev[d])
                        sb[d, pair, re] = jnp.concatenate([first[0], second[0]], axis=0).astype(BF16)
                        sb[d, pair, im] = jnp.concatenate([first[1], second[1]], axis=0).astype(BF16)
            for d in range(2):
                carry[d, :, re], carry[d, :, im] = st[d]
            for d in range(2):
                part = _dot(sb[d, :, re], cm_ref[d, re, :]) + _dot(sb[d, :, im], cm_ref[d, im, :])
                yacc[d] = part if yacc[d] is None else yacc[d] + part
        for d, y_ref in enumerate((yf_ref, yb_ref)):
            for sl in range(nslab):
                ys[d * nslab + sl] = yacc[d][:, sl * LANES:(sl + 1) * LANES]
            for b in range(SUBLANES):
                for sl in range(nslab):
                    c0 = b * SSM_WIDTH + sl * LANES
                    y_ref[:, c0:c0 + LANES] = ys[d * nslab + sl, pl.ds(b, steps, stride=SUBLANES), :]

    @pl.when(c == 0)
    def _():
        carry[...] = h0_ref[0]

    @pl.when(first)
    def _():
        xin = relayout_in((xf_ref, xb_ref), 0)
        for d in range(2):
            buf0[d] = _dot(xin[d], bm_ref[d])

    @pl.when(c % 2 == 0)
    def _():
        step(buf0, buf1)

    @pl.when(c % 2 == 1)
    def _():
        step(buf1, buf0)

    @pl.when(c == nchunks - 1)
    def _():
        fin_ref[0] = carry[...]


def _s5_call(x_wide, h0, pw, layer, groups, seq):
    steps = SCAN_STEPS
    rows = steps * SUBLANES
    nchunks = seq // steps
    assert nchunks % 2 == 0
    width = 2 * SSM_LANES
    nslab = SSM_WIDTH // LANES
    last = groups * nchunks - 1

    def nxt(g, c):
        s = jnp.minimum(g * nchunks + c + 1, last)
        return s // nchunks, s % nchunks

    def fwd_next(g, c):
        g2, c2 = nxt(g, c)
        return (g2, c2, 0)

    def bwd_next(g, c):
        g2, c2 = nxt(g, c)
        return (g2, nchunks - 1 - c2, 0)

    return pl.pallas_call(
        functools.partial(_s5_kernel, steps=steps, nchunks=nchunks),
        out_shape=[jax.ShapeDtypeStruct((groups, seq, WIDE), F32),
                   jax.ShapeDtypeStruct((groups, seq, WIDE), F32),
                   jax.ShapeDtypeStruct((groups, 2, SUBLANES, width), F32)],
        grid=(groups, nchunks),
        in_specs=[pl.BlockSpec((None, steps, WIDE), lambda g, c: (0, 0, 0)),
                  pl.BlockSpec((None, steps, WIDE), lambda g, c: (0, nchunks - 1, 0)),
                  pl.BlockSpec((None, steps, WIDE), fwd_next),
                  pl.BlockSpec((None, steps, WIDE), bwd_next),
                  pl.BlockSpec((1, 2, SUBLANES, width), lambda g, c: (g, 0, 0, 0)),
                  pl.BlockSpec((None, 2, SSM_WIDTH, width), lambda g, c: (layer, 0, 0, 0)),
                  pl.BlockSpec((None, 2, width, SSM_WIDTH), lambda g, c: (layer, 0, 0, 0)),
                  pl.BlockSpec((None, 2, SUBLANES, width), lambda g, c: (layer, 0, 0, 0))],
        out_specs=[pl.BlockSpec((None, steps, WIDE), lambda g, c: (g, c, 0)),
                   pl.BlockSpec((None, steps, WIDE), lambda g, c: (g, nchunks - 1 - c, 0)),
                   pl.BlockSpec((1, 2, SUBLANES, width), lambda g, c: (g, 0, 0, 0))],
        scratch_shapes=[pltpu.VMEM((4 * nslab, rows, LANES), F32),
                        pltpu.VMEM((2, rows, width), F32),
                        pltpu.VMEM((2, rows, width), F32),
                        pltpu.VMEM((2, rows, width), BF16),
                        pltpu.VMEM((2 * nslab, rows, LANES), F32),
                        pltpu.VMEM((2, SUBLANES, width), F32)],
        compiler_params=_params("arbitrary", "arbitrary"),
        name="s5_scan",
    )(x_wide, x_wide, x_wide, x_wide, h0, pw['bm'], pw['cm'], pw['lam'])


def _out_ffn_kernel(x_ref, mod_ref, a_ref, f_ref, us_ref, yf_ref, yb_ref, d_ref, wg_ref, wo_ref,
                    g_ref, w1_ref, w3_ref, w2_ref, o_ref, *, seq):
    m = mod_ref[0]
    us, yf, yb = (_wide_load(r, seq) for r in (us_ref, yf_ref, yb_ref))
    ys = jax.nn.gelu((d_ref[...] * us + yf) + yb)
    gl = _dot(ys.astype(BF16), wg_ref[...])
    so = gl[:, :SSM_WIDTH] * jax.nn.sigmoid(gl[:, SSM_WIDTH:])
    mixed = jnp.concatenate([a_ref[...], f_ref[...], so.astype(BF16)], axis=1)
    x2 = x_ref[...] + m[5:6] * _dot(mixed, wo_ref[...])
    o_ref[...] = _half_ffn(x2, m, 6, g_ref[...], w1_ref, w3_ref, w2_ref)


def _out_ffn_call(x, mod, a, f, us, yf, yb, pw, layer, seq, rows_per_group):
    n = x.shape[0]
    tm = _token_tile(seq)
    wide = _wide_spec(tm, seq)
    return pl.pallas_call(
        functools.partial(_out_ffn_kernel, seq=seq),
        out_shape=jax.ShapeDtypeStruct((n, D_MODEL), F32),
        grid=(n // tm,),
        in_specs=[_row_spec(tm, D_MODEL), _mod_spec(rows_per_group // tm),
                  _row_spec(tm, ATTN_WIDTH), _row_spec(tm, FNET_WIDTH), wide, wide, wide,
                  _const_spec((1, SSM_WIDTH), (layer,)),
                  _const_spec((SSM_WIDTH, 2 * SSM_WIDTH), (layer,)),
                  _const_spec((MIX_WIDTH, D_MODEL), (layer,)),
                  _const_spec((1, D_MODEL), (layer, 2)),
                  _const_spec((D_MODEL, D_FF), (layer, 1)),
                  _const_spec((D_MODEL, D_FF), (layer, 1)),
                  _const_spec((D_FF, D_MODEL), (layer, 1))],
        out_specs=_row_spec(tm, D_MODEL),
        compiler_params=_params("arbitrary"),
        name="out_proj_ffn",
    )(x, mod, a, f, us, yf, yb, pw['d'], pw['w_glu'], pw['w_out'], pw['g'], pw['w1'], pw['w3'], pw['w2'])


def _rope_tables(seq):
    pos = np.arange(seq)
    row, col = pos // GRID_W, pos % GRID_W
    quarter = HEAD_DIM // 4
    inv_freq = 1.0 / (ROPE_BASE ** (np.arange(quarter, dtype=np.float64) * 2.0 / (HEAD_DIM // 2)))
    lane = np.arange(LANES)
    in_head = lane % HEAD_DIM
    p = np.where((in_head < HEAD_DIM // 2)[None, :], row[:, None], col[:, None]).astype(np.float64)
    ang = p * inv_freq[lane % quarter][None, :]
    first = ((lane % (HEAD_DIM // 2)) < quarter)[None, :]
    cos = np.cos(ang)
    sa = np.where(first, -np.sin(ang), 0.0)
    sb = np.where(first, 0.0, np.sin(ang))
    return tuple(jnp.asarray(t, dtype=F32) for t in (cos, sa, sb))


def _dft_tables(seq):
    kl = np.outer(np.arange(seq), np.arange(seq)) % seq
    ang = 2.0 * np.pi * kl / seq
    dm = np.concatenate([np.cos(ang), -np.sin(ang)], axis=1)
    mc = np.outer(np.arange(FNET_HEAD_DIM), np.arange(FNET_HEAD_DIM)) % FNET_HEAD_DIM
    a64 = 2.0 * np.pi * mc / FNET_HEAD_DIM
    eye = np.eye(FNET_HEADS)
    wc = np.concatenate([np.kron(eye, np.cos(a64)), np.kron(eye, np.sin(a64))], axis=1)
    return jnp.asarray(dm, dtype=F32).astype(BF16), jnp.asarray(wc, dtype=F32).astype(BF16)


def _head_mean_matrix():
    return jnp.asarray(np.kron(np.eye(N_HEADS), np.full((HEAD_DIM, HEAD_DIM), 1.0 / HEAD_DIM)), dtype=BF16)


def _window_bias(past):
    i = np.arange(GQA_GROUP * BLOCK)[:, None] % BLOCK
    j = np.arange(LOCAL_SPAN)[None, :]
    out = np.zeros((3, GQA_GROUP * BLOCK, past + LOCAL_SPAN), np.float32)
    for v, off in enumerate((0, WINDOW, 2 * WINDOW)):
        out[v, :, past:] = np.where(np.abs(j - off - i) <= WINDOW, 0.0, NEG_INF)
    return jnp.asarray(out)


def _s5_matrices(lam_re, lam_im, b_re, b_im, c_re, c_im, log_step):
    step = jnp.exp(log_step)[..., None]
    mag = jnp.exp(lam_re * step)
    lr = mag * jnp.cos(lam_im * step)
    li = mag * jnp.sin(lam_im * step)
    den = lam_re * lam_re + lam_im * lam_im
    cr = ((lr - 1.0) * lam_re + li * lam_im) / den
    ci = (li * lam_re - (lr - 1.0) * lam_im) / den
    bbr = cr[..., None] * b_re - ci[..., None] * b_im
    bbi = cr[..., None] * b_im + ci[..., None] * b_re
    same_group = jnp.asarray(np.arange(SSM_WIDTH)[:, None] // SSM_GROUP
                             == np.arange(SSM_LANES)[None, :] // SSM_STATE, dtype=F32)

    def in_mat(b):
        rows = b.transpose(0, 1, 4, 2, 3).reshape(DEPTH, 2, SSM_GROUP, SSM_LANES)
        return jnp.tile(rows, (1, 1, SSM_GROUPS, 1)) * same_group

    def out_mat(cw):
        rows = cw.transpose(0, 1, 4, 2, 3).reshape(DEPTH, 2, SSM_STATE, SSM_WIDTH)
        return jnp.tile(rows, (1, 1, SSM_GROUPS, 1)) * same_group.T

    bm = jnp.concatenate([in_mat(bbr), in_mat(bbi)], axis=-1).astype(BF16)
    cm = jnp.concatenate([out_mat(c_re), -out_mat(c_im)], axis=-2).astype(BF16)
    lrow = jnp.concatenate([lr.reshape(DEPTH, 2, SSM_LANES), li.reshape(DEPTH, 2, SSM_LANES)], axis=-1)
    lam = jnp.broadcast_to(lrow[:, :, None, :], (DEPTH, 2, SUBLANES, 2 * SSM_LANES))
    return bm, cm, lam


def _stream_layer(x, mod, pw, layer, batch, seq, rows_per_group, tables, ctx_kv, h0):
    latent = ctx_kv is not None
    groups = batch // SUBLANES
    x1, qm, k, v, kd, vd, f, us = _ffn_in_call(x, mod, pw, layer, tables, latent, batch, seq, rows_per_group)
    if latent:
        a = _lat_attn_call(qm, kd, vd, ctx_kv[0], ctx_kv[1], tables['bias'], pw['sink'], layer, batch, seq)
    else:
        a = _ctx_attn_call(qm, kd, vd, pw['sink'], layer, batch, seq)
    dm, wc = tables['dft'][seq]
    fz = _fnet_call(f, wc, dm, pw['w_fnet'], layer, batch, seq)
    yf, yb, fin = _s5_call(us, h0, pw, layer, groups, seq)
    x = _out_ffn_call(x1, mod, a, fz, us, yf, yb, pw, layer, seq, rows_per_group)
    return x, (k, v, fin)


def kernel(x_prompt, x_sample, cache_k, cache_v, state_ssm_re, state_ssm_im, c, c_ctx, w_mod, b_mod, norm_g, ffn_w1, ffn_w3, ffn_w2, w_in, w_out, q_norm_g, k_norm_g, attn_sink, w_fnet, ssm_lambda_re, ssm_lambda_im, ssm_b_re, ssm_b_im, ssm_c_re, ssm_c_im, ssm_d, ssm_log_step, ssm_w_glu):
    batch, seq, _ = x_prompt.shape
    dec_batch, dec_seq, _ = x_sample.shape
    past = cache_k.shape[2]

    tables = {'e': _head_mean_matrix(), 'rope': _rope_tables(dec_seq), 'bias': _window_bias(past),
              'dft': {s: _dft_tables(s) for s in {seq, dec_seq}}}

    bm, cm, lam = _s5_matrices(ssm_lambda_re, ssm_lambda_im, ssm_b_re, ssm_b_im, ssm_c_re, ssm_c_im,
                               ssm_log_step)
    pw = {'g': norm_g.reshape(DEPTH, 3, 1, D_MODEL),
          'w1': ffn_w1.astype(BF16), 'w3': ffn_w3.astype(BF16), 'w2': ffn_w2.astype(BF16),
          'w_in': w_in.astype(BF16), 'w_out': w_out.astype(BF16),
          'qg': jnp.tile(q_norm_g, (1, N_HEADS)).reshape(DEPTH, 1, ATTN_WIDTH),
          'kg': jnp.tile(k_norm_g, (1, N_KV_HEADS)).reshape(DEPTH, 1, KV_WIDTH),
          'sink': jnp.broadcast_to(attn_sink[:, :, None], (DEPTH, N_HEADS, LANES)),
          'w_fnet': w_fnet.astype(BF16), 'bm': bm, 'cm': cm, 'lam': lam,
          'd': ssm_d.reshape(DEPTH, 1, SSM_WIDTH), 'w_glu': ssm_w_glu.astype(BF16)}

    cond = jnp.zeros((COND_ROWS, D_MODEL), F32).at[0].set(c_ctx).at[1:1 + dec_batch].set(c)
    mods = _mod_call(cond, w_mod, b_mod).reshape(DEPTH, COND_ROWS, N_MOD, D_MODEL)

    ctx_kv = (cache_k.reshape(dec_batch, DEPTH, past, KV_WIDTH), cache_v.reshape(dec_batch, DEPTH, past, KV_WIDTH))
    h0_lat = jnp.concatenate([state_ssm_re.reshape(dec_batch, DEPTH, 2, SSM_LANES),
                              state_ssm_im.reshape(dec_batch, DEPTH, 2, SSM_LANES)], axis=-1)
    h0_lat = h0_lat.reshape(dec_batch // SUBLANES, SUBLANES, DEPTH, 2, 2 * SSM_LANES).transpose(2, 0, 3, 1, 4)
    h0_ctx = jnp.zeros((batch // SUBLANES, 2, SUBLANES, 2 * SSM_LANES), F32)

    yp = x_prompt.reshape(batch * seq, D_MODEL)
    ys = x_sample.reshape(dec_batch * dec_seq, D_MODEL)
    ks, vs, sre, sim = [], [], [], []
    for l in range(DEPTH):
        yp, (k_l, v_l, fin) = _stream_layer(yp, mods[l, 0:1], pw, l, batch, seq, batch * seq, tables,
                                            None, h0_ctx)
        ks.append(k_l.reshape(batch, seq, N_KV_HEADS, HEAD_DIM))
        vs.append(v_l.reshape(batch, seq, N_KV_HEADS, HEAD_DIM))
        fin = fin.reshape(batch // SUBLANES, 2, SUBLANES, 2, SSM_GROUPS, SSM_STATE)
        fin = fin.transpose(3, 0, 2, 1, 4, 5).reshape(2, batch, 2, SSM_GROUPS, SSM_STATE)
        sre.append(fin[0])
        sim.append(fin[1])
        ys, _ = _stream_layer(ys, mods[l, 1:1 + dec_batch], pw, l, dec_batch, dec_seq, dec_seq, tables,
                              ctx_kv, h0_lat[l])

    return (yp.reshape(batch, seq, D_MODEL), ys.reshape(dec_batch, dec_seq, D_MODEL),
            jnp.stack(ks, axis=1), jnp.stack(vs, axis=1), jnp.stack(sre, axis=1), jnp.stack(sim, axis=1))
```

```python
import functools
import math

import numpy as np
import jax
import jax.numpy as jnp
from jax import lax
from jax.experimental import pallas as pl
from jax.experimental.pallas import tpu as pltpu

F32 = jnp.float32
BF16 = jnp.bfloat16

D_MODEL = 1024
DEPTH = 2
GRID_W = 64
HEAD_DIM = 64
N_HEADS = 8
N_KV_HEADS = 2
GQA_GROUP = N_HEADS // N_KV_HEADS
ATTN_WIDTH = N_HEADS * HEAD_DIM
KV_WIDTH = N_KV_HEADS * HEAD_DIM
WINDOW = 128
BLOCK = 128
ATTN_SCALE = HEAD_DIM ** -0.5
ROPE_BASE = 10000.0
NEG_INF = -1e30
LOG2E = math.log2(math.e)
FNET_HEADS = 4
FNET_HEAD_DIM = 64
FNET_WIDTH = FNET_HEADS * FNET_HEAD_DIM
SSM_WIDTH = 256
SSM_GROUP = 16
SSM_GROUPS = SSM_WIDTH // SSM_GROUP
SSM_STATE = 64
SSM_LANES = SSM_GROUPS * SSM_STATE
MIX_WIDTH = ATTN_WIDTH + FNET_WIDTH + SSM_WIDTH
Q_END = ATTN_WIDTH
K_END = Q_END + KV_WIDTH
V_END = K_END + KV_WIDTH
F_END = V_END + FNET_WIDTH
S_END = F_END + SSM_WIDTH
IN_WIDTH = S_END
D_FF = 2816
N_MOD = 9
NORM_EPS = 1e-6

LANES = 128
SUBLANES = 8
VMEM_LIMIT_BYTES = 60 * 1024 * 1024
TOKEN_TILE = 1024
FF_CHUNKS = 11
FF_CHUNK = D_FF // FF_CHUNKS
ATTN_BLOCKS_PER_STEP = 4
SCAN_STEPS = 64
SCAN_LANE_CHUNK = 256
COND_ROWS = 16
LOCAL_SPAN = BLOCK + 2 * WINDOW
QM_WIDTH = N_HEADS * LANES
KD_WIDTH = N_KV_HEADS * LANES
VD_WIDTH = N_KV_HEADS * 2 * LANES
WIDE = SUBLANES * SSM_WIDTH


def _params(*sem):
    return pltpu.CompilerParams(dimension_semantics=sem, vmem_limit_bytes=VMEM_LIMIT_BYTES)


def _dot(a, b):
    return jnp.dot(a, b, preferred_element_type=F32)


def _norm_mod(x, g, shift, scale):
    y = x * lax.rsqrt(jnp.mean(x * x, axis=-1, keepdims=True) + NORM_EPS)
    return (y * g) * (1.0 + scale) + shift


def _const_spec(shape, lead=()):
    nd = len(shape)
    idx = tuple(lead) + (0,) * nd
    return pl.BlockSpec((None,) * len(lead) + tuple(shape), lambda *_: idx,
                        pipeline_mode=pl.Buffered(1))


def _row_spec(tm, width):
    return pl.BlockSpec((tm, width), lambda i: (i, 0))


def _mod_spec(tiles_per_group):
    return pl.BlockSpec((1, N_MOD, D_MODEL), lambda i: (i // tiles_per_group, 0, 0))


def _wide_spec(tm, seq):
    if tm <= seq:
        tiles_per_seq = seq // tm

        def index(i):
            b = i // tiles_per_seq
            return (b // SUBLANES, i % tiles_per_seq, b % SUBLANES)
        return pl.BlockSpec((None, tm, SSM_WIDTH), index)
    nseq = tm // seq

    def index(i):
        b = i * nseq
        return (b // SUBLANES, 0, (b % SUBLANES) // nseq)
    return pl.BlockSpec((None, seq, nseq * SSM_WIDTH), index)


def _wide_store(ref, x, seq):
    for j in range(x.shape[0] // seq if x.shape[0] > seq else 1):
        rows = slice(j * seq, (j + 1) * seq) if x.shape[0] > seq else slice(None)
        ref[:, j * SSM_WIDTH:(j + 1) * SSM_WIDTH] = x[rows]


def _wide_load(ref, seq):
    ncol = ref.shape[1] // SSM_WIDTH
    if ncol == 1:
        return ref[...]
    return jnp.concatenate([ref[:, j * SSM_WIDTH:(j + 1) * SSM_WIDTH] for j in range(ncol)], axis=0)


def _mod_kernel(c_ref, w_ref, b_ref, o_ref):
    c = c_ref[...]
    a = (c * jax.nn.sigmoid(c)).astype(BF16)
    o_ref[0] = _dot(a, w_ref[0].astype(BF16)) + b_ref[0]


def _mod_call(cond, w_mod, b_mod):
    tn = N_MOD * D_MODEL // 4
    return pl.pallas_call(
        _mod_kernel,
        out_shape=jax.ShapeDtypeStruct((DEPTH, COND_ROWS, N_MOD * D_MODEL), F32),
        grid=(DEPTH, N_MOD * D_MODEL // tn),
        in_specs=[pl.BlockSpec((COND_ROWS, D_MODEL), lambda l, j: (0, 0)),
                  pl.BlockSpec((1, D_MODEL, tn), lambda l, j: (l, 0, j)),
                  pl.BlockSpec((1, 1, tn), lambda l, j: (l, 0, j))],
        out_specs=pl.BlockSpec((1, COND_ROWS, tn), lambda l, j: (l, 0, j)),
        compiler_params=_params("arbitrary", "arbitrary"),
        name="cond_mod",
    )(cond, w_mod, b_mod.reshape(DEPTH, 1, N_MOD * D_MODEL))


def _half_ffn(x, m, k0, g, w1_ref, w3_ref, w2_ref):
    h = _norm_mod(x, g, m[k0:k0 + 1], m[k0 + 1:k0 + 2]).astype(BF16)
    y = None
    for c in range(FF_CHUNKS):
        cols = slice(c * FF_CHUNK, (c + 1) * FF_CHUNK)
        a = _dot(h, w1_ref[:, cols])
        b = _dot(h, w3_ref[:, cols])
        t = ((a * jax.nn.sigmoid(a)) * b).astype(BF16)
        yc = _dot(t, w2_ref[cols, :])
        y = yc if y is None else y + yc
    return x + (0.5 * m[k0 + 2:k0 + 3]) * y


def _head_norm(z, e, g):
    ms = _dot((z * z).astype(BF16), e)
    return (z * lax.rsqrt(ms + NORM_EPS)) * g


def _low_half(shape):
    return lax.broadcasted_iota(jnp.int32, shape, 1) < HEAD_DIM


def _kv_slabs(k, v):
    low = _low_half(k.shape)
    kr = pltpu.roll(k, HEAD_DIM, 1)
    vr = pltpu.roll(v, HEAD_DIM, 1)
    kd = jnp.concatenate([jnp.where(low, k, kr), jnp.where(low, kr, k)], axis=1)
    vd = jnp.concatenate([jnp.where(low, v, 1.0), jnp.where(low, 1.0, vr),
                          jnp.where(low, vr, 1.0), jnp.where(low, 1.0, v)], axis=1)
    return kd.astype(BF16), vd.astype(BF16)


def _ffn_in_kernel(*refs, rope, seq):
    if rope:
        (x_ref, mod_ref, g0_ref, g1_ref, w1_ref, w3_ref, w2_ref, w_ref, qg_ref, kg_ref, e_ref,
         cos_ref, sa_ref, sb_ref, x1_ref, qm_ref, k_ref, v_ref, kd_ref, vd_ref, f_ref, s_ref) = refs
    else:
        (x_ref, mod_ref, g0_ref, g1_ref, w1_ref, w3_ref, w2_ref, w_ref, qg_ref, kg_ref, e_ref,
         x1_ref, qm_ref, k_ref, v_ref, kd_ref, vd_ref, f_ref, s_ref) = refs
    m = mod_ref[0]
    x1 = _half_ffn(x_ref[...], m, 0, g0_ref[...], w1_ref, w3_ref, w2_ref)
    x1_ref[...] = x1
    h = _norm_mod(x1, g1_ref[...], m[3:4], m[4:5]).astype(BF16)
    u = _dot(h, w_ref[...])
    q = _head_norm(u[:, :Q_END], e_ref[...], qg_ref[...])
    k = _head_norm(u[:, Q_END:K_END], e_ref[0:KV_WIDTH, 0:KV_WIDTH], kg_ref[...])
    v = u[:, K_END:V_END]
    if rope:
        cos, sa, sb = cos_ref[...], sa_ref[...], sb_ref[...]

        def rot(z):
            return z * cos + pltpu.roll(z, LANES - 16, 1) * sa + pltpu.roll(z, 16, 1) * sb
    else:
        def rot(z):
            return z

    low = _low_half((q.shape[0], LANES))
    for i in range(ATTN_WIDTH // LANES):
        qs = rot(q[:, i * LANES:(i + 1) * LANES]) * (ATTN_SCALE * LOG2E)
        qm_ref[:, (2 * i) * LANES:(2 * i + 1) * LANES] = jnp.where(low, qs, 0.0).astype(BF16)
        qm_ref[:, (2 * i + 1) * LANES:(2 * i + 2) * LANES] = jnp.where(low, 0.0, qs).astype(BF16)
    k = rot(k)
    k_ref[...] = k
    v_ref[...] = v
    kd_ref[...], vd_ref[...] = _kv_slabs(k, v)
    f_ref[...] = u[:, V_END:F_END].astype(BF16)
    _wide_store(s_ref, u[:, F_END:S_END], seq)


def _token_tile(seq):
    return TOKEN_TILE if TOKEN_TILE <= seq else min(TOKEN_TILE, SUBLANES * seq // 2)


def _ffn_in_call(x, mod, pw, layer, tables, rope, batch, seq, rows_per_group):
    n = x.shape[0]
    tm = _token_tile(seq)
    tiles_per_seq = max(seq // tm, 1)
    in_specs = [_row_spec(tm, D_MODEL), _mod_spec(rows_per_group // tm),
                _const_spec((1, D_MODEL), (layer, 0)),
                _const_spec((1, D_MODEL), (layer, 1)),
                _const_spec((D_MODEL, D_FF), (layer, 0)),
                _const_spec((D_MODEL, D_FF), (layer, 0)),
                _const_spec((D_FF, D_MODEL), (layer, 0)),
                _const_spec((D_MODEL, IN_WIDTH), (layer,)),
                _const_spec((1, ATTN_WIDTH), (layer,)),
                _const_spec((1, KV_WIDTH), (layer,)),
                _const_spec((ATTN_WIDTH, ATTN_WIDTH))]
    args = [x, mod, pw['g'], pw['g'], pw['w1'], pw['w3'], pw['w2'], pw['w_in'], pw['qg'], pw['kg'],
            tables['e']]
    if rope:
        in_specs += [pl.BlockSpec((tm, LANES), lambda i: (i % tiles_per_seq, 0))] * 3
        args += list(tables['rope'])
    widths = (D_MODEL, QM_WIDTH, KV_WIDTH, KV_WIDTH, KD_WIDTH, VD_WIDTH, FNET_WIDTH)
    dtypes = (F32, BF16, F32, F32, BF16, BF16, BF16)
    out_shape = [jax.ShapeDtypeStruct((n, w), d) for w, d in zip(widths, dtypes)]
    out_shape.append(jax.ShapeDtypeStruct((batch // SUBLANES, seq, WIDE), F32))
    out_specs = [_row_spec(tm, w) for w in widths] + [_wide_spec(tm, seq)]
    return pl.pallas_call(
        functools.partial(_ffn_in_kernel, rope=rope, seq=seq),
        out_shape=out_shape,
        grid=(n // tm,),
        in_specs=in_specs,
        out_specs=out_specs,
        compiler_params=_params("arbitrary"),
        name="ffn_in_proj",
    )(*args)


def _attend(qm_ref, o_ref, rows, sink_ref, kv, bias):
    nq = rows.stop - rows.start
    low = _low_half((nq, LANES))
    for h in range(N_KV_HEADS):
        j0 = GQA_GROUP * h
        qst = jnp.concatenate([qm_ref[rows, (j0 + g) * LANES:(j0 + g + 1) * LANES]
                               for g in range(GQA_GROUP)], axis=0)
        sink = jnp.concatenate([jnp.broadcast_to(sink_ref[j0 + g:j0 + g + 1, :], (nq, LANES))
                                for g in range(GQA_GROUP)], axis=0) * LOG2E
        kd, vd = kv(h)
        s = lax.dot_general(qst, kd, (((1,), (1,)), ((), ())), preferred_element_type=F32)
        if bias is not None:
            s = s + bias
        m = jnp.maximum(jnp.max(s, axis=-1, keepdims=True), sink)
        e = jnp.exp2(s - jnp.concatenate([m] * (s.shape[1] // LANES), axis=1)).astype(BF16)
        o = _dot(e, vd)
        es = jnp.exp2(sink - m)
        for t in range(GQA_GROUP // 2):
            r0, r1 = (2 * t) * nq, (2 * t + 1) * nq
            num = jnp.where(low, o[r0:r0 + nq, :LANES], o[r1:r1 + nq, LANES:])
            den = (jnp.where(low, o[r0:r0 + nq, LANES:], o[r1:r1 + nq, :LANES])
                   + jnp.where(low, es[r0:r0 + nq], es[r1:r1 + nq]))
            c0 = (j0 // 2 + t) * LANES
            o_ref[rows, c0:c0 + LANES] = (num * (1.0 / den)).astype(o_ref.dtype)


def _ctx_attn_kernel(qm_ref, kd_ref, vd_ref, sink_ref, o_ref, *, seq, nseq):
    for i in range(nseq):
        rows = slice(i * seq, (i + 1) * seq)

        def kv(h, rows=rows):
            return (kd_ref[rows, h * LANES:(h + 1) * LANES], vd_ref[rows, 2 * h * LANES:2 * (h + 1) * LANES])

        _attend(qm_ref, o_ref, rows, sink_ref, kv, None)


def _ctx_attn_call(qm, kd, vd, sink, layer, batch, seq):
    nseq = ATTN_BLOCKS_PER_STEP
    rows = nseq * seq
    return pl.pallas_call(
        functools.partial(_ctx_attn_kernel, seq=seq, nseq=nseq),
        out_shape=jax.ShapeDtypeStruct((batch * seq, ATTN_WIDTH), BF16),
        grid=(batch // nseq,),
        in_specs=[_row_spec(rows, QM_WIDTH), _row_spec(rows, KD_WIDTH), _row_spec(rows, VD_WIDTH),
                  pl.BlockSpec((None, N_HEADS, LANES), lambda b: (layer, 0, 0))],
        out_specs=_row_spec(rows, ATTN_WIDTH),
        compiler_params=_params("arbitrary"),
        name="ctx_attn",
    )(qm, kd, vd, sink)


def _lat_attn_kernel(qm_ref, kl_ref, vl_ref, kc_ref, vc_ref, bias_ref, sink_ref, o_ref,
                     kc_sc, vc_sc, *, seq, nblk):
    j = pl.program_id(1)
    nb = seq // BLOCK

    @pl.when(j == 0)
    def _():
        kc_sc[...], vc_sc[...] = _kv_slabs(kc_ref[...], vc_ref[...])

    for i in range(nblk):
        n = j * nblk + i
        start = pl.multiple_of(jnp.clip(n * BLOCK - WINDOW, 0, seq - LOCAL_SPAN), BLOCK)
        variant = jnp.where(n == 0, 0, jnp.where(n == nb - 1, 2, 1))

        def kv(h, start=start):
            kc, vc = slice(h * LANES, (h + 1) * LANES), slice(2 * h * LANES, 2 * (h + 1) * LANES)
            return (jnp.concatenate([kc_sc[:, kc], kl_ref[pl.ds(start, LOCAL_SPAN), kc]], axis=0),
                    jnp.concatenate([vc_sc[:, vc], vl_ref[pl.ds(start, LOCAL_SPAN), vc]], axis=0))

        _attend(qm_ref, o_ref, slice(i * BLOCK, (i + 1) * BLOCK), sink_ref, kv, bias_ref[variant])


def _lat_attn_call(qm, kd, vd, kctx, vctx, bias, sink, layer, batch, seq):
    past = kctx.shape[2]
    nblk = 2 * ATTN_BLOCKS_PER_STEP
    steps = seq // (BLOCK * nblk)
    rows = BLOCK * nblk
    return pl.pallas_call(
        functools.partial(_lat_attn_kernel, seq=seq, nblk=nblk),
        out_shape=jax.ShapeDtypeStruct((batch * seq, ATTN_WIDTH), BF16),
        grid=(batch, steps),
        in_specs=[pl.BlockSpec((rows, QM_WIDTH), lambda b, j: (b * steps + j, 0)),
                  pl.BlockSpec((seq, KD_WIDTH), lambda b, j: (b, 0)),
                  pl.BlockSpec((seq, VD_WIDTH), lambda b, j: (b, 0)),
                  pl.BlockSpec((None, None, past, KV_WIDTH), lambda b, j: (b, layer, 0, 0)),
                  pl.BlockSpec((None, None, past, KV_WIDTH), lambda b, j: (b, layer, 0, 0)),
                  pl.BlockSpec(bias.shape, lambda b, j: (0, 0, 0)),
                  pl.BlockSpec((None, N_HEADS, LANES), lambda b, j: (layer, 0, 0))],
        out_specs=pl.BlockSpec((rows, ATTN_WIDTH), lambda b, j: (b * steps + j, 0)),
        scratch_shapes=[pltpu.VMEM((past, KD_WIDTH), BF16), pltpu.VMEM((past, VD_WIDTH), BF16)],
        compiler_params=_params("arbitrary", "arbitrary"),
        name="lat_attn",
    )(qm, kd, vd, kctx, vctx, bias, sink)


def _fnet_kernel(x_ref, wc_ref, dm_ref, wf_ref, o_ref, *, seq):
    for i in range(x_ref.shape[0] // seq):
        rows = slice(i * seq, (i + 1) * seq)
        y = _dot(x_ref[rows, :], wc_ref[...])
        yst = jnp.concatenate([y[:, :FNET_WIDTH], y[:, FNET_WIDTH:]], axis=0).astype(BF16)
        z = _dot(dm_ref[...], yst) * ((seq * FNET_HEAD_DIM) ** -0.5)
        o_ref[rows, :] = _dot(z.astype(BF16), wf_ref[...]).astype(o_ref.dtype)


def _fnet_call(x, wc, dm, wf, layer, batch, seq):
    rows = max(seq, TOKEN_TILE)
    return pl.pallas_call(
        functools.partial(_fnet_kernel, seq=seq),
        out_shape=jax.ShapeDtypeStruct((batch * seq, FNET_WIDTH), BF16),
        grid=(batch * seq // rows,),
        in_specs=[_row_spec(rows, FNET_WIDTH),
                  _const_spec((FNET_WIDTH, 2 * FNET_WIDTH)),
                  _const_spec((seq, 2 * seq)),
                  _const_spec((FNET_WIDTH, FNET_WIDTH), (layer,))],
        out_specs=_row_spec(rows, FNET_WIDTH),
        compiler_params=_params("arbitrary"),
        name="fourier_mix",
    )(x, wc, dm, wf)


def _s5_kernel(xf_ref, xb_ref, xfn_ref, xbn_ref, h0_ref, bm_ref, cm_ref, lam_ref,
               yf_ref, yb_ref, fin_ref, xs, buf0, buf1, sb, ys, carry, *, steps, nchunks):
    first = (pl.program_id(0) == 0) & (pl.program_id(1) == 0)
    c = pl.program_id(1)
    nslab = SSM_WIDTH // LANES
    w = SCAN_LANE_CHUNK

    def relayout_in(x_refs, base):
        xin = []
        for d, x_ref in enumerate(x_refs):
            for b in range(SUBLANES):
                for sl in range(nslab):
                    c0 = b * SSM_WIDTH + sl * LANES
                    xs[base + d * nslab + sl, pl.ds(b, steps, stride=SUBLANES), :] = x_ref[:, c0:c0 + LANES]
            xin.append(jnp.concatenate([xs[base + d * nslab + sl] for sl in range(nslab)],
                                       axis=1).astype(BF16))
        return xin

    def step(cur, nxt):
        xin = relayout_in((xfn_ref, xbn_ref), 2 * nslab)
        yacc = [None, None]
        for lc in range(SSM_LANES // w):
            re = slice(lc * w, (lc + 1) * w)
            im = slice(SSM_LANES + lc * w, SSM_LANES + (lc + 1) * w)
            for d in range(2):
                nxt[d, :, re] = _dot(xin[d], bm_ref[d, :, re])
                nxt[d, :, im] = _dot(xin[d], bm_ref[d, :, im])
            lam = [(lam_ref[d, :, re], lam_ref[d, :, im]) for d in range(2)]
            st = [(carry[d, :, re], carry[d, :, im]) for d in range(2)]
            prev = [None, None]
            for t in range(steps):
                for d in range(2):
                    tt = t if d == 0 else steps - 1 - t
                    rows = slice(tt * SUBLANES, (tt + 1) * SUBLANES)
                    (lr, li), (sr, si) = lam[d], st[d]
                    nr = (lr * sr - li * si) + cur[d, rows, re]
                    ni = (lr * si + li * sr) + cur[d, rows, im]
                    st[d] = (nr, ni)
                    if t % 2 == 0:
                        prev[d] = (nr, ni)
                    else:
                        lo = min(tt, tt + (1 if d == 1 else -1)) * SUBLANES
                        pair = slice(lo, lo + 2 * SUBLANES)
                        first, second = (prev[d], (nr, ni)) if d == 0 else ((nr, ni), prev[d])
                        sb[d, pair, re] = jnp.concatenate([first[0], second[0]], axis=0).astype(BF16)
                        sb[d, pair, im] = jnp.concatenate([first[1], second[1]], axis=0).astype(BF16)
            for d in range(2):
                carry[d, :, re], carry[d, :, im] = st[d]
            for d in range(2):
                part = _dot(sb[d, :, re], cm_ref[d, re, :]) + _dot(sb[d, :, im], cm_ref[d, im, :])
                yacc[d] = part if yacc[d] is None else yacc[d] + part
        for d, y_ref in enumerate((yf_ref, yb_ref)):
            for sl in range(nslab):
                ys[d * nslab + sl] = yacc[d][:, sl * LANES:(sl + 1) * LANES]
            for b in range(SUBLANES):
                for sl in range(nslab):
                    c0 = b * SSM_WIDTH + sl * LANES
                    y_ref[:, c0:c0 + LANES] = ys[d * nslab + sl, pl.ds(b, steps, stride=SUBLANES), :]

    @pl.when(c == 0)
    def _():
        carry[...] = h0_ref[0]

    @pl.when(first)
    def _():
        xin = relayout_in((xf_ref, xb_ref), 0)
        for d in range(2):
            buf0[d] = _dot(xin[d], bm_ref[d])

    @pl.when(c % 2 == 0)
    def _():
        step(buf0, buf1)

    @pl.when(c % 2 == 1)
    def _():
        step(buf1, buf0)

    @pl.when(c == nchunks - 1)
    def _():
        fin_ref[0] = carry[...]


def _s5_call(x_wide, h0, pw, layer, groups, seq):
    steps = SCAN_STEPS
    rows = steps * SUBLANES
    nchunks = seq // steps
    assert nchunks % 2 == 0
    width = 2 * SSM_LANES
    nslab = SSM_WIDTH // LANES
    last = groups * nchunks - 1

    def nxt(g, c):
        s = jnp.minimum(g * nchunks + c + 1, last)
        return s // nchunks, s % nchunks

    def fwd_next(g, c):
        g2, c2 = nxt(g, c)
        return (g2, c2, 0)

    def bwd_next(g, c):
        g2, c2 = nxt(g, c)
        return (g2, nchunks - 1 - c2, 0)

    return pl.pallas_call(
        functools.partial(_s5_kernel, steps=steps, nchunks=nchunks),
        out_shape=[jax.ShapeDtypeStruct((groups, seq, WIDE), F32),
                   jax.ShapeDtypeStruct((groups, seq, WIDE), F32),
                   jax.ShapeDtypeStruct((groups, 2, SUBLANES, width), F32)],
        grid=(groups, nchunks),
        in_specs=[pl.BlockSpec((None, steps, WIDE), lambda g, c: (0, 0, 0)),
                  pl.BlockSpec((None, steps, WIDE), lambda g, c: (0, nchunks - 1, 0)),
                  pl.BlockSpec((None, steps, WIDE), fwd_next),
                  pl.BlockSpec((None, steps, WIDE), bwd_next),
                  pl.BlockSpec((1, 2, SUBLANES, width), lambda g, c: (g, 0, 0, 0)),
                  pl.BlockSpec((None, 2, SSM_WIDTH, width), lambda g, c: (layer, 0, 0, 0)),
                  pl.BlockSpec((None, 2, width, SSM_WIDTH), lambda g, c: (layer, 0, 0, 0)),
                  pl.BlockSpec((None, 2, SUBLANES, width), lambda g, c: (layer, 0, 0, 0))],
        out_specs=[pl.BlockSpec((None, steps, WIDE), lambda g, c: (g, c, 0)),
                   pl.BlockSpec((None, steps, WIDE), lambda g, c: (g, nchunks - 1 - c, 0)),
                   pl.BlockSpec((1, 2, SUBLANES, width), lambda g, c: (g, 0, 0, 0))],
        scratch_shapes=[pltpu.VMEM((4 * nslab, rows, LANES), F32),
                        pltpu.VMEM((2, rows, width), F32),
                        pltpu.VMEM((2, rows, width), F32),
                        pltpu.VMEM((2, rows, width), BF16),
                        pltpu.VMEM((2 * nslab, rows, LANES), F32),
                        pltpu.VMEM((2, SUBLANES, width), F32)],
        compiler_params=_params("arbitrary", "arbitrary"),
        name="s5_scan",
    )(x_wide, x_wide, x_wide, x_wide, h0, pw['bm'], pw['cm'], pw['lam'])


def _out_ffn_kernel(x_ref, mod_ref, a_ref, f_ref, us_ref, yf_ref, yb_ref, d_ref, wg_ref, wo_ref,
                    g_ref, w1_ref, w3_ref, w2_ref, o_ref, *, seq):
    m = mod_ref[0]
    us, yf, yb = (_wide_load(r, seq) for r in (us_ref, yf_ref, yb_ref))
    ys = jax.nn.gelu((d_ref[...] * us + yf) + yb)
    gl = _dot(ys.astype(BF16), wg_ref[...])
    so = gl[:, :SSM_WIDTH] * jax.nn.sigmoid(gl[:, SSM_WIDTH:])
    mixed = jnp.concatenate([a_ref[...], f_ref[...], so.astype(BF16)], axis=1)
    x2 = x_ref[...] + m[5:6] * _dot(mixed, wo_ref[...])
    o_ref[...] = _half_ffn(x2, m, 6, g_ref[...], w1_ref, w3_ref, w2_ref)


def _out_ffn_call(x, mod, a, f, us, yf, yb, pw, layer, seq, rows_per_group):
    n = x.shape[0]
    tm = _token_tile(seq)
    wide = _wide_spec(tm, seq)
    return pl.pallas_call(
        functools.partial(_out_ffn_kernel, seq=seq),
        out_shape=jax.ShapeDtypeStruct((n, D_MODEL), F32),
        grid=(n // tm,),
        in_specs=[_row_spec(tm, D_MODEL), _mod_spec(rows_per_group // tm),
                  _row_spec(tm, ATTN_WIDTH), _row_spec(tm, FNET_WIDTH), wide, wide, wide,
                  _const_spec((1, SSM_WIDTH), (layer,)),
                  _const_spec((SSM_WIDTH, 2 * SSM_WIDTH), (layer,)),
                  _const_spec((MIX_WIDTH, D_MODEL), (layer,)),
                  _const_spec((1, D_MODEL), (layer, 2)),
                  _const_spec((D_MODEL, D_FF), (layer, 1)),
                  _const_spec((D_MODEL, D_FF), (layer, 1)),
                  _const_spec((D_FF, D_MODEL), (layer, 1))],
        out_specs=_row_spec(tm, D_MODEL),
        compiler_params=_params("arbitrary"),
        name="out_proj_ffn",
    )(x, mod, a, f, us, yf, yb, pw['d'], pw['w_glu'], pw['w_out'], pw['g'], pw['w1'], pw['w3'], pw['w2'])


def _rope_tables(seq):
    pos = np.arange(seq)
    row, col = pos // GRID_W, pos % GRID_W
    quarter = HEAD_DIM // 4
    inv_freq = 1.0 / (ROPE_BASE ** (np.arange(quarter, dtype=np.float64) * 2.0 / (HEAD_DIM // 2)))
    lane = np.arange(LANES)
    in_head = lane % HEAD_DIM
    p = np.where((in_head < HEAD_DIM // 2)[None, :], row[:, None], col[:, None]).astype(np.float64)
    ang = p * inv_freq[lane % quarter][None, :]
    first = ((lane % (HEAD_DIM // 2)) < quarter)[None, :]
    cos = np.cos(ang)
    sa = np.where(first, -np.sin(ang), 0.0)
    sb = np.where(first, 0.0, np.sin(ang))
    return tuple(jnp.asarray(t, dtype=F32) for t in (cos, sa, sb))


def _dft_tables(seq):
    kl = np.outer(np.arange(seq), np.arange(seq)) % seq
    ang = 2.0 * np.pi * kl / seq
    dm = np.concatenate([np.cos(ang), -np.sin(ang)], axis=1)
    mc = np.outer(np.arange(FNET_HEAD_DIM), np.arange(FNET_HEAD_DIM)) % FNET_HEAD_DIM
    a64 = 2.0 * np.pi * mc / FNET_HEAD_DIM
    eye = np.eye(FNET_HEADS)
    wc = np.concatenate([np.kron(eye, np.cos(a64)), np.kron(eye, np.sin(a64))], axis=1)
    return jnp.asarray(dm, dtype=F32).astype(BF16), jnp.asarray(wc, dtype=F32).astype(BF16)


def _head_mean_matrix():
    return jnp.asarray(np.kron(np.eye(N_HEADS), np.full((HEAD_DIM, HEAD_DIM), 1.0 / HEAD_DIM)), dtype=BF16)


def _window_bias(past):
    i = np.arange(GQA_GROUP * BLOCK)[:, None] % BLOCK
    j = np.arange(LOCAL_SPAN)[None, :]
    out = np.zeros((3, GQA_GROUP * BLOCK, past + LOCAL_SPAN), np.float32)
    for v, off in enumerate((0, WINDOW, 2 * WINDOW)):
        out[v, :, past:] = np.where(np.abs(j - off - i) <= WINDOW, 0.0, NEG_INF)
    return jnp.asarray(out)


def _s5_matrices(lam_re, lam_im, b_re, b_im, c_re, c_im, log_step):
    step = jnp.exp(log_step)[..., None]
    mag = jnp.exp(lam_re * step)
    lr = mag * jnp.cos(lam_im * step)
    li = mag * jnp.sin(lam_im * step)
    den = lam_re * lam_re + lam_im * lam_im
    cr = ((lr - 1.0) * lam_re + li * lam_im) / den
    ci = (li * lam_re - (lr - 1.0) * lam_im) / den
    bbr = cr[..., None] * b_re - ci[..., None] * b_im
    bbi = cr[..., None] * b_im + ci[..., None] * b_re
    same_group = jnp.asarray(np.arange(SSM_WIDTH)[:, None] // SSM_GROUP
                             == np.arange(SSM_LANES)[None, :] // SSM_STATE, dtype=F32)

    def in_mat(b):
        rows = b.transpose(0, 1, 4, 2, 3).reshape(DEPTH, 2, SSM_GROUP, SSM_LANES)
        return jnp.tile(rows, (1, 1, SSM_GROUPS, 1)) * same_group

    def out_mat(cw):
        rows = cw.transpose(0, 1, 4, 2, 3).reshape(DEPTH, 2, SSM_STATE, SSM_WIDTH)
        return jnp.tile(rows, (1, 1, SSM_GROUPS, 1)) * same_group.T

    bm = jnp.concatenate([in_mat(bbr), in_mat(bbi)], axis=-1).astype(BF16)
    cm = jnp.concatenate([out_mat(c_re), -out_mat(c_im)], axis=-2).astype(BF16)
    lrow = jnp.concatenate([lr.reshape(DEPTH, 2, SSM_LANES), li.reshape(DEPTH, 2, SSM_LANES)], axis=-1)
    lam = jnp.broadcast_to(lrow[:, :, None, :], (DEPTH, 2, SUBLANES, 2 * SSM_LANES))
    return bm, cm, lam


def _stream_layer(x, mod, pw, layer, batch, seq, rows_per_group, tables, ctx_kv, h0):
    latent = ctx_kv is not None
    groups = batch // SUBLANES
    x1, qm, k, v, kd, vd, f, us = _ffn_in_call(x, mod, pw, layer, tables, latent, batch, seq, rows_per_group)
    if latent:
        a = _lat_attn_call(qm, kd, vd, ctx_kv[0], ctx_kv[1], tables['bias'], pw['sink'], layer, batch, seq)
    else:
        a = _ctx_attn_call(qm, kd, vd, pw['sink'], layer, batch, seq)
    dm, wc = tables['dft'][seq]
    fz = _fnet_call(f, wc, dm, pw['w_fnet'], layer, batch, seq)
    yf, yb, fin = _s5_call(us, h0, pw, layer, groups, seq)
    x = _out_ffn_call(x1, mod, a, fz, us, yf, yb, pw, layer, seq, rows_per_group)
    return x, (k, v, fin)


def kernel(x_prompt, x_sample, cache_k, cache_v, state_ssm_re, state_ssm_im, c, c_ctx, w_mod, b_mod, norm_g, ffn_w1, ffn_w3, ffn_w2, w_in, w_out, q_norm_g, k_norm_g, attn_sink, w_fnet, ssm_lambda_re, ssm_lambda_im, ssm_b_re, ssm_b_im, ssm_c_re, ssm_c_im, ssm_d, ssm_log_step, ssm_w_glu):
    batch, seq, _ = x_prompt.shape
    dec_batch, dec_seq, _ = x_sample.shape
    past = cache_k.shape[2]

    tables = {'e': _head_mean_matrix(), 'rope': _rope_tables(dec_seq), 'bias': _window_bias(past),
              'dft': {s: _dft_tables(s) for s in {seq, dec_seq}}}

    bm, cm, lam = _s5_matrices(ssm_lambda_re, ssm_lambda_im, ssm_b_re, ssm_b_im, ssm_c_re, ssm_c_im,
                               ssm_log_step)
    pw = {'g': norm_g.reshape(DEPTH, 3, 1, D_MODEL),
          'w1': ffn_w1.astype(BF16), 'w3': ffn_w3.astype(BF16), 'w2': ffn_w2.astype(BF16),
          'w_in': w_in.astype(BF16), 'w_out': w_out.astype(BF16),
          'qg': jnp.tile(q_norm_g, (1, N_HEADS)).reshape(DEPTH, 1, ATTN_WIDTH),
          'kg': jnp.tile(k_norm_g, (1, N_KV_HEADS)).reshape(DEPTH, 1, KV_WIDTH),
          'sink': jnp.broadcast_to(attn_sink[:, :, None], (DEPTH, N_HEADS, LANES)),
          'w_fnet': w_fnet.astype(BF16), 'bm': bm, 'cm': cm, 'lam': lam,
          'd': ssm_d.reshape(DEPTH, 1, SSM_WIDTH), 'w_glu': ssm_w_glu.astype(BF16)}

    cond = jnp.zeros((COND_ROWS, D_MODEL), F32).at[0].set(c_ctx).at[1:1 + dec_batch].set(c)
    mods = _mod_call(cond, w_mod, b_mod).reshape(DEPTH, COND_ROWS, N_MOD, D_MODEL)

    ctx_kv = (cache_k.reshape(dec_batch, DEPTH, past, KV_WIDTH), cache_v.reshape(dec_batch, DEPTH, past, KV_WIDTH))
    h0_lat = jnp.concatenate([state_ssm_re.reshape(dec_batch, DEPTH, 2, SSM_LANES),
                              state_ssm_im.reshape(dec_batch, DEPTH, 2, SSM_LANES)], axis=-1)
    h0_lat = h0_lat.reshape(dec_batch // SUBLANES, SUBLANES, DEPTH, 2, 2 * SSM_LANES).transpose(2, 0, 3, 1, 4)
    h0_ctx = jnp.zeros((batch // SUBLANES, 2, SUBLANES, 2 * SSM_LANES), F32)

    yp = x_prompt.reshape(batch * seq, D_MODEL)
    ys = x_sample.reshape(dec_batch * dec_seq, D_MODEL)
    ks, vs, sre, sim = [], [], [], []
    for l in range(DEPTH):
        yp, (k_l, v_l, fin) = _stream_layer(yp, mods[l, 0:1], pw, l, batch, seq, batch * seq, tables,
                                            None, h0_ctx)
        ks.append(k_l.reshape(batch, seq, N_KV_HEADS, HEAD_DIM))
        vs.append(v_l.reshape(batch, seq, N_KV_HEADS, HEAD_DIM))
        fin = fin.reshape(batch // SUBLANES, 2, SUBLANES, 2, SSM_GROUPS, SSM_STATE)
        fin = fin.transpose(3, 0, 2, 1, 4, 5).reshape(2, batch, 2, SSM_GROUPS, SSM_STATE)
        sre.append(fin[0])
        sim.append(fin[1])
        ys, _ = _stream_layer(ys, mods[l, 1:1 + dec_batch], pw, l, dec_batch, dec_seq, dec_seq, tables,
                              ctx_kv, h0_lat[l])

    return (yp.reshape(batch, seq, D_MODEL), ys.reshape(dec_batch, dec_seq, D_MODEL),
            jnp.stack(ks, axis=1), jnp.stack(vs, axis=1), jnp.stack(sre, axis=1), jnp.stack(sim, axis=1))
```

```python
import functools
import math

import numpy as np
import jax
import jax.numpy as jnp
from jax import lax
from jax.experimental import pallas as pl
from jax.experimental.pallas import tpu as pltpu

F32 = jnp.float32
BF16 = jnp.bfloat16

D_MODEL = 1024
DEPTH = 2
GRID_W = 64
HEAD_DIM = 64
N_HEADS = 8
N_KV_HEADS = 2
GQA_GROUP = N_HEADS // N_KV_HEADS
ATTN_WIDTH = N_HEADS * HEAD_DIM
KV_WIDTH = N_KV_HEADS * HEAD_DIM
WINDOW = 128
BLOCK = 128
ATTN_SCALE = HEAD_DIM ** -0.5
ROPE_BASE = 10000.0
NEG_INF = -1e30
LOG2E = math.log2(math.e)
FNET_HEADS = 4
FNET_HEAD_DIM = 64
FNET_WIDTH = FNET_HEADS * FNET_HEAD_DIM
SSM_WIDTH = 256
SSM_GROUP = 16
SSM_GROUPS = SSM_WIDTH // SSM_GROUP
SSM_STATE = 64
SSM_LANES = SSM_GROUPS * SSM_STATE
MIX_WIDTH = ATTN_WIDTH + FNET_WIDTH + SSM_WIDTH
Q_END = ATTN_WIDTH
K_END = Q_END + KV_WIDTH
V_END = K_END + KV_WIDTH
F_END = V_END + FNET_WIDTH
S_END = F_END + SSM_WIDTH
IN_WIDTH = S_END
D_FF = 2816
N_MOD = 9
NORM_EPS = 1e-6

LANES = 128
SUBLANES = 8
VMEM_LIMIT_BYTES = 60 * 1024 * 1024
TOKEN_TILE = 1024
FF_CHUNKS = 11
FF_CHUNK = D_FF // FF_CHUNKS
ATTN_BLOCKS_PER_STEP = 4
SCAN_STEPS = 64
SCAN_LANE_CHUNK = 256
COND_ROWS = 16
LOCAL_SPAN = BLOCK + 2 * WINDOW
QM_WIDTH = N_HEADS * LANES
KD_WIDTH = N_KV_HEADS * LANES
VD_WIDTH = N_KV_HEADS * 2 * LANES
WIDE = SUBLANES * SSM_WIDTH


def _params(*sem):
    return pltpu.CompilerParams(dimension_semantics=sem, vmem_limit_bytes=VMEM_LIMIT_BYTES)


def _dot(a, b):
    return jnp.dot(a, b, preferred_element_type=F32)


def _norm_mod(x, g, shift, scale):
    y = x * lax.rsqrt(jnp.mean(x * x, axis=-1, keepdims=True) + NORM_EPS)
    return (y * g) * (1.0 + scale) + shift


def _const_spec(shape, lead=()):
    nd = len(shape)
    idx = tuple(lead) + (0,) * nd
    return pl.BlockSpec((None,) * len(lead) + tuple(shape), lambda *_: idx,
                        pipeline_mode=pl.Buffered(1))


def _row_spec(tm, width):
    return pl.BlockSpec((tm, width), lambda i: (i, 0))


def _mod_spec(tiles_per_group):
    return pl.BlockSpec((1, N_MOD, D_MODEL), lambda i: (i // tiles_per_group, 0, 0))


def _wide_spec(tm, seq):
    if tm <= seq:
        tiles_per_seq = seq // tm

        def index(i):
            b = i // tiles_per_seq
            return (b // SUBLANES, i % tiles_per_seq, b % SUBLANES)
        return pl.BlockSpec((None, tm, SSM_WIDTH), index)
    nseq = tm // seq

    def index(i):
        b = i * nseq
        return (b // SUBLANES, 0, (b % SUBLANES) // nseq)
    return pl.BlockSpec((None, seq, nseq * SSM_WIDTH), index)


def _wide_store(ref, x, seq):
    for j in range(x.shape[0] // seq if x.shape[0] > seq else 1):
        rows = slice(j * seq, (j + 1) * seq) if x.shape[0] > seq else slice(None)
        ref[:, j * SSM_WIDTH:(j + 1) * SSM_WIDTH] = x[rows]


def _wide_load(ref, seq):
    ncol = ref.shape[1] // SSM_WIDTH
    if ncol == 1:
        return ref[...]
    return jnp.concatenate([ref[:, j * SSM_WIDTH:(j + 1) * SSM_WIDTH] for j in range(ncol)], axis=0)


def _mod_kernel(c_ref, w_ref, b_ref, o_ref):
    c = c_ref[...]
    a = (c * jax.nn.sigmoid(c)).astype(BF16)
    o_ref[0] = _dot(a, w_ref[0].astype(BF16)) + b_ref[0]


def _mod_call(cond, w_mod, b_mod):
    tn = N_MOD * D_MODEL // 4
    return pl.pallas_call(
        _mod_kernel,
        out_shape=jax.ShapeDtypeStruct((DEPTH, COND_ROWS, N_MOD * D_MODEL), F32),
        grid=(DEPTH, N_MOD * D_MODEL // tn),
        in_specs=[pl.BlockSpec((COND_ROWS, D_MODEL), lambda l, j: (0, 0)),
                  pl.BlockSpec((1, D_MODEL, tn), lambda l, j: (l, 0, j)),
                  pl.BlockSpec((1, 1, tn), lambda l, j: (l, 0, j))],
        out_specs=pl.BlockSpec((1, COND_ROWS, tn), lambda l, j: (l, 0, j)),
        compiler_params=_params("arbitrary", "arbitrary"),
        name="cond_mod",
    )(cond, w_mod, b_mod.reshape(DEPTH, 1, N_MOD * D_MODEL))


def _half_ffn(x, m, k0, g, w1_ref, w3_ref, w2_ref):
    h = _norm_mod(x, g, m[k0:k0 + 1], m[k0 + 1:k0 + 2]).astype(BF16)
    y = None
    for c in range(FF_CHUNKS):
        cols = slice(c * FF_CHUNK, (c + 1) * FF_CHUNK)
        a = _dot(h, w1_ref[:, cols])
        b = _dot(h, w3_ref[:, cols])
        t = ((a * jax.nn.sigmoid(a)) * b).astype(BF16)
        yc = _dot(t, w2_ref[cols, :])
        y = yc if y is None else y + yc
    return x + (0.5 * m[k0 + 2:k0 + 3]) * y


def _head_norm(z, e, g):
    ms = _dot((z * z).astype(BF16), e)
    return (z * lax.rsqrt(ms + NORM_EPS)) * g


def _low_half(shape):
    return lax.broadcasted_iota(jnp.int32, shape, 1) < HEAD_DIM


def _kv_slabs(k, v):
    low = _low_half(k.shape)
    kr = pltpu.roll(k, HEAD_DIM, 1)
    vr = pltpu.roll(v, HEAD_DIM, 1)
    kd = jnp.concatenate([jnp.where(low, k, kr), jnp.where(low, kr, k)], axis=1)
    vd = jnp.concatenate([jnp.where(low, v, 1.0), jnp.where(low, 1.0, vr),
                          jnp.where(low, vr, 1.0), jnp.where(low, 1.0, v)], axis=1)
    return kd.astype(BF16), vd.astype(BF16)


def _ffn_in_kernel(*refs, rope, seq):
    if rope:
        (x_ref, mod_ref, g0_ref, g1_ref, w1_ref, w3_ref, w2_ref, w_ref, qg_ref, kg_ref, e_ref,
         cos_ref, sa_ref, sb_ref, x1_ref, qm_ref, k_ref, v_ref, kd_ref, vd_ref, f_ref, s_ref) = refs
    else:
        (x_ref, mod_ref, g0_ref, g1_ref, w1_ref, w3_ref, w2_ref, w_ref, qg_ref, kg_ref, e_ref,
         x1_ref, qm_ref, k_ref, v_ref, kd_ref, vd_ref, f_ref, s_ref) = refs
    m = mod_ref[0]
    x1 = _half_ffn(x_ref[...], m, 0, g0_ref[...], w1_ref, w3_ref, w2_ref)
    x1_ref[...] = x1
    h = _norm_mod(x1, g1_ref[...], m[3:4], m[4:5]).astype(BF16)
    u = _dot(h, w_ref[...])
    q = _head_norm(u[:, :Q_END], e_ref[...], qg_ref[...])
    k = _head_norm(u[:, Q_END:K_END], e_ref[0:KV_WIDTH, 0:KV_WIDTH], kg_ref[...])
    v = u[:, K_END:V_END]
    if rope:
        cos, sa, sb = cos_ref[...], sa_ref[...], sb_ref[...]

        def rot(z):
            return z * cos + pltpu.roll(z, LANES - 16, 1) * sa + pltpu.roll(z, 16, 1) * sb
    else:
        def rot(z):
            return z

    low = _low_half((q.shape[0], LANES))
    for i in range(ATTN_WIDTH // LANES):
        qs = rot(q[:, i * LANES:(i + 1) * LANES]) * (ATTN_SCALE * LOG2E)
        qm_ref[:, (2 * i) * LANES:(2 * i + 1) * LANES] = jnp.where(low, qs, 0.0).astype(BF16)
        qm_ref[:, (2 * i + 1) * LANES:(2 * i + 2) * LANES] = jnp.where(low, 0.0, qs).astype(BF16)
    k = rot(k)
    k_ref[...] = k
    v_ref[...] = v
    kd_ref[...], vd_ref[...] = _kv_slabs(k, v)
    f_ref[...] = u[:, V_END:F_END].astype(BF16)
    _wide_store(s_ref, u[:, F_END:S_END], seq)


def _token_tile(seq):
    return TOKEN_TILE if TOKEN_TILE <= seq else min(TOKEN_TILE, SUBLANES * seq // 2)


def _ffn_in_call(x, mod, pw, layer, tables, rope, batch, seq, rows_per_group):
    n = x.shape[0]
    tm = _token_tile(seq)
    tiles_per_seq = max(seq // tm, 1)
    in_specs = [_row_spec(tm, D_MODEL), _mod_spec(rows_per_group // tm),
                _const_spec((1, D_MODEL), (layer, 0)),
                _const_spec((1, D_MODEL), (layer, 1)),
                _const_spec((D_MODEL, D_FF)),
                _const_spec((D_MODEL, D_FF)),
                _const_spec((D_FF, D_MODEL)),
                _const_spec((D_MODEL, IN_WIDTH), (layer,)),
                _const_spec((1, ATTN_WIDTH), (layer,)),
                _const_spec((1, KV_WIDTH), (layer,)),
                _const_spec((ATTN_WIDTH, ATTN_WIDTH))]
    args = [x, mod, pw['g'], pw['g'], *pw['ffn'][(layer, 0)], pw['w_in'], pw['qg'], pw['kg'],
            tables['e']]
    if rope:
        in_specs += [pl.BlockSpec((tm, LANES), lambda i: (i % tiles_per_seq, 0))] * 3
        args += list(tables['rope'])
    widths = (D_MODEL, QM_WIDTH, KV_WIDTH, KV_WIDTH, KD_WIDTH, VD_WIDTH, FNET_WIDTH)
    dtypes = (F32, BF16, F32, F32, BF16, BF16, BF16)
    out_shape = [jax.ShapeDtypeStruct((n, w), d) for w, d in zip(widths, dtypes)]
    out_shape.append(jax.ShapeDtypeStruct((batch // SUBLANES, seq, WIDE), F32))
    out_specs = [_row_spec(tm, w) for w in widths] + [_wide_spec(tm, seq)]
    return pl.pallas_call(
        functools.partial(_ffn_in_kernel, rope=rope, seq=seq),
        out_shape=out_shape,
        grid=(n // tm,),
        in_specs=in_specs,
        out_specs=out_specs,
        compiler_params=_params("arbitrary"),
        name="ffn_in_proj",
    )(*args)


def _attend(qm_ref, o_ref, rows, sink_ref, kv, bias):
    nq = rows.stop - rows.start
    low = _low_half((nq, LANES))
    for h in range(N_KV_HEADS):
        j0 = GQA_GROUP * h
        qst = jnp.concatenate([qm_ref[rows, (j0 + g) * LANES:(j0 + g + 1) * LANES]
                               for g in range(GQA_GROUP)], axis=0)
        sink = jnp.concatenate([jnp.broadcast_to(sink_ref[j0 + g:j0 + g + 1, :], (nq, LANES))
                                for g in range(GQA_GROUP)], axis=0) * LOG2E
        kd, vd = kv(h)
        s = lax.dot_general(qst, kd, (((1,), (1,)), ((), ())), preferred_element_type=F32)
        if bias is not None:
            s = s + bias
        m = jnp.maximum(jnp.max(s, axis=-1, keepdims=True), sink)
        e = jnp.exp2(s - jnp.concatenate([m] * (s.shape[1] // LANES), axis=1)).astype(BF16)
        o = _dot(e, vd)
        es = jnp.exp2(sink - m)
        for t in range(GQA_GROUP // 2):
            r0, r1 = (2 * t) * nq, (2 * t + 1) * nq
            num = jnp.where(low, o[r0:r0 + nq, :LANES], o[r1:r1 + nq, LANES:])
            den = (jnp.where(low, o[r0:r0 + nq, LANES:], o[r1:r1 + nq, :LANES])
                   + jnp.where(low, es[r0:r0 + nq], es[r1:r1 + nq]))
            c0 = (j0 // 2 + t) * LANES
            o_ref[rows, c0:c0 + LANES] = (num * (1.0 / den)).astype(o_ref.dtype)


def _cast_specs(jobs, nsteps, step_of):
    in_specs, out_specs, out_shape = [], [], []
    for arr, layer, half in jobs:
        rows, cols = arr.shape[2:]
        rb = rows // nsteps
        in_specs.append(pl.BlockSpec((None, None, rb, cols),
                                     lambda *g, layer=layer, half=half: (layer, half, step_of(*g), 0)))
        out_specs.append(pl.BlockSpec((rb, cols), lambda *g: (step_of(*g), 0)))
        out_shape.append(jax.ShapeDtypeStruct((rows, cols), BF16))
    return in_specs, out_specs, out_shape


def _run_casts(srcs, dsts):
    for s, d in zip(srcs, dsts):
        d[...] = s[...].astype(BF16)


def _ctx_attn_kernel(qm_ref, kd_ref, vd_ref, sink_ref, *rest, seq, nseq, ncast):
    o_ref = rest[ncast]
    _run_casts(rest[:ncast], rest[ncast + 1:])
    for i in range(nseq):
        rows = slice(i * seq, (i + 1) * seq)

        def kv(h, rows=rows):
            return (kd_ref[rows, h * LANES:(h + 1) * LANES], vd_ref[rows, 2 * h * LANES:2 * (h + 1) * LANES])

        _attend(qm_ref, o_ref, rows, sink_ref, kv, None)


def _ctx_attn_call(qm, kd, vd, sink, layer, batch, seq, casts=()):
    nseq = ATTN_BLOCKS_PER_STEP
    rows = nseq * seq
    nsteps = batch // nseq
    c_in, c_out, c_shape = _cast_specs(casts, nsteps, lambda b: b)
    out = pl.pallas_call(
        functools.partial(_ctx_attn_kernel, seq=seq, nseq=nseq, ncast=len(casts)),
        out_shape=[jax.ShapeDtypeStruct((batch * seq, ATTN_WIDTH), BF16)] + c_shape,
        grid=(nsteps,),
        in_specs=[_row_spec(rows, QM_WIDTH), _row_spec(rows, KD_WIDTH), _row_spec(rows, VD_WIDTH),
                  pl.BlockSpec((None, N_HEADS, LANES), lambda b: (layer, 0, 0))] + c_in,
        out_specs=[_row_spec(rows, ATTN_WIDTH)] + c_out,
        compiler_params=_params("arbitrary"),
        name="ctx_attn",
    )(qm, kd, vd, sink, *[job[0] for job in casts])
    return out[0], out[1:]


def _lat_attn_kernel(qm_ref, kl_ref, vl_ref, kc_ref, vc_ref, bias_ref, sink_ref, *rest,
                     seq, nblk, ncast):
    o_ref = rest[ncast]
    kc_sc, vc_sc = rest[-2:]
    _run_casts(rest[:ncast], rest[ncast + 1:-2])
    j = pl.program_id(1)
    nb = seq // BLOCK

    @pl.when(j == 0)
    def _():
        kc_sc[...], vc_sc[...] = _kv_slabs(kc_ref[...], vc_ref[...])

    for i in range(nblk):
        n = j * nblk + i
        start = pl.multiple_of(jnp.clip(n * BLOCK - WINDOW, 0, seq - LOCAL_SPAN), BLOCK)
        variant = jnp.where(n == 0, 0, jnp.where(n == nb - 1, 2, 1))

        def kv(h, start=start):
            kc, vc = slice(h * LANES, (h + 1) * LANES), slice(2 * h * LANES, 2 * (h + 1) * LANES)
            return (jnp.concatenate([kc_sc[:, kc], kl_ref[pl.ds(start, LOCAL_SPAN), kc]], axis=0),
                    jnp.concatenate([vc_sc[:, vc], vl_ref[pl.ds(start, LOCAL_SPAN), vc]], axis=0))

        _attend(qm_ref, o_ref, slice(i * BLOCK, (i + 1) * BLOCK), sink_ref, kv, bias_ref[variant])


def _lat_attn_call(qm, kd, vd, kctx, vctx, bias, sink, layer, batch, seq, casts=()):
    past = kctx.shape[2]
    nblk = 2 * ATTN_BLOCKS_PER_STEP
    steps = seq // (BLOCK * nblk)
    rows = BLOCK * nblk
    c_in, c_out, c_shape = _cast_specs(casts, batch * steps, lambda b, j: b * steps + j)
    out = pl.pallas_call(
        functools.partial(_lat_attn_kernel, seq=seq, nblk=nblk, ncast=len(casts)),
        out_shape=[jax.ShapeDtypeStruct((batch * seq, ATTN_WIDTH), BF16)] + c_shape,
        grid=(batch, steps),
        in_specs=[pl.BlockSpec((rows, QM_WIDTH), lambda b, j: (b * steps + j, 0)),
                  pl.BlockSpec((seq, KD_WIDTH), lambda b, j: (b, 0)),
                  pl.BlockSpec((seq, VD_WIDTH), lambda b, j: (b, 0)),
                  pl.BlockSpec((None, None, past, KV_WIDTH), lambda b, j: (b, layer, 0, 0)),
                  pl.BlockSpec((None, None, past, KV_WIDTH), lambda b, j: (b, layer, 0, 0)),
                  pl.BlockSpec(bias.shape, lambda b, j: (0, 0, 0)),
                  pl.BlockSpec((None, N_HEADS, LANES), lambda b, j: (layer, 0, 0))] + c_in,
        out_specs=[pl.BlockSpec((rows, ATTN_WIDTH), lambda b, j: (b * steps + j, 0))] + c_out,
        scratch_shapes=[pltpu.VMEM((past, KD_WIDTH), BF16), pltpu.VMEM((past, VD_WIDTH), BF16)],
        compiler_params=_params("arbitrary", "arbitrary"),
        name="lat_attn",
    )(qm, kd, vd, kctx, vctx, bias, sink, *[job[0] for job in casts])
    return out[0], out[1:]


def _fnet_kernel(x_ref, wc_ref, dm_ref, wf_ref, o_ref, *, seq):
    for i in range(x_ref.shape[0] // seq):
        rows = slice(i * seq, (i + 1) * seq)
        y = _dot(x_ref[rows, :], wc_ref[...])
        yst = jnp.concatenate([y[:, :FNET_WIDTH], y[:, FNET_WIDTH:]], axis=0).astype(BF16)
        z = _dot(dm_ref[...], yst) * ((seq * FNET_HEAD_DIM) ** -0.5)
        o_ref[rows, :] = _dot(z.astype(BF16), wf_ref[...]).astype(o_ref.dtype)


def _fnet_call(x, wc, dm, wf, layer, batch, seq):
    rows = max(seq, TOKEN_TILE)
    return pl.pallas_call(
        functools.partial(_fnet_kernel, seq=seq),
        out_shape=jax.ShapeDtypeStruct((batch * seq, FNET_WIDTH), BF16),
        grid=(batch * seq // rows,),
        in_specs=[_row_spec(rows, FNET_WIDTH),
                  _const_spec((FNET_WIDTH, 2 * FNET_WIDTH)),
                  _const_spec((seq, 2 * seq)),
                  _const_spec((FNET_WIDTH, FNET_WIDTH), (layer,))],
        out_specs=_row_spec(rows, FNET_WIDTH),
        compiler_params=_params("arbitrary"),
        name="fourier_mix",
    )(x, wc, dm, wf)


def _s5_kernel(xf_ref, xb_ref, xfn_ref, xbn_ref, h0_ref, bm_ref, cm_ref, lam_ref,
               yf_ref, yb_ref, fin_ref, xs, buf0, buf1, sb, ys, carry, *, steps, nchunks):
    first = (pl.program_id(0) == 0) & (pl.program_id(1) == 0)
    c = pl.program_id(1)
    nslab = SSM_WIDTH // LANES
    w = SCAN_LANE_CHUNK

    def relayout_in(x_refs, base):
        xin = []
        for d, x_ref in enumerate(x_refs):
            for b in range(SUBLANES):
                for sl in range(nslab):
                    c0 = b * SSM_WIDTH + sl * LANES
                    xs[base + d * nslab + sl, pl.ds(b, steps, stride=SUBLANES), :] = x_ref[:, c0:c0 + LANES]
            xin.append(jnp.concatenate([xs[base + d * nslab + sl] for sl in range(nslab)],
                                       axis=1).astype(BF16))
        return xin

    def step(cur, nxt):
        xin = relayout_in((xfn_ref, xbn_ref), 2 * nslab)
        yacc = [None, None]
        for lc in range(SSM_LANES // w):
            re = slice(lc * w, (lc + 1) * w)
            im = slice(SSM_LANES + lc * w, SSM_LANES + (lc + 1) * w)
            for d in range(2):
                nxt[d, :, re] = _dot(xin[d], bm_ref[d, :, re])
                nxt[d, :, im] = _dot(xin[d], bm_ref[d, :, im])
            lam = [(lam_ref[d, :, re], lam_ref[d, :, im]) for d in range(2)]
            st = [(carry[d, :, re], carry[d, :, im]) for d in range(2)]
            prev = [None, None]
            for t in range(steps):
                for d in range(2):
                    tt = t if d == 0 else steps - 1 - t
                    rows = slice(tt * SUBLANES, (tt + 1) * SUBLANES)
                    (lr, li), (sr, si) = lam[d], st[d]
                    nr = (lr * sr - li * si) + cur[d, rows, re]
                    ni = (lr * si + li * sr) + cur[d, rows, im]
                    st[d] = (nr, ni)
                    if t % 2 == 0:
                        prev[d] = (nr, ni)
                    else:
                        lo = min(tt, tt + (1 if d == 1 else -1)) * SUBLANES
                        pair = slice(lo, lo + 2 * SUBLANES)
                        first, second = (prev[d], (nr, ni)) if d == 0 else ((nr, ni), prev[d])
                        sb[d, pair, re] = jnp.concatenate([first[0], second[0]], axis=0).astype(BF16)
                        sb[d, pair, im] = jnp.concatenate([first[1], second[1]], axis=0).astype(BF16)
            for d in range(2):
                carry[d, :, re], carry[d, :, im] = st[d]
            for d in range(2):
                part = _dot(sb[d, :, re], cm_ref[d, re, :]) + _dot(sb[d, :, im], cm_ref[d, im, :])
                yacc[d] = part if yacc[d] is None else yacc[d] + part
        for d, y_ref in enumerate((yf_ref, yb_ref)):
            for sl in range(nslab):
                ys[d * nslab + sl] = yacc[d][:, sl * LANES:(sl + 1) * LANES]
            for b in range(SUBLANES):
                for sl in range(nslab):
                    c0 = b * SSM_WIDTH + sl * LANES
                    y_ref[:, c0:c0 + LANES] = ys[d * nslab + sl, pl.ds(b, steps, stride=SUBLANES), :]

    @pl.when(c == 0)
    def _():
        carry[...] = h0_ref[0]

    @pl.when(first)
    def _():
        xin = relayout_in((xf_ref, xb_ref), 0)
        for d in range(2):
            buf0[d] = _dot(xin[d], bm_ref[d])

    @pl.when(c % 2 == 0)
    def _():
        step(buf0, buf1)

    @pl.when(c % 2 == 1)
    def _():
        step(buf1, buf0)

    @pl.when(c == nchunks - 1)
    def _():
        fin_ref[0] = carry[...]


def _s5_call(x_wide, h0, pw, layer, groups, seq):
    steps = SCAN_STEPS
    rows = steps * SUBLANES
    nchunks = seq // steps
    assert nchunks % 2 == 0
    width = 2 * SSM_LANES
    nslab = SSM_WIDTH // LANES
    last = groups * nchunks - 1

    def nxt(g, c):
        s = jnp.minimum(g * nchunks + c + 1, last)
        return s // nchunks, s % nchunks

    def fwd_next(g, c):
        g2, c2 = nxt(g, c)
        return (g2, c2, 0)

    def bwd_next(g, c):
        g2, c2 = nxt(g, c)
        return (g2, nchunks - 1 - c2, 0)

    return pl.pallas_call(
        functools.partial(_s5_kernel, steps=steps, nchunks=nchunks),
        out_shape=[jax.ShapeDtypeStruct((groups, seq, WIDE), F32),
                   jax.ShapeDtypeStruct((groups, seq, WIDE), F32),
                   jax.ShapeDtypeStruct((groups, 2, SUBLANES, width), F32)],
        grid=(groups, nchunks),
        in_specs=[pl.BlockSpec((None, steps, WIDE), lambda g, c: (0, 0, 0)),
                  pl.BlockSpec((None, steps, WIDE), lambda g, c: (0, nchunks - 1, 0)),
                  pl.BlockSpec((None, steps, WIDE), fwd_next),
                  pl.BlockSpec((None, steps, WIDE), bwd_next),
                  pl.BlockSpec((1, 2, SUBLANES, width), lambda g, c: (g, 0, 0, 0)),
                  pl.BlockSpec((None, 2, SSM_WIDTH, width), lambda g, c: (layer, 0, 0, 0)),
                  pl.BlockSpec((None, 2, width, SSM_WIDTH), lambda g, c: (layer, 0, 0, 0)),
                  pl.BlockSpec((None, 2, SUBLANES, width), lambda g, c: (layer, 0, 0, 0))],
        out_specs=[pl.BlockSpec((None, steps, WIDE), lambda g, c: (g, c, 0)),
                   pl.BlockSpec((None, steps, WIDE), lambda g, c: (g, nchunks - 1 - c, 0)),
                   pl.BlockSpec((1, 2, SUBLANES, width), lambda g, c: (g, 0, 0, 0))],
        scratch_shapes=[pltpu.VMEM((4 * nslab, rows, LANES), F32),
                        pltpu.VMEM((2, rows, width), F32),
                        pltpu.VMEM((2, rows, width), F32),
                        pltpu.VMEM((2, rows, width), BF16),
                        pltpu.VMEM((2 * nslab, rows, LANES), F32),
                        pltpu.VMEM((2, SUBLANES, width), F32)],
        compiler_params=_params("arbitrary", "arbitrary"),
        name="s5_scan",
    )(x_wide, x_wide, x_wide, x_wide, h0, pw['bm'], pw['cm'], pw['lam'])


def _out_ffn_kernel(x_ref, mod_ref, a_ref, f_ref, us_ref, yf_ref, yb_ref, d_ref, wg_ref, wo_ref,
                    g_ref, w1_ref, w3_ref, w2_ref, o_ref, *, seq):
    m = mod_ref[0]
    us, yf, yb = (_wide_load(r, seq) for r in (us_ref, yf_ref, yb_ref))
    ys = jax.nn.gelu((d_ref[...] * us + yf) + yb)
    gl = _dot(ys.astype(BF16), wg_ref[...])
    so = gl[:, :SSM_WIDTH] * jax.nn.sigmoid(gl[:, SSM_WIDTH:])
    mixed = jnp.concatenate([a_ref[...], f_ref[...], so.astype(BF16)], axis=1)
    x2 = x_ref[...] + m[5:6] * _dot(mixed, wo_ref[...])
    o_ref[...] = _half_ffn(x2, m, 6, g_ref[...], w1_ref, w3_ref, w2_ref)


def _out_ffn_call(x, mod, a, f, us, yf, yb, pw, layer, seq, rows_per_group):
    n = x.shape[0]
    tm = _token_tile(seq)
    wide = _wide_spec(tm, seq)
    return pl.pallas_call(
        functools.partial(_out_ffn_kernel, seq=seq),
        out_shape=jax.ShapeDtypeStruct((n, D_MODEL), F32),
        grid=(n // tm,),
        in_specs=[_row_spec(tm, D_MODEL), _mod_spec(rows_per_group // tm),
                  _row_spec(tm, ATTN_WIDTH), _row_spec(tm, FNET_WIDTH), wide, wide, wide,
                  _const_spec((1, SSM_WIDTH), (layer,)),
                  _const_spec((SSM_WIDTH, 2 * SSM_WIDTH), (layer,)),
                  _const_spec((MIX_WIDTH, D_MODEL), (layer,)),
                  _const_spec((1, D_MODEL), (layer, 2)),
                  _const_spec((D_MODEL, D_FF)),
                  _const_spec((D_MODEL, D_FF)),
                  _const_spec((D_FF, D_MODEL))],
        out_specs=_row_spec(tm, D_MODEL),
        compiler_params=_params("arbitrary"),
        name="out_proj_ffn",
    )(x, mod, a, f, us, yf, yb, pw['d'], pw['w_glu'], pw['w_out'], pw['g'], *pw['ffn'][(layer, 1)])


def _rope_tables(seq):
    pos = np.arange(seq)
    row, col = pos // GRID_W, pos % GRID_W
    quarter = HEAD_DIM // 4
    inv_freq = 1.0 / (ROPE_BASE ** (np.arange(quarter, dtype=np.float64) * 2.0 / (HEAD_DIM // 2)))
    lane = np.arange(LANES)
    in_head = lane % HEAD_DIM
    p = np.where((in_head < HEAD_DIM // 2)[None, :], row[:, None], col[:, None]).astype(np.float64)
    ang = p * inv_freq[lane % quarter][None, :]
    first = ((lane % (HEAD_DIM // 2)) < quarter)[None, :]
    cos = np.cos(ang)
    sa = np.where(first, -np.sin(ang), 0.0)
    sb = np.where(first, 0.0, np.sin(ang))
    return tuple(jnp.asarray(t, dtype=F32) for t in (cos, sa, sb))


def _dft_tables(seq):
    kl = np.outer(np.arange(seq), np.arange(seq)) % seq
    ang = 2.0 * np.pi * kl / seq
    dm = np.concatenate([np.cos(ang), -np.sin(ang)], axis=1)
    mc = np.outer(np.arange(FNET_HEAD_DIM), np.arange(FNET_HEAD_DIM)) % FNET_HEAD_DIM
    a64 = 2.0 * np.pi * mc / FNET_HEAD_DIM
    eye = np.eye(FNET_HEADS)
    wc = np.concatenate([np.kron(eye, np.cos(a64)), np.kron(eye, np.sin(a64))], axis=1)
    return jnp.asarray(dm, dtype=F32).astype(BF16), jnp.asarray(wc, dtype=F32).astype(BF16)


def _head_mean_matrix():
    return jnp.asarray(np.kron(np.eye(N_HEADS), np.full((HEAD_DIM, HEAD_DIM), 1.0 / HEAD_DIM)), dtype=BF16)


def _window_bias(past):
    i = np.arange(GQA_GROUP * BLOCK)[:, None] % BLOCK
    j = np.arange(LOCAL_SPAN)[None, :]
    out = np.zeros((3, GQA_GROUP * BLOCK, past + LOCAL_SPAN), np.float32)
    for v, off in enumerate((0, WINDOW, 2 * WINDOW)):
        out[v, :, past:] = np.where(np.abs(j - off - i) <= WINDOW, 0.0, NEG_INF)
    return jnp.asarray(out)


def _s5_matrices(lam_re, lam_im, b_re, b_im, c_re, c_im, log_step):
    step = jnp.exp(log_step)[..., None]
    mag = jnp.exp(lam_re * step)
    lr = mag * jnp.cos(lam_im * step)
    li = mag * jnp.sin(lam_im * step)
    den = lam_re * lam_re + lam_im * lam_im
    cr = ((lr - 1.0) * lam_re + li * lam_im) / den
    ci = (li * lam_re - (lr - 1.0) * lam_im) / den
    bbr = cr[..., None] * b_re - ci[..., None] * b_im
    bbi = cr[..., None] * b_im + ci[..., None] * b_re
    same_group = jnp.asarray(np.arange(SSM_WIDTH)[:, None] // SSM_GROUP
                             == np.arange(SSM_LANES)[None, :] // SSM_STATE, dtype=F32)

    def in_mat(b):
        rows = b.transpose(0, 1, 4, 2, 3).reshape(DEPTH, 2, SSM_GROUP, SSM_LANES)
        return jnp.tile(rows, (1, 1, SSM_GROUPS, 1)) * same_group

    def out_mat(cw):
        rows = cw.transpose(0, 1, 4, 2, 3).reshape(DEPTH, 2, SSM_STATE, SSM_WIDTH)
        return jnp.tile(rows, (1, 1, SSM_GROUPS, 1)) * same_group.T

    bm = jnp.concatenate([in_mat(bbr), in_mat(bbi)], axis=-1).astype(BF16)
    cm = jnp.concatenate([out_mat(c_re), -out_mat(c_im)], axis=-2).astype(BF16)
    lrow = jnp.concatenate([lr.reshape(DEPTH, 2, SSM_LANES), li.reshape(DEPTH, 2, SSM_LANES)], axis=-1)
    lam = jnp.broadcast_to(lrow[:, :, None, :], (DEPTH, 2, SUBLANES, 2 * SSM_LANES))
    return bm, cm, lam


def _stream_layer(x, mod, pw, layer, batch, seq, rows_per_group, tables, ctx_kv, h0):
    latent = ctx_kv is not None
    groups = batch // SUBLANES
    x1, qm, k, v, kd, vd, f, us = _ffn_in_call(x, mod, pw, layer, tables, latent, batch, seq, rows_per_group)
    pending = [key for key in ([(0, 1)] if not latent else [(l, h) for l in range(1, DEPTH) for h in range(2)])
               if key not in pw['ffn']]
    casts = [(w, l, h) for (l, h) in pending for w in pw['ffn_f32']]
    if latent:
        a, cast = _lat_attn_call(qm, kd, vd, ctx_kv[0], ctx_kv[1], tables['bias'], pw['sink'], layer,
                                 batch, seq, casts)
    else:
        a, cast = _ctx_attn_call(qm, kd, vd, pw['sink'], layer, batch, seq, casts)
    for i, key in enumerate(pending):
        pw['ffn'][key] = tuple(cast[3 * i:3 * i + 3])
    dm, wc = tables['dft'][seq]
    fz = _fnet_call(f, wc, dm, pw['w_fnet'], layer, batch, seq)
    yf, yb, fin = _s5_call(us, h0, pw, layer, groups, seq)
    x = _out_ffn_call(x1, mod, a, fz, us, yf, yb, pw, layer, seq, rows_per_group)
    return x, (k, v, fin)


def kernel(x_prompt, x_sample, cache_k, cache_v, state_ssm_re, state_ssm_im, c, c_ctx, w_mod, b_mod, norm_g, ffn_w1, ffn_w3, ffn_w2, w_in, w_out, q_norm_g, k_norm_g, attn_sink, w_fnet, ssm_lambda_re, ssm_lambda_im, ssm_b_re, ssm_b_im, ssm_c_re, ssm_c_im, ssm_d, ssm_log_step, ssm_w_glu):
    batch, seq, _ = x_prompt.shape
    dec_batch, dec_seq, _ = x_sample.shape
    past = cache_k.shape[2]

    tables = {'e': _head_mean_matrix(), 'rope': _rope_tables(dec_seq), 'bias': _window_bias(past),
              'dft': {s: _dft_tables(s) for s in {seq, dec_seq}}}

    bm, cm, lam = _s5_matrices(ssm_lambda_re, ssm_lambda_im, ssm_b_re, ssm_b_im, ssm_c_re, ssm_c_im,
                               ssm_log_step)
    pw = {'g': norm_g.reshape(DEPTH, 3, 1, D_MODEL),
          'ffn_f32': (ffn_w1, ffn_w3, ffn_w2),
          'ffn': {(0, 0): tuple(w[0, 0].astype(BF16) for w in (ffn_w1, ffn_w3, ffn_w2))},
          'w_in': w_in.astype(BF16), 'w_out': w_out.astype(BF16),
          'qg': jnp.tile(q_norm_g, (1, N_HEADS)).reshape(DEPTH, 1, ATTN_WIDTH),
          'kg': jnp.tile(k_norm_g, (1, N_KV_HEADS)).reshape(DEPTH, 1, KV_WIDTH),
          'sink': jnp.broadcast_to(attn_sink[:, :, None], (DEPTH, N_HEADS, LANES)),
          'w_fnet': w_fnet.astype(BF16), 'bm': bm, 'cm': cm, 'lam': lam,
          'd': ssm_d.reshape(DEPTH, 1, SSM_WIDTH), 'w_glu': ssm_w_glu.astype(BF16)}

    cond = jnp.zeros((COND_ROWS, D_MODEL), F32).at[0].set(c_ctx).at[1:1 + dec_batch].set(c)
    mods = _mod_call(cond, w_mod, b_mod).reshape(DEPTH, COND_ROWS, N_MOD, D_MODEL)

    ctx_kv = (cache_k.reshape(dec_batch, DEPTH, past, KV_WIDTH), cache_v.reshape(dec_batch, DEPTH, past, KV_WIDTH))
    h0_lat = jnp.concatenate([state_ssm_re.reshape(dec_batch, DEPTH, 2, SSM_LANES),
                              state_ssm_im.reshape(dec_batch, DEPTH, 2, SSM_LANES)], axis=-1)
    h0_lat = h0_lat.reshape(dec_batch // SUBLANES, SUBLANES, DEPTH, 2, 2 * SSM_LANES).transpose(2, 0, 3, 1, 4)
    h0_ctx = jnp.zeros((batch // SUBLANES, 2, SUBLANES, 2 * SSM_LANES), F32)

    yp = x_prompt.reshape(batch * seq, D_MODEL)
    ys = x_sample.reshape(dec_batch * dec_seq, D_MODEL)
    ks, vs, sre, sim = [], [], [], []
    for l in range(DEPTH):
        yp, (k_l, v_l, fin) = _stream_layer(yp, mods[l, 0:1], pw, l, batch, seq, batch * seq, tables,
                                            None, h0_ctx)
        ks.append(k_l.reshape(batch, seq, N_KV_HEADS, HEAD_DIM))
        vs.append(v_l.reshape(batch, seq, N_KV_HEADS, HEAD_DIM))
        fin = fin.reshape(batch // SUBLANES, 2, SUBLANES, 2, SSM_GROUPS, SSM_STATE)
        fin = fin.transpose(3, 0, 2, 1, 4, 5).reshape(2, batch, 2, SSM_GROUPS, SSM_STATE)
        sre.append(fin[0])
        sim.append(fin[1])
        ys, _ = _stream_layer(ys, mods[l, 1:1 + dec_batch], pw, l, dec_batch, dec_seq, dec_seq, tables,
                              ctx_kv, h0_lat[l])

    return (yp.reshape(batch, seq, D_MODEL), ys.reshape(dec_batch, dec_seq, D_MODEL),
            jnp.stack(ks, axis=1), jnp.stack(vs, axis=1), jnp.stack(sre, axis=1), jnp.stack(sim, axis=1))
```

```python
import functools
import math

import numpy as np
import jax
import jax.numpy as jnp
from jax import lax
from jax.experimental import pallas as pl
from jax.experimental.pallas import tpu as pltpu

F32 = jnp.float32
BF16 = jnp.bfloat16

D_MODEL = 1024
DEPTH = 2
GRID_W = 64
HEAD_DIM = 64
N_HEADS = 8
N_KV_HEADS = 2
GQA_GROUP = N_HEADS // N_KV_HEADS
ATTN_WIDTH = N_HEADS * HEAD_DIM
KV_WIDTH = N_KV_HEADS * HEAD_DIM
WINDOW = 128
BLOCK = 128
ATTN_SCALE = HEAD_DIM ** -0.5
ROPE_BASE = 10000.0
NEG_INF = -1e30
LOG2E = math.log2(math.e)
FNET_HEADS = 4
FNET_HEAD_DIM = 64
FNET_WIDTH = FNET_HEADS * FNET_HEAD_DIM
SSM_WIDTH = 256
SSM_GROUP = 16
SSM_GROUPS = SSM_WIDTH // SSM_GROUP
SSM_STATE = 64
SSM_LANES = SSM_GROUPS * SSM_STATE
MIX_WIDTH = ATTN_WIDTH + FNET_WIDTH + SSM_WIDTH
Q_END = ATTN_WIDTH
K_END = Q_END + KV_WIDTH
V_END = K_END + KV_WIDTH
F_END = V_END + FNET_WIDTH
S_END = F_END + SSM_WIDTH
IN_WIDTH = S_END
D_FF = 2816
N_MOD = 9
NORM_EPS = 1e-6

LANES = 128
SUBLANES = 8
VMEM_LIMIT_BYTES = 60 * 1024 * 1024
TOKEN_TILE = 1024
FF_CHUNKS = 11
FF_CHUNK = D_FF // FF_CHUNKS
ATTN_BLOCKS_PER_STEP = 4
SCAN_STEPS = 64
SCAN_LANE_CHUNK = 256
COND_ROWS = 16
LOCAL_SPAN = BLOCK + 2 * WINDOW
QM_WIDTH = N_HEADS * LANES
KD_WIDTH = N_KV_HEADS * LANES
VD_WIDTH = N_KV_HEADS * 2 * LANES
WIDE = SUBLANES * SSM_WIDTH


def _params(*sem):
    return pltpu.CompilerParams(dimension_semantics=sem, vmem_limit_bytes=VMEM_LIMIT_BYTES)


def _dot(a, b):
    return jnp.dot(a, b, preferred_element_type=F32)


def _norm_mod(x, g, shift, scale):
    y = x * lax.rsqrt(jnp.mean(x * x, axis=-1, keepdims=True) + NORM_EPS)
    return (y * g) * (1.0 + scale) + shift


def _const_spec(shape, lead=()):
    nd = len(shape)
    idx = tuple(lead) + (0,) * nd
    return pl.BlockSpec((None,) * len(lead) + tuple(shape), lambda *_: idx,
                        pipeline_mode=pl.Buffered(1))


def _row_spec(tm, width):
    return pl.BlockSpec((tm, width), lambda i: (i, 0))


def _mod_spec(tiles_per_group):
    return pl.BlockSpec((1, N_MOD, D_MODEL), lambda i: (i // tiles_per_group, 0, 0))


def _wide_spec(tm, seq):
    if tm <= seq:
        tiles_per_seq = seq // tm

        def index(i):
            b = i // tiles_per_seq
            return (b // SUBLANES, i % tiles_per_seq, b % SUBLANES)
        return pl.BlockSpec((None, tm, SSM_WIDTH), index)
    nseq = tm // seq

    def index(i):
        b = i * nseq
        return (b // SUBLANES, 0, (b % SUBLANES) // nseq)
    return pl.BlockSpec((None, seq, nseq * SSM_WIDTH), index)


def _wide_store(ref, x, seq):
    for j in range(x.shape[0] // seq if x.shape[0] > seq else 1):
        rows = slice(j * seq, (j + 1) * seq) if x.shape[0] > seq else slice(None)
        ref[:, j * SSM_WIDTH:(j + 1) * SSM_WIDTH] = x[rows]


def _wide_load(ref, seq):
    ncol = ref.shape[1] // SSM_WIDTH
    if ncol == 1:
        return ref[...]
    return jnp.concatenate([ref[:, j * SSM_WIDTH:(j + 1) * SSM_WIDTH] for j in range(ncol)], axis=0)


def _mod_kernel(c_ref, w_ref, b_ref, o_ref):
    c = c_ref[...]
    a = (c * jax.nn.sigmoid(c)).astype(BF16)
    o_ref[0] = _dot(a, w_ref[0].astype(BF16)) + b_ref[0]


def _mod_call(cond, w_mod, b_mod):
    tn = N_MOD * D_MODEL // 4
    return pl.pallas_call(
        _mod_kernel,
        out_shape=jax.ShapeDtypeStruct((DEPTH, COND_ROWS, N_MOD * D_MODEL), F32),
        grid=(DEPTH, N_MOD * D_MODEL // tn),
        in_specs=[pl.BlockSpec((COND_ROWS, D_MODEL), lambda l, j: (0, 0)),
                  pl.BlockSpec((1, D_MODEL, tn), lambda l, j: (l, 0, j)),
                  pl.BlockSpec((1, 1, tn), lambda l, j: (l, 0, j))],
        out_specs=pl.BlockSpec((1, COND_ROWS, tn), lambda l, j: (l, 0, j)),
        compiler_params=_params("arbitrary", "arbitrary"),
        name="cond_mod",
    )(cond, w_mod, b_mod.reshape(DEPTH, 1, N_MOD * D_MODEL))


def _half_ffn(x, m, k0, g, w1_ref, w3_ref, w2_ref):
    h = _norm_mod(x, g, m[k0:k0 + 1], m[k0 + 1:k0 + 2]).astype(BF16)
    y = None
    for c in range(FF_CHUNKS):
        cols = slice(c * FF_CHUNK, (c + 1) * FF_CHUNK)
        a = _dot(h, w1_ref[:, cols])
        b = _dot(h, w3_ref[:, cols])
        t = ((a * jax.nn.sigmoid(a)) * b).astype(BF16)
        yc = _dot(t, w2_ref[cols, :])
        y = yc if y is None else y + yc
    return x + (0.5 * m[k0 + 2:k0 + 3]) * y


def _head_norm(z, e, g):
    ms = _dot((z * z).astype(BF16), e)
    return (z * lax.rsqrt(ms + NORM_EPS)) * g


def _low_half(shape):
    return lax.broadcasted_iota(jnp.int32, shape, 1) < HEAD_DIM


def _kv_slabs(k, v):
    low = _low_half(k.shape)
    kr = pltpu.roll(k, HEAD_DIM, 1)
    vr = pltpu.roll(v, HEAD_DIM, 1)
    kd = jnp.concatenate([jnp.where(low, k, kr), jnp.where(low, kr, k)], axis=1)
    vd = jnp.concatenate([jnp.where(low, v, 1.0), jnp.where(low, 1.0, vr),
                          jnp.where(low, vr, 1.0), jnp.where(low, 1.0, v)], axis=1)
    return kd.astype(BF16), vd.astype(BF16)


def _ffn_in_kernel(*refs, rope, seq):
    if rope:
        (x_ref, mod_ref, g0_ref, g1_ref, w1_ref, w3_ref, w2_ref, w_ref, qg_ref, kg_ref, e_ref,
         cos_ref, sa_ref, sb_ref, x1_ref, qm_ref, k_ref, v_ref, kd_ref, vd_ref, f_ref, s_ref) = refs
    else:
        (x_ref, mod_ref, g0_ref, g1_ref, w1_ref, w3_ref, w2_ref, w_ref, qg_ref, kg_ref, e_ref,
         x1_ref, qm_ref, k_ref, v_ref, kd_ref, vd_ref, f_ref, s_ref) = refs
    m = mod_ref[0]
    x1 = _half_ffn(x_ref[...], m, 0, g0_ref[...], w1_ref, w3_ref, w2_ref)
    x1_ref[...] = x1
    h = _norm_mod(x1, g1_ref[...], m[3:4], m[4:5]).astype(BF16)
    u = _dot(h, w_ref[...])
    q = _head_norm(u[:, :Q_END], e_ref[...], qg_ref[...])
    k = _head_norm(u[:, Q_END:K_END], e_ref[0:KV_WIDTH, 0:KV_WIDTH], kg_ref[...])
    v = u[:, K_END:V_END]
    if rope:
        cos, sa, sb = cos_ref[...], sa_ref[...], sb_ref[...]

        def rot(z):
            return z * cos + pltpu.roll(z, LANES - 16, 1) * sa + pltpu.roll(z, 16, 1) * sb
    else:
        def rot(z):
            return z

    low = _low_half((q.shape[0], LANES))
    for i in range(ATTN_WIDTH // LANES):
        qs = rot(q[:, i * LANES:(i + 1) * LANES]) * (ATTN_SCALE * LOG2E)
        qm_ref[:, (2 * i) * LANES:(2 * i + 1) * LANES] = jnp.where(low, qs, 0.0).astype(BF16)
        qm_ref[:, (2 * i + 1) * LANES:(2 * i + 2) * LANES] = jnp.where(low, 0.0, qs).astype(BF16)
    k = rot(k)
    k_ref[...] = k
    v_ref[...] = v
    kd_ref[...], vd_ref[...] = _kv_slabs(k, v)
    f_ref[...] = u[:, V_END:F_END].astype(BF16)
    _wide_store(s_ref, u[:, F_END:S_END], seq)


def _token_tile(seq):
    return TOKEN_TILE if TOKEN_TILE <= seq else min(TOKEN_TILE, SUBLANES * seq // 2)


def _ffn_in_call(x, mod, pw, layer, tables, rope, batch, seq, rows_per_group):
    n = x.shape[0]
    tm = _token_tile(seq)
    tiles_per_seq = max(seq // tm, 1)
    in_specs = [_row_spec(tm, D_MODEL), _mod_spec(rows_per_group // tm),
                _const_spec((1, D_MODEL), (layer, 0)),
                _const_spec((1, D_MODEL), (layer, 1)),
                _const_spec((D_MODEL, D_FF)),
                _const_spec((D_MODEL, D_FF)),
                _const_spec((D_FF, D_MODEL)),
                _const_spec((D_MODEL, IN_WIDTH), (layer,)),
                _const_spec((1, ATTN_WIDTH), (layer,)),
                _const_spec((1, KV_WIDTH), (layer,)),
                _const_spec((ATTN_WIDTH, ATTN_WIDTH))]
    args = [x, mod, pw['g'], pw['g'], *pw['ffn'][(layer, 0)], pw['w_in'], pw['qg'], pw['kg'],
            tables['e']]
    if rope:
        in_specs += [pl.BlockSpec((tm, LANES), lambda i: (i % tiles_per_seq, 0))] * 3
        args += list(tables['rope'])
    widths = (D_MODEL, QM_WIDTH, KV_WIDTH, KV_WIDTH, KD_WIDTH, VD_WIDTH, FNET_WIDTH)
    dtypes = (F32, BF16, F32, F32, BF16, BF16, BF16)
    out_shape = [jax.ShapeDtypeStruct((n, w), d) for w, d in zip(widths, dtypes)]
    out_shape.append(jax.ShapeDtypeStruct((batch // SUBLANES, seq, WIDE), F32))
    out_specs = [_row_spec(tm, w) for w in widths] + [_wide_spec(tm, seq)]
    return pl.pallas_call(
        functools.partial(_ffn_in_kernel, rope=rope, seq=seq),
        out_shape=out_shape,
        grid=(n // tm,),
        in_specs=in_specs,
        out_specs=out_specs,
        compiler_params=_params("arbitrary"),
        name="ffn_in_proj",
    )(*args)


def _attend(qm_ref, o_ref, rows, sink_ref, kv, bias):
    nq = rows.stop - rows.start
    low = _low_half((nq, LANES))
    for h in range(N_KV_HEADS):
        j0 = GQA_GROUP * h
        qst = jnp.concatenate([qm_ref[rows, (j0 + g) * LANES:(j0 + g + 1) * LANES]
                               for g in range(GQA_GROUP)], axis=0)
        sink = jnp.concatenate([jnp.broadcast_to(sink_ref[j0 + g:j0 + g + 1, :], (nq, LANES))
                                for g in range(GQA_GROUP)], axis=0) * LOG2E
        kd, vd = kv(h)
        s = lax.dot_general(qst, kd, (((1,), (1,)), ((), ())), preferred_element_type=F32)
        if bias is not None:
            s = s + bias
        m = jnp.maximum(jnp.max(s, axis=-1, keepdims=True), sink)
        e = jnp.exp2(s - jnp.concatenate([m] * (s.shape[1] // LANES), axis=1)).astype(BF16)
        o = _dot(e, vd)
        es = jnp.exp2(sink - m)
        for t in range(GQA_GROUP // 2):
            r0, r1 = (2 * t) * nq, (2 * t + 1) * nq
            num = jnp.where(low, o[r0:r0 + nq, :LANES], o[r1:r1 + nq, LANES:])
            den = (jnp.where(low, o[r0:r0 + nq, LANES:], o[r1:r1 + nq, :LANES])
                   + jnp.where(low, es[r0:r0 + nq], es[r1:r1 + nq]))
            c0 = (j0 // 2 + t) * LANES
            o_ref[rows, c0:c0 + LANES] = (num * (1.0 / den)).astype(o_ref.dtype)


def _cast_specs(jobs, nsteps, step_of):
    in_specs, out_specs, out_shape = [], [], []
    for arr, layer, half in jobs:
        rows, cols = arr.shape[2:]
        rb = rows // nsteps
        in_specs.append(pl.BlockSpec((None, None, rb, cols),
                                     lambda *g, layer=layer, half=half: (layer, half, step_of(*g), 0)))
        out_specs.append(pl.BlockSpec((rb, cols), lambda *g: (step_of(*g), 0)))
        out_shape.append(jax.ShapeDtypeStruct((rows, cols), BF16))
    return in_specs, out_specs, out_shape


def _run_casts(srcs, dsts):
    for s, d in zip(srcs, dsts):
        d[...] = s[...].astype(BF16)


def _ctx_attn_kernel(qm_ref, kd_ref, vd_ref, sink_ref, *rest, seq, nseq, ncast):
    o_ref = rest[ncast]
    _run_casts(rest[:ncast], rest[ncast + 1:])
    for i in range(nseq):
        rows = slice(i * seq, (i + 1) * seq)

        def kv(h, rows=rows):
            return (kd_ref[rows, h * LANES:(h + 1) * LANES], vd_ref[rows, 2 * h * LANES:2 * (h + 1) * LANES])

        _attend(qm_ref, o_ref, rows, sink_ref, kv, None)


def _ctx_attn_call(qm, kd, vd, sink, layer, batch, seq, casts=()):
    nseq = ATTN_BLOCKS_PER_STEP
    rows = nseq * seq
    nsteps = batch // nseq
    c_in, c_out, c_shape = _cast_specs(casts, nsteps, lambda b: b)
    out = pl.pallas_call(
        functools.partial(_ctx_attn_kernel, seq=seq, nseq=nseq, ncast=len(casts)),
        out_shape=[jax.ShapeDtypeStruct((batch * seq, ATTN_WIDTH), BF16)] + c_shape,
        grid=(nsteps,),
        in_specs=[_row_spec(rows, QM_WIDTH), _row_spec(rows, KD_WIDTH), _row_spec(rows, VD_WIDTH),
                  pl.BlockSpec((None, N_HEADS, LANES), lambda b: (layer, 0, 0))] + c_in,
        out_specs=[_row_spec(rows, ATTN_WIDTH)] + c_out,
        compiler_params=_params("arbitrary"),
        name="ctx_attn",
    )(qm, kd, vd, sink, *[job[0] for job in casts])
    return out[0], out[1:]


def _lat_attn_kernel(qm_ref, kl_ref, vl_ref, kc_ref, vc_ref, bias_ref, sink_ref, *rest,
                     seq, nblk, ncast):
    o_ref = rest[ncast]
    kc_sc, vc_sc = rest[-2:]
    _run_casts(rest[:ncast], rest[ncast + 1:-2])
    j = pl.program_id(1)
    nb = seq // BLOCK

    @pl.when(j == 0)
    def _():
        kc_sc[...], vc_sc[...] = _kv_slabs(kc_ref[...], vc_ref[...])

    for i in range(nblk):
        n = j * nblk + i
        start = pl.multiple_of(jnp.clip(n * BLOCK - WINDOW, 0, seq - LOCAL_SPAN), BLOCK)
        variant = jnp.where(n == 0, 0, jnp.where(n == nb - 1, 2, 1))

        def kv(h, start=start):
            kc, vc = slice(h * LANES, (h + 1) * LANES), slice(2 * h * LANES, 2 * (h + 1) * LANES)
            return (jnp.concatenate([kc_sc[:, kc], kl_ref[pl.ds(start, LOCAL_SPAN), kc]], axis=0),
                    jnp.concatenate([vc_sc[:, vc], vl_ref[pl.ds(start, LOCAL_SPAN), vc]], axis=0))

        _attend(qm_ref, o_ref, slice(i * BLOCK, (i + 1) * BLOCK), sink_ref, kv, bias_ref[variant])


def _lat_attn_call(qm, kd, vd, kctx, vctx, bias, sink, layer, batch, seq, casts=()):
    past = kctx.shape[2]
    nblk = 2 * ATTN_BLOCKS_PER_STEP
    steps = seq // (BLOCK * nblk)
    rows = BLOCK * nblk
    c_in, c_out, c_shape = _cast_specs(casts, batch * steps, lambda b, j: b * steps + j)
    out = pl.pallas_call(
        functools.partial(_lat_attn_kernel, seq=seq, nblk=nblk, ncast=len(casts)),
        out_shape=[jax.ShapeDtypeStruct((batch * seq, ATTN_WIDTH), BF16)] + c_shape,
        grid=(batch, steps),
        in_specs=[pl.BlockSpec((rows, QM_WIDTH), lambda b, j: (b * steps + j, 0)),
                  pl.BlockSpec((seq, KD_WIDTH), lambda b, j: (b, 0)),
                  pl.BlockSpec((seq, VD_WIDTH), lambda b, j: (b, 0)),
                  pl.BlockSpec((None, None, past, KV_WIDTH), lambda b, j: (b, layer, 0, 0)),
                  pl.BlockSpec((None, None, past, KV_WIDTH), lambda b, j: (b, layer, 0, 0)),
                  pl.BlockSpec(bias.shape, lambda b, j: (0, 0, 0)),
                  pl.BlockSpec((None, N_HEADS, LANES), lambda b, j: (layer, 0, 0))] + c_in,
        out_specs=[pl.BlockSpec((rows, ATTN_WIDTH), lambda b, j: (b * steps + j, 0))] + c_out,
        scratch_shapes=[pltpu.VMEM((past, KD_WIDTH), BF16), pltpu.VMEM((past, VD_WIDTH), BF16)],
        compiler_params=_params("arbitrary", "arbitrary"),
        name="lat_attn",
    )(qm, kd, vd, kctx, vctx, bias, sink, *[job[0] for job in casts])
    return out[0], out[1:]


def _fnet_kernel(x_ref, wc_ref, dm_ref, wf_ref, o_ref, *, seq):
    for i in range(x_ref.shape[0] // seq):
        rows = slice(i * seq, (i + 1) * seq)
        y = _dot(x_ref[rows, :], wc_ref[...])
        yst = jnp.concatenate([y[:, :FNET_WIDTH], y[:, FNET_WIDTH:]], axis=0).astype(BF16)
        z = _dot(dm_ref[...], yst) * ((seq * FNET_HEAD_DIM) ** -0.5)
        o_ref[rows, :] = _dot(z.astype(BF16), wf_ref[...]).astype(o_ref.dtype)


def _fnet_call(x, wc, dm, wf, layer, batch, seq):
    rows = max(seq, TOKEN_TILE)
    return pl.pallas_call(
        functools.partial(_fnet_kernel, seq=seq),
        out_shape=jax.ShapeDtypeStruct((batch * seq, FNET_WIDTH), BF16),
        grid=(batch * seq // rows,),
        in_specs=[_row_spec(rows, FNET_WIDTH),
                  _const_spec((FNET_WIDTH, 2 * FNET_WIDTH)),
                  _const_spec((seq, 2 * seq)),
                  _const_spec((FNET_WIDTH, FNET_WIDTH), (layer,))],
        out_specs=_row_spec(rows, FNET_WIDTH),
        compiler_params=_params("arbitrary"),
        name="fourier_mix",
    )(x, wc, dm, wf)


def _s5_kernel(xf_ref, xb_ref, xfn_ref, xbn_ref, h0_ref, bm_ref, cm_ref, lam_ref, *rest,
               steps, nchunks, ncast):
    yf_ref, yb_ref, fin_ref = rest[ncast:ncast + 3]
    xs, buf0, buf1, sb, ys, carry = rest[-6:]
    _run_casts(rest[:ncast], rest[ncast + 3:-6])
    _s5_body(xf_ref, xb_ref, xfn_ref, xbn_ref, h0_ref, bm_ref, cm_ref, lam_ref,
             yf_ref, yb_ref, fin_ref, xs, buf0, buf1, sb, ys, carry, steps=steps, nchunks=nchunks)


def _s5_body(xf_ref, xb_ref, xfn_ref, xbn_ref, h0_ref, bm_ref, cm_ref, lam_ref,
             yf_ref, yb_ref, fin_ref, xs, buf0, buf1, sb, ys, carry, *, steps, nchunks):
    first = (pl.program_id(0) == 0) & (pl.program_id(1) == 0)
    c = pl.program_id(1)
    nslab = SSM_WIDTH // LANES
    w = SCAN_LANE_CHUNK

    def relayout_in(x_refs, base):
        xin = []
        for d, x_ref in enumerate(x_refs):
            for b in range(SUBLANES):
                for sl in range(nslab):
                    c0 = b * SSM_WIDTH + sl * LANES
                    xs[base + d * nslab + sl, pl.ds(b, steps, stride=SUBLANES), :] = x_ref[:, c0:c0 + LANES]
            xin.append(jnp.concatenate([xs[base + d * nslab + sl] for sl in range(nslab)],
                                       axis=1).astype(BF16))
        return xin

    def step(cur, nxt):
        xin = relayout_in((xfn_ref, xbn_ref), 2 * nslab)
        yacc = [None, None]
        for lc in range(SSM_LANES // w):
            re = slice(lc * w, (lc + 1) * w)
            im = slice(SSM_LANES + lc * w, SSM_LANES + (lc + 1) * w)
            for d in range(2):
                nxt[d, :, re] = _dot(xin[d], bm_ref[d, :, re])
                nxt[d, :, im] = _dot(xin[d], bm_ref[d, :, im])
            lam = [(lam_ref[d, :, re], lam_ref[d, :, im]) for d in range(2)]
            st = [(carry[d, :, re], carry[d, :, im]) for d in range(2)]
            prev = [None, None]
            for t in range(steps):
                for d in range(2):
                    tt = t if d == 0 else steps - 1 - t
                    rows = slice(tt * SUBLANES, (tt + 1) * SUBLANES)
                    (lr, li), (sr, si) = lam[d], st[d]
                    nr = (lr * sr - li * si) + cur[d, rows, re]
                    ni = (lr * si + li * sr) + cur[d, rows, im]
                    st[d] = (nr, ni)
                    if t % 2 == 0:
                        prev[d] = (nr, ni)
                    else:
                        lo = min(tt, tt + (1 if d == 1 else -1)) * SUBLANES
                        pair = slice(lo, lo + 2 * SUBLANES)
                        first, second = (prev[d], (nr, ni)) if d == 0 else ((nr, ni), prev[d])
                        sb[d, pair, re] = jnp.concatenate([first[0], second[0]], axis=0).astype(BF16)
                        sb[d, pair, im] = jnp.concatenate([first[1], second[1]], axis=0).astype(BF16)
            for d in range(2):
                carry[d, :, re], carry[d, :, im] = st[d]
            for d in range(2):
                part = _dot(sb[d, :, re], cm_ref[d, re, :]) + _dot(sb[d, :, im], cm_ref[d, im, :])
                yacc[d] = part if yacc[d] is None else yacc[d] + part
        for d, y_ref in enumerate((yf_ref, yb_ref)):
            for sl in range(nslab):
                ys[d * nslab + sl] = yacc[d][:, sl * LANES:(sl + 1) * LANES]
            for b in range(SUBLANES):
                for sl in range(nslab):
                    c0 = b * SSM_WIDTH + sl * LANES
                    y_ref[:, c0:c0 + LANES] = ys[d * nslab + sl, pl.ds(b, steps, stride=SUBLANES), :]

    @pl.when(c == 0)
    def _():
        carry[...] = h0_ref[0]

    @pl.when(first)
    def _():
        xin = relayout_in((xf_ref, xb_ref), 0)
        for d in range(2):
            buf0[d] = _dot(xin[d], bm_ref[d])

    @pl.when(c % 2 == 0)
    def _():
        step(buf0, buf1)

    @pl.when(c % 2 == 1)
    def _():
        step(buf1, buf0)

    @pl.when(c == nchunks - 1)
    def _():
        fin_ref[0] = carry[...]


def _s5_call(x_wide, h0, pw, layer, groups, seq, casts=()):
    steps = SCAN_STEPS
    rows = steps * SUBLANES
    nchunks = seq // steps
    assert nchunks % 2 == 0
    width = 2 * SSM_LANES
    nslab = SSM_WIDTH // LANES
    last = groups * nchunks - 1

    def nxt(g, c):
        s = jnp.minimum(g * nchunks + c + 1, last)
        return s // nchunks, s % nchunks

    def fwd_next(g, c):
        g2, c2 = nxt(g, c)
        return (g2, c2, 0)

    def bwd_next(g, c):
        g2, c2 = nxt(g, c)
        return (g2, nchunks - 1 - c2, 0)

    c_in, c_out, c_shape = _cast_specs(casts, groups * nchunks, lambda g, c: g * nchunks + c)
    out = pl.pallas_call(
        functools.partial(_s5_kernel, steps=steps, nchunks=nchunks, ncast=len(casts)),
        out_shape=[jax.ShapeDtypeStruct((groups, seq, WIDE), F32),
                   jax.ShapeDtypeStruct((groups, seq, WIDE), F32),
                   jax.ShapeDtypeStruct((groups, 2, SUBLANES, width), F32)] + c_shape,
        grid=(groups, nchunks),
        in_specs=[pl.BlockSpec((None, steps, WIDE), lambda g, c: (0, 0, 0)),
                  pl.BlockSpec((None, steps, WIDE), lambda g, c: (0, nchunks - 1, 0)),
                  pl.BlockSpec((None, steps, WIDE), fwd_next),
                  pl.BlockSpec((None, steps, WIDE), bwd_next),
                  pl.BlockSpec((1, 2, SUBLANES, width), lambda g, c: (g, 0, 0, 0)),
                  pl.BlockSpec((None, 2, SSM_WIDTH, width), lambda g, c: (layer, 0, 0, 0)),
                  pl.BlockSpec((None, 2, width, SSM_WIDTH), lambda g, c: (layer, 0, 0, 0)),
                  pl.BlockSpec((None, 2, SUBLANES, width), lambda g, c: (layer, 0, 0, 0))] + c_in,
        out_specs=[pl.BlockSpec((None, steps, WIDE), lambda g, c: (g, c, 0)),
                   pl.BlockSpec((None, steps, WIDE), lambda g, c: (g, nchunks - 1 - c, 0)),
                   pl.BlockSpec((1, 2, SUBLANES, width), lambda g, c: (g, 0, 0, 0))] + c_out,
        scratch_shapes=[pltpu.VMEM((4 * nslab, rows, LANES), F32),
                        pltpu.VMEM((2, rows, width), F32),
                        pltpu.VMEM((2, rows, width), F32),
                        pltpu.VMEM((2, rows, width), BF16),
                        pltpu.VMEM((2 * nslab, rows, LANES), F32),
                        pltpu.VMEM((2, SUBLANES, width), F32)],
        compiler_params=_params("arbitrary", "arbitrary"),
        name="s5_scan",
    )(x_wide, x_wide, x_wide, x_wide, h0, pw['bm'], pw['cm'], pw['lam'], *[job[0] for job in casts])
    return out[0], out[1], out[2], out[3:]


def _out_ffn_kernel(x_ref, mod_ref, a_ref, f_ref, us_ref, yf_ref, yb_ref, d_ref, wg_ref, wo_ref,
                    g_ref, w1_ref, w3_ref, w2_ref, o_ref, *, seq):
    m = mod_ref[0]
    us, yf, yb = (_wide_load(r, seq) for r in (us_ref, yf_ref, yb_ref))
    ys = jax.nn.gelu((d_ref[...] * us + yf) + yb)
    gl = _dot(ys.astype(BF16), wg_ref[...])
    so = gl[:, :SSM_WIDTH] * jax.nn.sigmoid(gl[:, SSM_WIDTH:])
    mixed = jnp.concatenate([a_ref[...], f_ref[...], so.astype(BF16)], axis=1)
    x2 = x_ref[...] + m[5:6] * _dot(mixed, wo_ref[...])
    o_ref[...] = _half_ffn(x2, m, 6, g_ref[...], w1_ref, w3_ref, w2_ref)


def _out_ffn_call(x, mod, a, f, us, yf, yb, pw, layer, seq, rows_per_group):
    n = x.shape[0]
    tm = _token_tile(seq)
    wide = _wide_spec(tm, seq)
    return pl.pallas_call(
        functools.partial(_out_ffn_kernel, seq=seq),
        out_shape=jax.ShapeDtypeStruct((n, D_MODEL), F32),
        grid=(n // tm,),
        in_specs=[_row_spec(tm, D_MODEL), _mod_spec(rows_per_group // tm),
                  _row_spec(tm, ATTN_WIDTH), _row_spec(tm, FNET_WIDTH), wide, wide, wide,
                  _const_spec((1, SSM_WIDTH), (layer,)),
                  _const_spec((SSM_WIDTH, 2 * SSM_WIDTH), (layer,)),
                  _const_spec((MIX_WIDTH, D_MODEL), (layer,)),
                  _const_spec((1, D_MODEL), (layer, 2)),
                  _const_spec((D_MODEL, D_FF)),
                  _const_spec((D_MODEL, D_FF)),
                  _const_spec((D_FF, D_MODEL))],
        out_specs=_row_spec(tm, D_MODEL),
        compiler_params=_params("arbitrary"),
        name="out_proj_ffn",
    )(x, mod, a, f, us, yf, yb, pw['d'], pw['w_glu'], pw['w_out'], pw['g'], *pw['ffn'][(layer, 1)])


def _rope_tables(seq):
    pos = np.arange(seq)
    row, col = pos // GRID_W, pos % GRID_W
    quarter = HEAD_DIM // 4
    inv_freq = 1.0 / (ROPE_BASE ** (np.arange(quarter, dtype=np.float64) * 2.0 / (HEAD_DIM // 2)))
    lane = np.arange(LANES)
    in_head = lane % HEAD_DIM
    p = np.where((in_head < HEAD_DIM // 2)[None, :], row[:, None], col[:, None]).astype(np.float64)
    ang = p * inv_freq[lane % quarter][None, :]
    first = ((lane % (HEAD_DIM // 2)) < quarter)[None, :]
    cos = np.cos(ang)
    sa = np.where(first, -np.sin(ang), 0.0)
    sb = np.where(first, 0.0, np.sin(ang))
    return tuple(jnp.asarray(t, dtype=F32) for t in (cos, sa, sb))


def _dft_tables(seq):
    kl = np.outer(np.arange(seq), np.arange(seq)) % seq
    ang = 2.0 * np.pi * kl / seq
    dm = np.concatenate([np.cos(ang), -np.sin(ang)], axis=1)
    mc = np.outer(np.arange(FNET_HEAD_DIM), np.arange(FNET_HEAD_DIM)) % FNET_HEAD_DIM
    a64 = 2.0 * np.pi * mc / FNET_HEAD_DIM
    eye = np.eye(FNET_HEADS)
    wc = np.concatenate([np.kron(eye, np.cos(a64)), np.kron(eye, np.sin(a64))], axis=1)
    return jnp.asarray(dm, dtype=F32).astype(BF16), jnp.asarray(wc, dtype=F32).astype(BF16)


def _head_mean_matrix():
    return jnp.asarray(np.kron(np.eye(N_HEADS), np.full((HEAD_DIM, HEAD_DIM), 1.0 / HEAD_DIM)), dtype=BF16)


def _window_bias(past):
    i = np.arange(GQA_GROUP * BLOCK)[:, None] % BLOCK
    j = np.arange(LOCAL_SPAN)[None, :]
    out = np.zeros((3, GQA_GROUP * BLOCK, past + LOCAL_SPAN), np.float32)
    for v, off in enumerate((0, WINDOW, 2 * WINDOW)):
        out[v, :, past:] = np.where(np.abs(j - off - i) <= WINDOW, 0.0, NEG_INF)
    return jnp.asarray(out)


def _s5_matrices(lam_re, lam_im, b_re, b_im, c_re, c_im, log_step):
    step = jnp.exp(log_step)[..., None]
    mag = jnp.exp(lam_re * step)
    lr = mag * jnp.cos(lam_im * step)
    li = mag * jnp.sin(lam_im * step)
    den = lam_re * lam_re + lam_im * lam_im
    cr = ((lr - 1.0) * lam_re + li * lam_im) / den
    ci = (li * lam_re - (lr - 1.0) * lam_im) / den
    bbr = cr[..., None] * b_re - ci[..., None] * b_im
    bbi = cr[..., None] * b_im + ci[..., None] * b_re
    same_group = jnp.asarray(np.arange(SSM_WIDTH)[:, None] // SSM_GROUP
                             == np.arange(SSM_LANES)[None, :] // SSM_STATE, dtype=F32)

    def in_mat(b):
        rows = b.transpose(0, 1, 4, 2, 3).reshape(DEPTH, 2, SSM_GROUP, SSM_LANES)
        return jnp.tile(rows, (1, 1, SSM_GROUPS, 1)) * same_group

    def out_mat(cw):
        rows = cw.transpose(0, 1, 4, 2, 3).reshape(DEPTH, 2, SSM_STATE, SSM_WIDTH)
        return jnp.tile(rows, (1, 1, SSM_GROUPS, 1)) * same_group.T

    bm = jnp.concatenate([in_mat(bbr), in_mat(bbi)], axis=-1).astype(BF16)
    cm = jnp.concatenate([out_mat(c_re), -out_mat(c_im)], axis=-2).astype(BF16)
    lrow = jnp.concatenate([lr.reshape(DEPTH, 2, SSM_LANES), li.reshape(DEPTH, 2, SSM_LANES)], axis=-1)
    lam = jnp.broadcast_to(lrow[:, :, None, :], (DEPTH, 2, SUBLANES, 2 * SSM_LANES))
    return bm, cm, lam


def _stream_layer(x, mod, pw, layer, batch, seq, rows_per_group, tables, ctx_kv, h0):
    latent = ctx_kv is not None
    groups = batch // SUBLANES
    x1, qm, k, v, kd, vd, f, us = _ffn_in_call(x, mod, pw, layer, tables, latent, batch, seq, rows_per_group)
    pending = [key for key in ([(0, 1)] if not latent else [(l, h) for l in range(1, DEPTH) for h in range(2)])
               if key not in pw['ffn']]
    casts = [(w, l, h) for (l, h) in pending for w in pw['ffn_f32']]
    if latent:
        a, cast = _lat_attn_call(qm, kd, vd, ctx_kv[0], ctx_kv[1], tables['bias'], pw['sink'], layer,
                                 batch, seq, casts)
    else:
        a, _ = _ctx_attn_call(qm, kd, vd, pw['sink'], layer, batch, seq)
    dm, wc = tables['dft'][seq]
    fz = _fnet_call(f, wc, dm, pw['w_fnet'], layer, batch, seq)
    yf, yb, fin, s5_cast = _s5_call(us, h0, pw, layer, groups, seq, () if latent else casts)
    if not latent:
        cast = s5_cast
    for i, key in enumerate(pending):
        pw['ffn'][key] = tuple(cast[3 * i:3 * i + 3])
    x = _out_ffn_call(x1, mod, a, fz, us, yf, yb, pw, layer, seq, rows_per_group)
    return x, (k, v, fin)


def kernel(x_prompt, x_sample, cache_k, cache_v, state_ssm_re, state_ssm_im, c, c_ctx, w_mod, b_mod, norm_g, ffn_w1, ffn_w3, ffn_w2, w_in, w_out, q_norm_g, k_norm_g, attn_sink, w_fnet, ssm_lambda_re, ssm_lambda_im, ssm_b_re, ssm_b_im, ssm_c_re, ssm_c_im, ssm_d, ssm_log_step, ssm_w_glu):
    batch, seq, _ = x_prompt.shape
    dec_batch, dec_seq, _ = x_sample.shape
    past = cache_k.shape[2]

    tables = {'e': _head_mean_matrix(), 'rope': _rope_tables(dec_seq), 'bias': _window_bias(past),
              'dft': {s: _dft_tables(s) for s in {seq, dec_seq}}}

    bm, cm, lam = _s5_matrices(ssm_lambda_re, ssm_lambda_im, ssm_b_re, ssm_b_im, ssm_c_re, ssm_c_im,
                               ssm_log_step)
    pw = {'g': norm_g.reshape(DEPTH, 3, 1, D_MODEL),
          'ffn_f32': (ffn_w1, ffn_w3, ffn_w2),
          'ffn': {(0, 0): tuple(w[0, 0].astype(BF16) for w in (ffn_w1, ffn_w3, ffn_w2))},
          'w_in': w_in.astype(BF16), 'w_out': w_out.astype(BF16),
          'qg': jnp.tile(q_norm_g, (1, N_HEADS)).reshape(DEPTH, 1, ATTN_WIDTH),
          'kg': jnp.tile(k_norm_g, (1, N_KV_HEADS)).reshape(DEPTH, 1, KV_WIDTH),
          'sink': jnp.broadcast_to(attn_sink[:, :, None], (DEPTH, N_HEADS, LANES)),
          'w_fnet': w_fnet.astype(BF16), 'bm': bm, 'cm': cm, 'lam': lam,
          'd': ssm_d.reshape(DEPTH, 1, SSM_WIDTH), 'w_glu': ssm_w_glu.astype(BF16)}

    cond = jnp.zeros((COND_ROWS, D_MODEL), F32).at[0].set(c_ctx).at[1:1 + dec_batch].set(c)
    mods = _mod_call(cond, w_mod, b_mod).reshape(DEPTH, COND_ROWS, N_MOD, D_MODEL)

    ctx_kv = (cache_k.reshape(dec_batch, DEPTH, past, KV_WIDTH), cache_v.reshape(dec_batch, DEPTH, past, KV_WIDTH))
    h0_lat = jnp.concatenate([state_ssm_re.reshape(dec_batch, DEPTH, 2, SSM_LANES),
                              state_ssm_im.reshape(dec_batch, DEPTH, 2, SSM_LANES)], axis=-1)
    h0_lat = h0_lat.reshape(dec_batch // SUBLANES, SUBLANES, DEPTH, 2, 2 * SSM_LANES).transpose(2, 0, 3, 1, 4)
    h0_ctx = jnp.zeros((batch // SUBLANES, 2, SUBLANES, 2 * SSM_LANES), F32)

    yp = x_prompt.reshape(batch * seq, D_MODEL)
    ys = x_sample.reshape(dec_batch * dec_seq, D_MODEL)
    ks, vs, sre, sim = [], [], [], []
    for l in range(DEPTH):
        yp, (k_l, v_l, fin) = _stream_layer(yp, mods[l, 0:1], pw, l, batch, seq, batch * seq, tables,
                                            None, h0_ctx)
        ks.append(k_l.reshape(batch, seq, N_KV_HEADS, HEAD_DIM))
        vs.append(v_l.reshape(batch, seq, N_KV_HEADS, HEAD_DIM))
        fin = fin.reshape(batch // SUBLANES, 2, SUBLANES, 2, SSM_GROUPS, SSM_STATE)
        fin = fin.transpose(3, 0, 2, 1, 4, 5).reshape(2, batch, 2, SSM_GROUPS, SSM_STATE)
        sre.append(fin[0])
        sim.append(fin[1])
        ys, _ = _stream_layer(ys, mods[l, 1:1 + dec_batch], pw, l, dec_batch, dec_seq, dec_seq, tables,
                              ctx_kv, h0_lat[l])

    return (yp.reshape(batch, seq, D_MODEL), ys.reshape(dec_batch, dec_seq, D_MODEL),
            jnp.stack(ks, axis=1), jnp.stack(vs, axis=1), jnp.stack(sre, axis=1), jnp.stack(sim, axis=1))
```

```python
import functools
import math

import numpy as np
import jax
import jax.numpy as jnp
from jax import lax
from jax.experimental import pallas as pl
from jax.experimental.pallas import tpu as pltpu

F32 = jnp.float32
BF16 = jnp.bfloat16

D_MODEL = 1024
DEPTH = 2
GRID_W = 64
HEAD_DIM = 64
N_HEADS = 8
N_KV_HEADS = 2
GQA_GROUP = N_HEADS // N_KV_HEADS
ATTN_WIDTH = N_HEADS * HEAD_DIM
KV_WIDTH = N_KV_HEADS * HEAD_DIM
WINDOW = 128
BLOCK = 128
ATTN_SCALE = HEAD_DIM ** -0.5
ROPE_BASE = 10000.0
NEG_INF = -1e30
LOG2E = math.log2(math.e)
FNET_HEADS = 4
FNET_HEAD_DIM = 64
FNET_WIDTH = FNET_HEADS * FNET_HEAD_DIM
SSM_WIDTH = 256
SSM_GROUP = 16
SSM_GROUPS = SSM_WIDTH // SSM_GROUP
SSM_STATE = 64
SSM_LANES = SSM_GROUPS * SSM_STATE
MIX_WIDTH = ATTN_WIDTH + FNET_WIDTH + SSM_WIDTH
Q_END = ATTN_WIDTH
K_END = Q_END + KV_WIDTH
V_END = K_END + KV_WIDTH
F_END = V_END + FNET_WIDTH
S_END = F_END + SSM_WIDTH
IN_WIDTH = S_END
D_FF = 2816
N_MOD = 9
NORM_EPS = 1e-6

LANES = 128
SUBLANES = 8
VMEM_LIMIT_BYTES = 60 * 1024 * 1024
TOKEN_TILE = 1024
FF_CHUNKS = 11
FF_CHUNK = D_FF // FF_CHUNKS
ATTN_BLOCKS_PER_STEP = 4
SCAN_STEPS = 64
SCAN_LANE_CHUNK = 256
COND_ROWS = 16
LOCAL_SPAN = BLOCK + 2 * WINDOW
QM_WIDTH = N_HEADS * LANES
KD_WIDTH = N_KV_HEADS * LANES
VD_WIDTH = N_KV_HEADS * 2 * LANES
WIDE = SUBLANES * SSM_WIDTH


def _params(*sem):
    return pltpu.CompilerParams(dimension_semantics=sem, vmem_limit_bytes=VMEM_LIMIT_BYTES)


def _dot(a, b):
    return jnp.dot(a, b, preferred_element_type=F32)


def _norm_mod(x, g, shift, scale):
    y = x * lax.rsqrt(jnp.mean(x * x, axis=-1, keepdims=True) + NORM_EPS)
    return (y * g) * (1.0 + scale) + shift


def _const_spec(shape, lead=()):
    nd = len(shape)
    idx = tuple(lead) + (0,) * nd
    return pl.BlockSpec((None,) * len(lead) + tuple(shape), lambda *_: idx,
                        pipeline_mode=pl.Buffered(1))


def _row_spec(tm, width):
    return pl.BlockSpec((tm, width), lambda i: (i, 0))


def _mod_spec(tiles_per_group):
    return pl.BlockSpec((1, N_MOD, D_MODEL), lambda i: (i // tiles_per_group, 0, 0))


def _wide_spec(tm, seq):
    if tm <= seq:
        tiles_per_seq = seq // tm

        def index(i):
            b = i // tiles_per_seq
            return (b // SUBLANES, i % tiles_per_seq, b % SUBLANES)
        return pl.BlockSpec((None, tm, SSM_WIDTH), index)
    nseq = tm // seq

    def index(i):
        b = i * nseq
        return (b // SUBLANES, 0, (b % SUBLANES) // nseq)
    return pl.BlockSpec((None, seq, nseq * SSM_WIDTH), index)


def _wide_store(ref, x, seq):
    rows = min(seq, x.shape[0])
    for j in range(x.shape[0] // rows):
        ref[:, j * SSM_WIDTH:(j + 1) * SSM_WIDTH] = x[j * rows:(j + 1) * rows]


def _wide_load(ref, seq):
    ncol = ref.shape[1] // SSM_WIDTH
    if ncol == 1:
        return ref[...]
    return jnp.concatenate([ref[:, j * SSM_WIDTH:(j + 1) * SSM_WIDTH] for j in range(ncol)], axis=0)


def _mod_kernel(c_ref, w_ref, b_ref, o_ref):
    c = c_ref[...]
    a = (c * jax.nn.sigmoid(c)).astype(BF16)
    o_ref[0] = _dot(a, w_ref[0].astype(BF16)) + b_ref[0]


def _mod_call(cond, w_mod, b_mod):
    tn = N_MOD * D_MODEL // 4
    return pl.pallas_call(
        _mod_kernel,
        out_shape=jax.ShapeDtypeStruct((DEPTH, COND_ROWS, N_MOD * D_MODEL), F32),
        grid=(DEPTH, N_MOD * D_MODEL // tn),
        in_specs=[pl.BlockSpec((COND_ROWS, D_MODEL), lambda l, j: (0, 0)),
                  pl.BlockSpec((1, D_MODEL, tn), lambda l, j: (l, 0, j)),
                  pl.BlockSpec((1, 1, tn), lambda l, j: (l, 0, j))],
        out_specs=pl.BlockSpec((1, COND_ROWS, tn), lambda l, j: (l, 0, j)),
        compiler_params=_params("arbitrary", "arbitrary"),
        name="cond_mod",
    )(cond, w_mod, b_mod.reshape(DEPTH, 1, N_MOD * D_MODEL))


def _half_ffn(x, m, k0, g, w1_ref, w3_ref, w2_ref):
    h = _norm_mod(x, g, m[k0:k0 + 1], m[k0 + 1:k0 + 2]).astype(BF16)
    y = None
    for c in range(FF_CHUNKS):
        cols = slice(c * FF_CHUNK, (c + 1) * FF_CHUNK)
        a = _dot(h, w1_ref[:, cols])
        b = _dot(h, w3_ref[:, cols])
        t = ((a * jax.nn.sigmoid(a)) * b).astype(BF16)
        yc = _dot(t, w2_ref[cols, :])
        y = yc if y is None else y + yc
    return x + (0.5 * m[k0 + 2:k0 + 3]) * y


def _head_norm(z, e, g):
    ms = _dot((z * z).astype(BF16), e)
    return (z * lax.rsqrt(ms + NORM_EPS)) * g


def _low_half(shape):
    return lax.broadcasted_iota(jnp.int32, shape, 1) < HEAD_DIM


def _kv_slabs(k, v):
    low = _low_half(k.shape)
    kr = pltpu.roll(k, HEAD_DIM, 1)
    vr = pltpu.roll(v, HEAD_DIM, 1)
    kd = jnp.concatenate([jnp.where(low, k, kr), jnp.where(low, kr, k)], axis=1)
    vd = jnp.concatenate([jnp.where(low, v, 1.0), jnp.where(low, 1.0, vr),
                          jnp.where(low, vr, 1.0), jnp.where(low, 1.0, v)], axis=1)
    return kd.astype(BF16), vd.astype(BF16)


def _ffn_in_kernel(*refs, rope, seq):
    if rope:
        (x_ref, mod_ref, g0_ref, g1_ref, w1_ref, w3_ref, w2_ref, w_ref, qg_ref, kg_ref, e_ref,
         cos_ref, sa_ref, sb_ref, x1_ref, qm_ref, k_ref, v_ref, kd_ref, vd_ref, f_ref, s_ref) = refs
    else:
        (x_ref, mod_ref, g0_ref, g1_ref, w1_ref, w3_ref, w2_ref, w_ref, qg_ref, kg_ref, e_ref,
         x1_ref, qm_ref, k_ref, v_ref, kd_ref, vd_ref, f_ref, s_ref) = refs
    m = mod_ref[0]
    x1 = _half_ffn(x_ref[...], m, 0, g0_ref[...], w1_ref, w3_ref, w2_ref)
    x1_ref[...] = x1
    h = _norm_mod(x1, g1_ref[...], m[3:4], m[4:5]).astype(BF16)
    u = _dot(h, w_ref[...])
    q = _head_norm(u[:, :Q_END], e_ref[...], qg_ref[...])
    k = _head_norm(u[:, Q_END:K_END], e_ref[0:KV_WIDTH, 0:KV_WIDTH], kg_ref[...])
    v = u[:, K_END:V_END]
    if rope:
        cos, sa, sb = cos_ref[...], sa_ref[...], sb_ref[...]

        def rot(z):
            return z * cos + pltpu.roll(z, LANES - 16, 1) * sa + pltpu.roll(z, 16, 1) * sb
    else:
        def rot(z):
            return z

    low = _low_half((q.shape[0], LANES))
    for i in range(ATTN_WIDTH // LANES):
        qs = rot(q[:, i * LANES:(i + 1) * LANES]) * (ATTN_SCALE * LOG2E)
        qm_ref[:, (2 * i) * LANES:(2 * i + 1) * LANES] = jnp.where(low, qs, 0.0).astype(BF16)
        qm_ref[:, (2 * i + 1) * LANES:(2 * i + 2) * LANES] = jnp.where(low, 0.0, qs).astype(BF16)
    k = rot(k)
    k_ref[...] = k
    v_ref[...] = v
    kd_ref[...], vd_ref[...] = _kv_slabs(k, v)
    f_ref[...] = u[:, V_END:F_END].astype(BF16)
    _wide_store(s_ref, u[:, F_END:S_END], seq)


def _token_tile(seq):
    return TOKEN_TILE if TOKEN_TILE <= seq else min(TOKEN_TILE, SUBLANES * seq // 2)


def _ffn_in_call(x, mod, pw, layer, tables, rope, batch, seq, rows_per_group):
    n = x.shape[0]
    tm = _token_tile(seq)
    tiles_per_seq = max(seq // tm, 1)
    in_specs = [_row_spec(tm, D_MODEL), _mod_spec(rows_per_group // tm),
                _const_spec((1, D_MODEL), (layer, 0)),
                _const_spec((1, D_MODEL), (layer, 1)),
                _const_spec((D_MODEL, D_FF)),
                _const_spec((D_MODEL, D_FF)),
                _const_spec((D_FF, D_MODEL)),
                _const_spec((D_MODEL, IN_WIDTH), (layer,)),
                _const_spec((1, ATTN_WIDTH), (layer,)),
                _const_spec((1, KV_WIDTH), (layer,)),
                _const_spec((ATTN_WIDTH, ATTN_WIDTH))]
    args = [x, mod, pw['g'], pw['g'], *pw['ffn'][(layer, 0)], pw['w_in'], pw['qg'], pw['kg'],
            tables['e']]
    if rope:
        in_specs += [pl.BlockSpec((tm, LANES), lambda i: (i % tiles_per_seq, 0))] * 3
        args += list(tables['rope'])
    widths = (D_MODEL, QM_WIDTH, KV_WIDTH, KV_WIDTH, KD_WIDTH, VD_WIDTH, FNET_WIDTH)
    dtypes = (F32, BF16, F32, F32, BF16, BF16, BF16)
    out_shape = [jax.ShapeDtypeStruct((n, w), d) for w, d in zip(widths, dtypes)]
    out_shape.append(jax.ShapeDtypeStruct((batch // SUBLANES, seq, WIDE), F32))
    out_specs = [_row_spec(tm, w) for w in widths] + [_wide_spec(tm, seq)]
    return pl.pallas_call(
        functools.partial(_ffn_in_kernel, rope=rope, seq=seq),
        out_shape=out_shape,
        grid=(n // tm,),
        in_specs=in_specs,
        out_specs=out_specs,
        compiler_params=_params("arbitrary"),
        name="ffn_in_proj",
    )(*args)


def _attend(qm_ref, o_ref, rows, sink_ref, kv, bias):
    nq = rows.stop - rows.start
    low = _low_half((nq, LANES))
    for h in range(N_KV_HEADS):
        j0 = GQA_GROUP * h
        qst = jnp.concatenate([qm_ref[rows, (j0 + g) * LANES:(j0 + g + 1) * LANES]
                               for g in range(GQA_GROUP)], axis=0)
        sink = jnp.concatenate([jnp.broadcast_to(sink_ref[j0 + g:j0 + g + 1, :], (nq, LANES))
                                for g in range(GQA_GROUP)], axis=0) * LOG2E
        kd, vd = kv(h)
        s = lax.dot_general(qst, kd, (((1,), (1,)), ((), ())), preferred_element_type=F32)
        if bias is not None:
            s = s + bias
        m = jnp.maximum(jnp.max(s, axis=-1, keepdims=True), sink)
        e = jnp.exp2(s - jnp.concatenate([m] * (s.shape[1] // LANES), axis=1)).astype(BF16)
        o = _dot(e, vd)
        es = jnp.exp2(sink - m)
        for t in range(GQA_GROUP // 2):
            r0, r1 = (2 * t) * nq, (2 * t + 1) * nq
            num = jnp.where(low, o[r0:r0 + nq, :LANES], o[r1:r1 + nq, LANES:])
            den = (jnp.where(low, o[r0:r0 + nq, LANES:], o[r1:r1 + nq, :LANES])
                   + jnp.where(low, es[r0:r0 + nq], es[r1:r1 + nq]))
            c0 = (j0 // 2 + t) * LANES
            o_ref[rows, c0:c0 + LANES] = (num * (1.0 / den)).astype(o_ref.dtype)


def _cast_specs(jobs, nsteps, step_of):
    in_specs, out_specs, out_shape = [], [], []
    for arr, layer, half in jobs:
        rows, cols = arr.shape[2:]
        rb = rows // nsteps
        in_specs.append(pl.BlockSpec((None, None, rb, cols),
                                     lambda *g, layer=layer, half=half: (layer, half, step_of(*g), 0)))
        out_specs.append(pl.BlockSpec((rb, cols), lambda *g: (step_of(*g), 0)))
        out_shape.append(jax.ShapeDtypeStruct((rows, cols), BF16))
    return in_specs, out_specs, out_shape


def _run_casts(srcs, dsts):
    for s, d in zip(srcs, dsts):
        d[...] = s[...].astype(BF16)


def _ctx_attn_kernel(qm_ref, kd_ref, vd_ref, sink_ref, *rest, seq, nseq, ncast):
    o_ref = rest[ncast]
    _run_casts(rest[:ncast], rest[ncast + 1:])
    for i in range(nseq):
        rows = slice(i * seq, (i + 1) * seq)

        def kv(h, rows=rows):
            return (kd_ref[rows, h * LANES:(h + 1) * LANES], vd_ref[rows, 2 * h * LANES:2 * (h + 1) * LANES])

        _attend(qm_ref, o_ref, rows, sink_ref, kv, None)


def _ctx_attn_call(qm, kd, vd, sink, layer, batch, seq, casts=()):
    nseq = 2 * ATTN_BLOCKS_PER_STEP
    rows = nseq * seq
    nsteps = batch // nseq
    c_in, c_out, c_shape = _cast_specs(casts, nsteps, lambda b: b)
    out = pl.pallas_call(
        functools.partial(_ctx_attn_kernel, seq=seq, nseq=nseq, ncast=len(casts)),
        out_shape=[jax.ShapeDtypeStruct((batch * seq, ATTN_WIDTH), BF16)] + c_shape,
        grid=(nsteps,),
        in_specs=[_row_spec(rows, QM_WIDTH), _row_spec(rows, KD_WIDTH), _row_spec(rows, VD_WIDTH),
                  pl.BlockSpec((None, N_HEADS, LANES), lambda b: (layer, 0, 0))] + c_in,
        out_specs=[_row_spec(rows, ATTN_WIDTH)] + c_out,
        compiler_params=_params("arbitrary"),
        name="ctx_attn",
    )(qm, kd, vd, sink, *[job[0] for job in casts])
    return out[0], out[1:]


def _lat_attn_kernel(qm_ref, kl_ref, vl_ref, kc_ref, vc_ref, bias_ref, sink_ref, *rest,
                     seq, nblk, ncast):
    o_ref = rest[ncast]
    kc_sc, vc_sc = rest[-2:]
    _run_casts(rest[:ncast], rest[ncast + 1:-2])
    j = pl.program_id(1)
    nb = seq // BLOCK

    @pl.when(j == 0)
    def _():
        kc_sc[...], vc_sc[...] = _kv_slabs(kc_ref[...], vc_ref[...])

    for i in range(nblk):
        n = j * nblk + i
        start = pl.multiple_of(jnp.clip(n * BLOCK - WINDOW, 0, seq - LOCAL_SPAN), BLOCK)
        variant = jnp.where(n == 0, 0, jnp.where(n == nb - 1, 2, 1))

        def kv(h, start=start):
            kc, vc = slice(h * LANES, (h + 1) * LANES), slice(2 * h * LANES, 2 * (h + 1) * LANES)
            return (jnp.concatenate([kc_sc[:, kc], kl_ref[pl.ds(start, LOCAL_SPAN), kc]], axis=0),
                    jnp.concatenate([vc_sc[:, vc], vl_ref[pl.ds(start, LOCAL_SPAN), vc]], axis=0))

        _attend(qm_ref, o_ref, slice(i * BLOCK, (i + 1) * BLOCK), sink_ref, kv, bias_ref[variant])


def _lat_attn_call(qm, kd, vd, kctx, vctx, bias, sink, layer, batch, seq, casts=()):
    past = kctx.shape[2]
    nblk = 2 * ATTN_BLOCKS_PER_STEP
    steps = seq // (BLOCK * nblk)
    rows = BLOCK * nblk
    c_in, c_out, c_shape = _cast_specs(casts, batch * steps, lambda b, j: b * steps + j)
    out = pl.pallas_call(
        functools.partial(_lat_attn_kernel, seq=seq, nblk=nblk, ncast=len(casts)),
        out_shape=[jax.ShapeDtypeStruct((batch * seq, ATTN_WIDTH), BF16)] + c_shape,
        grid=(batch, steps),
        in_specs=[pl.BlockSpec((rows, QM_WIDTH), lambda b, j: (b * steps + j, 0)),
                  pl.BlockSpec((seq, KD_WIDTH), lambda b, j: (b, 0)),
                  pl.BlockSpec((seq, VD_WIDTH), lambda b, j: (b, 0)),
                  pl.BlockSpec((None, None, past, KV_WIDTH), lambda b, j: (b, layer, 0, 0)),
                  pl.BlockSpec((None, None, past, KV_WIDTH), lambda b, j: (b, layer, 0, 0)),
                  pl.BlockSpec(bias.shape, lambda b, j: (0, 0, 0)),
                  pl.BlockSpec((None, N_HEADS, LANES), lambda b, j: (layer, 0, 0))] + c_in,
        out_specs=[pl.BlockSpec((rows, ATTN_WIDTH), lambda b, j: (b * steps + j, 0))] + c_out,
        scratch_shapes=[pltpu.VMEM((past, KD_WIDTH), BF16), pltpu.VMEM((past, VD_WIDTH), BF16)],
        compiler_params=_params("arbitrary", "arbitrary"),
        name="lat_attn",
    )(qm, kd, vd, kctx, vctx, bias, sink, *[job[0] for job in casts])
    return out[0], out[1:]


def _fnet_kernel(x_ref, wc_ref, dm_ref, wf_ref, o_ref, *, seq):
    for i in range(x_ref.shape[0] // seq):
        rows = slice(i * seq, (i + 1) * seq)
        y = _dot(x_ref[rows, :], wc_ref[...])
        yst = jnp.concatenate([y[:, :FNET_WIDTH], y[:, FNET_WIDTH:]], axis=0).astype(BF16)
        z = _dot(dm_ref[...], yst) * ((seq * FNET_HEAD_DIM) ** -0.5)
        o_ref[rows, :] = _dot(z.astype(BF16), wf_ref[...]).astype(o_ref.dtype)


def _fnet_call(x, wc, dm, wf, layer, batch, seq):
    rows = max(seq, TOKEN_TILE)
    return pl.pallas_call(
        functools.partial(_fnet_kernel, seq=seq),
        out_shape=jax.ShapeDtypeStruct((batch * seq, FNET_WIDTH), BF16),
        grid=(batch * seq // rows,),
        in_specs=[_row_spec(rows, FNET_WIDTH),
                  _const_spec((FNET_WIDTH, 2 * FNET_WIDTH)),
                  _const_spec((seq, 2 * seq)),
                  _const_spec((FNET_WIDTH, FNET_WIDTH), (layer,))],
        out_specs=_row_spec(rows, FNET_WIDTH),
        compiler_params=_params("arbitrary"),
        name="fourier_mix",
    )(x, wc, dm, wf)


def _s5_kernel(xf_ref, xb_ref, xfn_ref, xbn_ref, h0_ref, bm_ref, cm_ref, lam_ref, *rest,
               steps, nchunks, ncast):
    yf_ref, yb_ref, fin_ref = rest[ncast:ncast + 3]
    xs, buf0, buf1, sb, ys, carry = rest[-6:]
    _run_casts(rest[:ncast], rest[ncast + 3:-6])
    _s5_body(xf_ref, xb_ref, xfn_ref, xbn_ref, h0_ref, bm_ref, cm_ref, lam_ref,
             yf_ref, yb_ref, fin_ref, xs, buf0, buf1, sb, ys, carry, steps=steps, nchunks=nchunks)


def _s5_body(xf_ref, xb_ref, xfn_ref, xbn_ref, h0_ref, bm_ref, cm_ref, lam_ref,
             yf_ref, yb_ref, fin_ref, xs, buf0, buf1, sb, ys, carry, *, steps, nchunks):
    first = (pl.program_id(0) == 0) & (pl.program_id(1) == 0)
    c = pl.program_id(1)
    nslab = SSM_WIDTH // LANES
    w = SCAN_LANE_CHUNK

    def relayout_in(x_refs, base):
        xin = []
        for d, x_ref in enumerate(x_refs):
            for b in range(SUBLANES):
                for sl in range(nslab):
                    c0 = b * SSM_WIDTH + sl * LANES
                    xs[base + d * nslab + sl, pl.ds(b, steps, stride=SUBLANES), :] = x_ref[:, c0:c0 + LANES]
            xin.append(jnp.concatenate([xs[base + d * nslab + sl] for sl in range(nslab)],
                                       axis=1).astype(BF16))
        return xin

    def step(cur, nxt):
        xin = relayout_in((xfn_ref, xbn_ref), 2 * nslab)
        yacc = [None, None]
        for lc in range(SSM_LANES // w):
            re = slice(lc * w, (lc + 1) * w)
            im = slice(SSM_LANES + lc * w, SSM_LANES + (lc + 1) * w)
            for d in range(2):
                nxt[d, :, re] = _dot(xin[d], bm_ref[d, :, re])
                nxt[d, :, im] = _dot(xin[d], bm_ref[d, :, im])
            lam = [(lam_ref[d, :, re], lam_ref[d, :, im]) for d in range(2)]
            st = [(carry[d, :, re], carry[d, :, im]) for d in range(2)]
            prev = [None, None]
            for t in range(steps):
                for d in range(2):
                    tt = t if d == 0 else steps - 1 - t
                    rows = slice(tt * SUBLANES, (tt + 1) * SUBLANES)
                    (lr, li), (sr, si) = lam[d], st[d]
                    nr = (lr * sr - li * si) + cur[d, rows, re]
                    ni = (lr * si + li * sr) + cur[d, rows, im]
                    st[d] = (nr, ni)
                    if t % 2 == 0:
                        prev[d] = (nr, ni)
                    else:
                        lo = min(tt, tt + (1 if d == 1 else -1)) * SUBLANES
                        pair = slice(lo, lo + 2 * SUBLANES)
                        first, second = (prev[d], (nr, ni)) if d == 0 else ((nr, ni), prev[d])
                        sb[d, pair, re] = jnp.concatenate([first[0], second[0]], axis=0).astype(BF16)
                        sb[d, pair, im] = jnp.concatenate([first[1], second[1]], axis=0).astype(BF16)
            for d in range(2):
                carry[d, :, re], carry[d, :, im] = st[d]
            for d in range(2):
                part = _dot(sb[d, :, re], cm_ref[d, re, :]) + _dot(sb[d, :, im], cm_ref[d, im, :])
                yacc[d] = part if yacc[d] is None else yacc[d] + part
        for d, y_ref in enumerate((yf_ref, yb_ref)):
            for sl in range(nslab):
                ys[d * nslab + sl] = yacc[d][:, sl * LANES:(sl + 1) * LANES]
            for b in range(SUBLANES):
                for sl in range(nslab):
                    c0 = b * SSM_WIDTH + sl * LANES
                    y_ref[:, c0:c0 + LANES] = ys[d * nslab + sl, pl.ds(b, steps, stride=SUBLANES), :]

    @pl.when(c == 0)
    def _():
        carry[...] = h0_ref[0]

    @pl.when(first)
    def _():
        xin = relayout_in((xf_ref, xb_ref), 0)
        for d in range(2):
            buf0[d] = _dot(xin[d], bm_ref[d])

    @pl.when(c % 2 == 0)
    def _():
        step(buf0, buf1)

    @pl.when(c % 2 == 1)
    def _():
        step(buf1, buf0)

    @pl.when(c == nchunks - 1)
    def _():
        fin_ref[0] = carry[...]


def _s5_call(x_wide, h0, pw, layer, groups, seq, casts=()):
    steps = SCAN_STEPS
    rows = steps * SUBLANES
    nchunks = seq // steps
    assert nchunks % 2 == 0
    width = 2 * SSM_LANES
    nslab = SSM_WIDTH // LANES
    last = groups * nchunks - 1

    def nxt(g, c):
        s = jnp.minimum(g * nchunks + c + 1, last)
        return s // nchunks, s % nchunks

    def fwd_next(g, c):
        g2, c2 = nxt(g, c)
        return (g2, c2, 0)

    def bwd_next(g, c):
        g2, c2 = nxt(g, c)
        return (g2, nchunks - 1 - c2, 0)

    c_in, c_out, c_shape = _cast_specs(casts, groups * nchunks, lambda g, c: g * nchunks + c)
    out = pl.pallas_call(
        functools.partial(_s5_kernel, steps=steps, nchunks=nchunks, ncast=len(casts)),
        out_shape=[jax.ShapeDtypeStruct((groups, seq, WIDE), F32),
                   jax.ShapeDtypeStruct((groups, seq, WIDE), F32),
                   jax.ShapeDtypeStruct((groups, 2, SUBLANES, width), F32)] + c_shape,
        grid=(groups, nchunks),
        in_specs=[pl.BlockSpec((None, steps, WIDE), lambda g, c: (0, 0, 0)),
                  pl.BlockSpec((None, steps, WIDE), lambda g, c: (0, nchunks - 1, 0)),
                  pl.BlockSpec((None, steps, WIDE), fwd_next),
                  pl.BlockSpec((None, steps, WIDE), bwd_next),
                  pl.BlockSpec((1, 2, SUBLANES, width), lambda g, c: (g, 0, 0, 0)),
                  pl.BlockSpec((None, 2, SSM_WIDTH, width), lambda g, c: (layer, 0, 0, 0)),
                  pl.BlockSpec((None, 2, width, SSM_WIDTH), lambda g, c: (layer, 0, 0, 0)),
                  pl.BlockSpec((None, 2, SUBLANES, width), lambda g, c: (layer, 0, 0, 0))] + c_in,
        out_specs=[pl.BlockSpec((None, steps, WIDE), lambda g, c: (g, c, 0)),
                   pl.BlockSpec((None, steps, WIDE), lambda g, c: (g, nchunks - 1 - c, 0)),
                   pl.BlockSpec((1, 2, SUBLANES, width), lambda g, c: (g, 0, 0, 0))] + c_out,
        scratch_shapes=[pltpu.VMEM((4 * nslab, rows, LANES), F32),
                        pltpu.VMEM((2, rows, width), F32),
                        pltpu.VMEM((2, rows, width), F32),
                        pltpu.VMEM((2, rows, width), BF16),
                        pltpu.VMEM((2 * nslab, rows, LANES), F32),
                        pltpu.VMEM((2, SUBLANES, width), F32)],
        compiler_params=_params("arbitrary", "arbitrary"),
        name="s5_scan",
    )(x_wide, x_wide, x_wide, x_wide, h0, pw['bm'], pw['cm'], pw['lam'], *[job[0] for job in casts])
    return out[0], out[1], out[2], out[3:]


def _out_ffn_kernel(x_ref, mod_ref, a_ref, f_ref, us_ref, yf_ref, yb_ref, d_ref, wg_ref, wo_ref,
                    g_ref, w1_ref, w3_ref, w2_ref, o_ref, *, seq):
    m = mod_ref[0]
    us, yf, yb = (_wide_load(r, seq) for r in (us_ref, yf_ref, yb_ref))
    ys = jax.nn.gelu((d_ref[...] * us + yf) + yb)
    gl = _dot(ys.astype(BF16), wg_ref[...])
    so = gl[:, :SSM_WIDTH] * jax.nn.sigmoid(gl[:, SSM_WIDTH:])
    mixed = jnp.concatenate([a_ref[...], f_ref[...], so.astype(BF16)], axis=1)
    x2 = x_ref[...] + m[5:6] * _dot(mixed, wo_ref[...])
    o_ref[...] = _half_ffn(x2, m, 6, g_ref[...], w1_ref, w3_ref, w2_ref)


def _out_ffn_call(x, mod, a, f, us, yf, yb, pw, layer, seq, rows_per_group):
    n = x.shape[0]
    tm = _token_tile(seq)
    wide = _wide_spec(tm, seq)
    return pl.pallas_call(
        functools.partial(_out_ffn_kernel, seq=seq),
        out_shape=jax.ShapeDtypeStruct((n, D_MODEL), F32),
        grid=(n // tm,),
        in_specs=[_row_spec(tm, D_MODEL), _mod_spec(rows_per_group // tm),
                  _row_spec(tm, ATTN_WIDTH), _row_spec(tm, FNET_WIDTH), wide, wide, wide,
                  _const_spec((1, SSM_WIDTH), (layer,)),
                  _const_spec((SSM_WIDTH, 2 * SSM_WIDTH), (layer,)),
                  _const_spec((MIX_WIDTH, D_MODEL), (layer,)),
                  _const_spec((1, D_MODEL), (layer, 2)),
                  _const_spec((D_MODEL, D_FF)),
                  _const_spec((D_MODEL, D_FF)),
                  _const_spec((D_FF, D_MODEL))],
        out_specs=_row_spec(tm, D_MODEL),
        compiler_params=_params("arbitrary"),
        name="out_proj_ffn",
    )(x, mod, a, f, us, yf, yb, pw['d'], pw['w_glu'], pw['w_out'], pw['g'], *pw['ffn'][(layer, 1)])


def _rope_tables(seq):
    pos = np.arange(seq)
    row, col = pos // GRID_W, pos % GRID_W
    quarter = HEAD_DIM // 4
    inv_freq = 1.0 / (ROPE_BASE ** (np.arange(quarter, dtype=np.float64) * 2.0 / (HEAD_DIM // 2)))
    lane = np.arange(LANES)
    in_head = lane % HEAD_DIM
    p = np.where((in_head < HEAD_DIM // 2)[None, :], row[:, None], col[:, None]).astype(np.float64)
    ang = p * inv_freq[lane % quarter][None, :]
    first = ((lane % (HEAD_DIM // 2)) < quarter)[None, :]
    cos = np.cos(ang)
    sa = np.where(first, -np.sin(ang), 0.0)
    sb = np.where(first, 0.0, np.sin(ang))
    return tuple(jnp.asarray(t, dtype=F32) for t in (cos, sa, sb))


def _dft_tables(seq):
    kl = np.outer(np.arange(seq), np.arange(seq)) % seq
    ang = 2.0 * np.pi * kl / seq
    dm = np.concatenate([np.cos(ang), -np.sin(ang)], axis=1)
    mc = np.outer(np.arange(FNET_HEAD_DIM), np.arange(FNET_HEAD_DIM)) % FNET_HEAD_DIM
    a64 = 2.0 * np.pi * mc / FNET_HEAD_DIM
    eye = np.eye(FNET_HEADS)
    wc = np.concatenate([np.kron(eye, np.cos(a64)), np.kron(eye, np.sin(a64))], axis=1)
    return jnp.asarray(dm, dtype=F32).astype(BF16), jnp.asarray(wc, dtype=F32).astype(BF16)


def _head_mean_matrix():
    return jnp.asarray(np.kron(np.eye(N_HEADS), np.full((HEAD_DIM, HEAD_DIM), 1.0 / HEAD_DIM)), dtype=BF16)


def _window_bias(past):
    i = np.arange(GQA_GROUP * BLOCK)[:, None] % BLOCK
    j = np.arange(LOCAL_SPAN)[None, :]
    out = np.zeros((3, GQA_GROUP * BLOCK, past + LOCAL_SPAN), np.float32)
    for v, off in enumerate((0, WINDOW, 2 * WINDOW)):
        out[v, :, past:] = np.where(np.abs(j - off - i) <= WINDOW, 0.0, NEG_INF)
    return jnp.asarray(out)


def _s5_matrices(lam_re, lam_im, b_re, b_im, c_re, c_im, log_step):
    step = jnp.exp(log_step)[..., None]
    mag = jnp.exp(lam_re * step)
    lr = mag * jnp.cos(lam_im * step)
    li = mag * jnp.sin(lam_im * step)
    den = lam_re * lam_re + lam_im * lam_im
    cr = ((lr - 1.0) * lam_re + li * lam_im) / den
    ci = (li * lam_re - (lr - 1.0) * lam_im) / den
    bbr = cr[..., None] * b_re - ci[..., None] * b_im
    bbi = cr[..., None] * b_im + ci[..., None] * b_re
    same_group = jnp.asarray(np.arange(SSM_WIDTH)[:, None] // SSM_GROUP
                             == np.arange(SSM_LANES)[None, :] // SSM_STATE, dtype=F32)

    def in_mat(b):
        rows = b.transpose(0, 1, 4, 2, 3).reshape(DEPTH, 2, SSM_GROUP, SSM_LANES)
        return jnp.tile(rows, (1, 1, SSM_GROUPS, 1)) * same_group

    def out_mat(cw):
        rows = cw.transpose(0, 1, 4, 2, 3).reshape(DEPTH, 2, SSM_STATE, SSM_WIDTH)
        return jnp.tile(rows, (1, 1, SSM_GROUPS, 1)) * same_group.T

    bm = jnp.concatenate([in_mat(bbr), in_mat(bbi)], axis=-1).astype(BF16)
    cm = jnp.concatenate([out_mat(c_re), -out_mat(c_im)], axis=-2).astype(BF16)
    lrow = jnp.concatenate([lr.reshape(DEPTH, 2, SSM_LANES), li.reshape(DEPTH, 2, SSM_LANES)], axis=-1)
    lam = jnp.broadcast_to(lrow[:, :, None, :], (DEPTH, 2, SUBLANES, 2 * SSM_LANES))
    return bm, cm, lam


def _stream_layer(x, mod, pw, layer, batch, seq, rows_per_group, tables, ctx_kv, h0):
    latent = ctx_kv is not None
    groups = batch // SUBLANES
    x1, qm, k, v, kd, vd, f, us = _ffn_in_call(x, mod, pw, layer, tables, latent, batch, seq, rows_per_group)
    pending = [key for key in ([(0, 1)] if not latent else [(l, h) for l in range(1, DEPTH) for h in range(2)])
               if key not in pw['ffn']]
    casts = [(w, l, h) for (l, h) in pending for w in pw['ffn_f32']]
    if latent:
        a, cast = _lat_attn_call(qm, kd, vd, ctx_kv[0], ctx_kv[1], tables['bias'], pw['sink'], layer,
                                 batch, seq, casts)
    else:
        a, _ = _ctx_attn_call(qm, kd, vd, pw['sink'], layer, batch, seq)
    dm, wc = tables['dft'][seq]
    fz = _fnet_call(f, wc, dm, pw['w_fnet'], layer, batch, seq)
    yf, yb, fin, s5_cast = _s5_call(us, h0, pw, layer, groups, seq, () if latent else casts)
    if not latent:
        cast = s5_cast
    for i, key in enumerate(pending):
        pw['ffn'][key] = tuple(cast[3 * i:3 * i + 3])
    x = _out_ffn_call(x1, mod, a, fz, us, yf, yb, pw, layer, seq, rows_per_group)
    return x, (k, v, fin)


def kernel(x_prompt, x_sample, cache_k, cache_v, state_ssm_re, state_ssm_im, c, c_ctx, w_mod, b_mod, norm_g, ffn_w1, ffn_w3, ffn_w2, w_in, w_out, q_norm_g, k_norm_g, attn_sink, w_fnet, ssm_lambda_re, ssm_lambda_im, ssm_b_re, ssm_b_im, ssm_c_re, ssm_c_im, ssm_d, ssm_log_step, ssm_w_glu):
    batch, seq, _ = x_prompt.shape
    dec_batch, dec_seq, _ = x_sample.shape
    past = cache_k.shape[2]

    tables = {'e': _head_mean_matrix(), 'rope': _rope_tables(dec_seq), 'bias': _window_bias(past),
              'dft': {s: _dft_tables(s) for s in {seq, dec_seq}}}

    bm, cm, lam = _s5_matrices(ssm_lambda_re, ssm_lambda_im, ssm_b_re, ssm_b_im, ssm_c_re, ssm_c_im,
                               ssm_log_step)
    pw = {'g': norm_g.reshape(DEPTH, 3, 1, D_MODEL),
          'ffn_f32': (ffn_w1, ffn_w3, ffn_w2),
          'ffn': {(0, 0): tuple(w[0, 0].astype(BF16) for w in (ffn_w1, ffn_w3, ffn_w2))},
          'w_in': w_in.astype(BF16), 'w_out': w_out.astype(BF16),
          'qg': jnp.tile(q_norm_g, (1, N_HEADS)).reshape(DEPTH, 1, ATTN_WIDTH),
          'kg': jnp.tile(k_norm_g, (1, N_KV_HEADS)).reshape(DEPTH, 1, KV_WIDTH),
          'sink': jnp.broadcast_to(attn_sink[:, :, None], (DEPTH, N_HEADS, LANES)),
          'w_fnet': w_fnet.astype(BF16), 'bm': bm, 'cm': cm, 'lam': lam,
          'd': ssm_d.reshape(DEPTH, 1, SSM_WIDTH), 'w_glu': ssm_w_glu.astype(BF16)}

    cond = jnp.zeros((COND_ROWS, D_MODEL), F32).at[0].set(c_ctx).at[1:1 + dec_batch].set(c)
    mods = _mod_call(cond, w_mod, b_mod).reshape(DEPTH, COND_ROWS, N_MOD, D_MODEL)

    ctx_kv = (cache_k.reshape(dec_batch, DEPTH, past, KV_WIDTH), cache_v.reshape(dec_batch, DEPTH, past, KV_WIDTH))
    h0_lat = jnp.concatenate([state_ssm_re.reshape(dec_batch, DEPTH, 2, SSM_LANES),
                              state_ssm_im.reshape(dec_batch, DEPTH, 2, SSM_LANES)], axis=-1)
    h0_lat = h0_lat.reshape(dec_batch // SUBLANES, SUBLANES, DEPTH, 2, 2 * SSM_LANES).transpose(2, 0, 3, 1, 4)
    h0_ctx = jnp.zeros((batch // SUBLANES, 2, SUBLANES, 2 * SSM_LANES), F32)

    yp = x_prompt.reshape(batch * seq, D_MODEL)
    ys = x_sample.reshape(dec_batch * dec_seq, D_MODEL)
    ks, vs, sre, sim = [], [], [], []
    for l in range(DEPTH):
        yp, (k_l, v_l, fin) = _stream_layer(yp, mods[l, 0:1], pw, l, batch, seq, batch * seq, tables,
                                            None, h0_ctx)
        ks.append(k_l.reshape(batch, seq, N_KV_HEADS, HEAD_DIM))
        vs.append(v_l.reshape(batch, seq, N_KV_HEADS, HEAD_DIM))
        fin = fin.reshape(batch // SUBLANES, 2, SUBLANES, 2, SSM_GROUPS, SSM_STATE)
        fin = fin.transpose(3, 0, 2, 1, 4, 5).reshape(2, batch, 2, SSM_GROUPS, SSM_STATE)
        sre.append(fin[0])
        sim.append(fin[1])
        ys, _ = _stream_layer(ys, mods[l, 1:1 + dec_batch], pw, l, dec_batch, dec_seq, dec_seq, tables,
                              ctx_kv, h0_lat[l])

    return (yp.reshape(batch, seq, D_MODEL), ys.reshape(dec_batch, dec_seq, D_MODEL),
            jnp.stack(ks, axis=1), jnp.stack(vs, axis=1), jnp.stack(sre, axis=1), jnp.stack(sim, axis=1))
```

```python
import functools
import math

import numpy as np
import jax
import jax.numpy as jnp
from jax import lax
from jax.experimental import pallas as pl
from jax.experimental.pallas import tpu as pltpu

F32 = jnp.float32
BF16 = jnp.bfloat16

D_MODEL = 1024
DEPTH = 2
GRID_W = 64
HEAD_DIM = 64
N_HEADS = 8
N_KV_HEADS = 2
GQA_GROUP = N_HEADS // N_KV_HEADS
ATTN_WIDTH = N_HEADS * HEAD_DIM
KV_WIDTH = N_KV_HEADS * HEAD_DIM
WINDOW = 128
BLOCK = 128
ATTN_SCALE = HEAD_DIM ** -0.5
ROPE_BASE = 10000.0
NEG_INF = -1e30
LOG2E = math.log2(math.e)
FNET_HEADS = 4
FNET_HEAD_DIM = 64
FNET_WIDTH = FNET_HEADS * FNET_HEAD_DIM
SSM_WIDTH = 256
SSM_GROUP = 16
SSM_GROUPS = SSM_WIDTH // SSM_GROUP
SSM_STATE = 64
SSM_LANES = SSM_GROUPS * SSM_STATE
MIX_WIDTH = ATTN_WIDTH + FNET_WIDTH + SSM_WIDTH
Q_END = ATTN_WIDTH
K_END = Q_END + KV_WIDTH
V_END = K_END + KV_WIDTH
F_END = V_END + FNET_WIDTH
S_END = F_END + SSM_WIDTH
IN_WIDTH = S_END
D_FF = 2816
N_MOD = 9
NORM_EPS = 1e-6

LANES = 128
SUBLANES = 8
VMEM_LIMIT_BYTES = 60 * 1024 * 1024
TOKEN_TILE = 1024
FF_CHUNKS = 11
FF_CHUNK = D_FF // FF_CHUNKS
ATTN_BLOCKS_PER_STEP = 4
SCAN_STEPS = 64
SCAN_LANE_CHUNK = 256
COND_ROWS = 16
LOCAL_SPAN = BLOCK + 2 * WINDOW
QM_WIDTH = N_HEADS * LANES
KD_WIDTH = N_KV_HEADS * LANES
VD_WIDTH = N_KV_HEADS * 2 * LANES
WIDE = SUBLANES * SSM_WIDTH


def _params(*sem):
    return pltpu.CompilerParams(dimension_semantics=sem, vmem_limit_bytes=VMEM_LIMIT_BYTES)


def _dot(a, b):
    return jnp.dot(a, b, preferred_element_type=F32)


def _norm_mod(x, g, shift, scale):
    y = x * lax.rsqrt(jnp.mean(x * x, axis=-1, keepdims=True) + NORM_EPS)
    return (y * g) * (1.0 + scale) + shift


def _const_spec(shape, lead=()):
    nd = len(shape)
    idx = tuple(lead) + (0,) * nd
    return pl.BlockSpec((None,) * len(lead) + tuple(shape), lambda *_: idx,
                        pipeline_mode=pl.Buffered(1))


def _row_spec(tm, width):
    return pl.BlockSpec((tm, width), lambda i: (i, 0))


def _mod_spec(tiles_per_group):
    return pl.BlockSpec((1, N_MOD, D_MODEL), lambda i: (i // tiles_per_group, 0, 0))


def _wide_spec(tm, seq):
    if tm <= seq:
        tiles_per_seq = seq // tm

        def index(i):
            b = i // tiles_per_seq
            return (b // SUBLANES, i % tiles_per_seq, b % SUBLANES)
        return pl.BlockSpec((None, tm, SSM_WIDTH), index)
    nseq = tm // seq

    def index(i):
        b = i * nseq
        return (b // SUBLANES, 0, (b % SUBLANES) // nseq)
    return pl.BlockSpec((None, seq, nseq * SSM_WIDTH), index)


def _wide_store(ref, x, seq):
    for j in range(x.shape[0] // seq if x.shape[0] > seq else 1):
        rows = slice(j * seq, (j + 1) * seq) if x.shape[0] > seq else slice(None)
        ref[:, j * SSM_WIDTH:(j + 1) * SSM_WIDTH] = x[rows]


def _wide_load(ref, seq):
    ncol = ref.shape[1] // SSM_WIDTH
    if ncol == 1:
        return ref[...]
    return jnp.concatenate([ref[:, j * SSM_WIDTH:(j + 1) * SSM_WIDTH] for j in range(ncol)], axis=0)


def _mod_kernel(c_ref, w_ref, b_ref, o_ref):
    c = c_ref[...]
    a = (c * jax.nn.sigmoid(c)).astype(BF16)
    o_ref[0] = _dot(a, w_ref[0].astype(BF16)) + b_ref[0]


def _mod_call(cond, w_mod, b_mod):
    tn = N_MOD * D_MODEL // 4
    return pl.pallas_call(
        _mod_kernel,
        out_shape=jax.ShapeDtypeStruct((DEPTH, COND_ROWS, N_MOD * D_MODEL), F32),
        grid=(DEPTH, N_MOD * D_MODEL // tn),
        in_specs=[pl.BlockSpec((COND_ROWS, D_MODEL), lambda l, j: (0, 0)),
                  pl.BlockSpec((1, D_MODEL, tn), lambda l, j: (l, 0, j)),
                  pl.BlockSpec((1, 1, tn), lambda l, j: (l, 0, j))],
        out_specs=pl.BlockSpec((1, COND_ROWS, tn), lambda l, j: (l, 0, j)),
        compiler_params=_params("arbitrary", "arbitrary"),
        name="cond_mod",
    )(cond, w_mod, b_mod.reshape(DEPTH, 1, N_MOD * D_MODEL))


def _half_ffn(x, m, k0, g, w1_ref, w3_ref, w2_ref):
    h = _norm_mod(x, g, m[k0:k0 + 1], m[k0 + 1:k0 + 2]).astype(BF16)
    y = None
    for c in range(FF_CHUNKS):
        cols = slice(c * FF_CHUNK, (c + 1) * FF_CHUNK)
        a = _dot(h, w1_ref[:, cols])
        b = _dot(h, w3_ref[:, cols])
        t = ((a * jax.nn.sigmoid(a)) * b).astype(BF16)
        yc = _dot(t, w2_ref[cols, :])
        y = yc if y is None else y + yc
    return x + (0.5 * m[k0 + 2:k0 + 3]) * y


def _head_norm(z, e, g):
    ms = _dot((z * z).astype(BF16), e)
    return (z * lax.rsqrt(ms + NORM_EPS)) * g


def _low_half(shape):
    return lax.broadcasted_iota(jnp.int32, shape, 1) < HEAD_DIM


def _kv_slabs(k, v):
    low = _low_half(k.shape)
    kr = pltpu.roll(k, HEAD_DIM, 1)
    vr = pltpu.roll(v, HEAD_DIM, 1)
    kd = jnp.concatenate([jnp.where(low, k, kr), jnp.where(low, kr, k)], axis=1)
    vd = jnp.concatenate([jnp.where(low, v, 1.0), jnp.where(low, 1.0, vr),
                          jnp.where(low, vr, 1.0), jnp.where(low, 1.0, v)], axis=1)
    return kd.astype(BF16), vd.astype(BF16)


def _ffn_in_kernel(*refs, rope, seq):
    if rope:
        (x_ref, mod_ref, g0_ref, g1_ref, w1_ref, w3_ref, w2_ref, w_ref, qg_ref, kg_ref, e_ref,
         cos_ref, sa_ref, sb_ref, x1_ref, qm_ref, k_ref, v_ref, kd_ref, vd_ref, f_ref, s_ref) = refs
    else:
        (x_ref, mod_ref, g0_ref, g1_ref, w1_ref, w3_ref, w2_ref, w_ref, qg_ref, kg_ref, e_ref,
         x1_ref, qm_ref, k_ref, v_ref, kd_ref, vd_ref, f_ref, s_ref) = refs
    m = mod_ref[0]
    x1 = _half_ffn(x_ref[...], m, 0, g0_ref[...], w1_ref, w3_ref, w2_ref)
    x1_ref[...] = x1
    h = _norm_mod(x1, g1_ref[...], m[3:4], m[4:5]).astype(BF16)
    u = _dot(h, w_ref[...])
    q = _head_norm(u[:, :Q_END], e_ref[...], qg_ref[...])
    k = _head_norm(u[:, Q_END:K_END], e_ref[0:KV_WIDTH, 0:KV_WIDTH], kg_ref[...])
    v = u[:, K_END:V_END]
    if rope:
        cos, sa, sb = cos_ref[...], sa_ref[...], sb_ref[...]

        def rot(z):
            return z * cos + pltpu.roll(z, LANES - 16, 1) * sa + pltpu.roll(z, 16, 1) * sb
    else:
        def rot(z):
            return z

    low = _low_half((q.shape[0], LANES))
    for i in range(ATTN_WIDTH // LANES):
        qs = rot(q[:, i * LANES:(i + 1) * LANES]) * (ATTN_SCALE * LOG2E)
        qm_ref[:, (2 * i) * LANES:(2 * i + 1) * LANES] = jnp.where(low, qs, 0.0).astype(BF16)
        qm_ref[:, (2 * i + 1) * LANES:(2 * i + 2) * LANES] = jnp.where(low, 0.0, qs).astype(BF16)
    k = rot(k)
    k_ref[...] = k
    v_ref[...] = v
    kd_ref[...], vd_ref[...] = _kv_slabs(k, v)
    f_ref[...] = u[:, V_END:F_END].astype(BF16)
    _wide_store(s_ref, u[:, F_END:S_END], seq)


def _token_tile(seq):
    return TOKEN_TILE if TOKEN_TILE <= seq else min(TOKEN_TILE, SUBLANES * seq // 2)


def _ffn_in_call(x, mod, pw, layer, tables, rope, batch, seq, rows_per_group):
    n = x.shape[0]
    tm = _token_tile(seq)
    tiles_per_seq = max(seq // tm, 1)
    in_specs = [_row_spec(tm, D_MODEL), _mod_spec(rows_per_group // tm),
                _const_spec((1, D_MODEL), (layer, 0)),
                _const_spec((1, D_MODEL), (layer, 1)),
                _const_spec((D_MODEL, D_FF)),
                _const_spec((D_MODEL, D_FF)),
                _const_spec((D_FF, D_MODEL)),
                _const_spec((D_MODEL, IN_WIDTH), (layer,)),
                _const_spec((1, ATTN_WIDTH), (layer,)),
                _const_spec((1, KV_WIDTH), (layer,)),
                _const_spec((ATTN_WIDTH, ATTN_WIDTH))]
    args = [x, mod, pw['g'], pw['g'], *pw['ffn'][(layer, 0)], pw['w_in'], pw['qg'], pw['kg'],
            tables['e']]
    if rope:
        in_specs += [pl.BlockSpec((tm, LANES), lambda i: (i % tiles_per_seq, 0))] * 3
        args += list(tables['rope'])
    widths = (D_MODEL, QM_WIDTH, KV_WIDTH, KV_WIDTH, KD_WIDTH, VD_WIDTH, FNET_WIDTH)
    dtypes = (F32, BF16, F32, F32, BF16, BF16, BF16)
    out_shape = [jax.ShapeDtypeStruct((n, w), d) for w, d in zip(widths, dtypes)]
    out_shape.append(jax.ShapeDtypeStruct((batch // SUBLANES, seq, WIDE), F32))
    out_specs = [_row_spec(tm, w) for w in widths] + [_wide_spec(tm, seq)]
    return pl.pallas_call(
        functools.partial(_ffn_in_kernel, rope=rope, seq=seq),
        out_shape=out_shape,
        grid=(n // tm,),
        in_specs=in_specs,
        out_specs=out_specs,
        compiler_params=_params("arbitrary"),
        name="ffn_in_proj",
    )(*args)


def _attend(qm_ref, o_ref, rows, sink_ref, kv, bias):
    nq = rows.stop - rows.start
    low = _low_half((nq, LANES))
    for h in range(N_KV_HEADS):
        j0 = GQA_GROUP * h
        qst = jnp.concatenate([qm_ref[rows, (j0 + g) * LANES:(j0 + g + 1) * LANES]
                               for g in range(GQA_GROUP)], axis=0)
        sink = jnp.concatenate([jnp.broadcast_to(sink_ref[j0 + g:j0 + g + 1, :], (nq, LANES))
                                for g in range(GQA_GROUP)], axis=0) * LOG2E
        kd, vd = kv(h)
        s = lax.dot_general(qst, kd, (((1,), (1,)), ((), ())), preferred_element_type=F32)
        if bias is not None:
            s = s + bias
        m = jnp.maximum(jnp.max(s, axis=-1, keepdims=True), sink)
        e = jnp.exp2(s - jnp.concatenate([m] * (s.shape[1] // LANES), axis=1)).astype(BF16)
        o = _dot(e, vd)
        es = jnp.exp2(sink - m)
        for t in range(GQA_GROUP // 2):
            r0, r1 = (2 * t) * nq, (2 * t + 1) * nq
            num = jnp.where(low, o[r0:r0 + nq, :LANES], o[r1:r1 + nq, LANES:])
            den = (jnp.where(low, o[r0:r0 + nq, LANES:], o[r1:r1 + nq, :LANES])
                   + jnp.where(low, es[r0:r0 + nq], es[r1:r1 + nq]))
            c0 = (j0 // 2 + t) * LANES
            o_ref[rows, c0:c0 + LANES] = (num * (1.0 / den)).astype(o_ref.dtype)


def _cast_specs(jobs, nsteps, step_of):
    in_specs, out_specs, out_shape = [], [], []
    for arr, layer, half in jobs:
        rows, cols = arr.shape[2:]
        rb = rows // nsteps
        in_specs.append(pl.BlockSpec((None, None, rb, cols),
                                     lambda *g, layer=layer, half=half: (layer, half, step_of(*g), 0)))
        out_specs.append(pl.BlockSpec((rb, cols), lambda *g: (step_of(*g), 0)))
        out_shape.append(jax.ShapeDtypeStruct((rows, cols), BF16))
    return in_specs, out_specs, out_shape


def _run_casts(srcs, dsts):
    for s, d in zip(srcs, dsts):
        d[...] = s[...].astype(BF16)


def _ctx_attn_kernel(qm_ref, kd_ref, vd_ref, sink_ref, *rest, seq, nseq, ncast):
    o_ref = rest[ncast]
    _run_casts(rest[:ncast], rest[ncast + 1:])
    for i in range(nseq):
        rows = slice(i * seq, (i + 1) * seq)

        def kv(h, rows=rows):
            return (kd_ref[rows, h * LANES:(h + 1) * LANES], vd_ref[rows, 2 * h * LANES:2 * (h + 1) * LANES])

        _attend(qm_ref, o_ref, rows, sink_ref, kv, None)


def _ctx_attn_call(qm, kd, vd, sink, layer, batch, seq, casts=()):
    nseq = 2 * ATTN_BLOCKS_PER_STEP
    rows = nseq * seq
    nsteps = batch // nseq
    c_in, c_out, c_shape = _cast_specs(casts, nsteps, lambda b: b)
    out = pl.pallas_call(
        functools.partial(_ctx_attn_kernel, seq=seq, nseq=nseq, ncast=len(casts)),
        out_shape=[jax.ShapeDtypeStruct((batch * seq, ATTN_WIDTH), BF16)] + c_shape,
        grid=(nsteps,),
        in_specs=[_row_spec(rows, QM_WIDTH), _row_spec(rows, KD_WIDTH), _row_spec(rows, VD_WIDTH),
                  pl.BlockSpec((None, N_HEADS, LANES), lambda b: (layer, 0, 0))] + c_in,
        out_specs=[_row_spec(rows, ATTN_WIDTH)] + c_out,
        compiler_params=_params("arbitrary"),
        name="ctx_attn",
    )(qm, kd, vd, sink, *[job[0] for job in casts])
    return out[0], out[1:]


def _lat_attn_kernel(qm_ref, kl_ref, vl_ref, kc_ref, vc_ref, bias_ref, sink_ref, *rest,
                     seq, nblk, ncast):
    o_ref = rest[ncast]
    kc_sc, vc_sc = rest[-2:]
    _run_casts(rest[:ncast], rest[ncast + 1:-2])
    j = pl.program_id(1)
    nb = seq // BLOCK

    @pl.when(j == 0)
    def _():
        kc_sc[...], vc_sc[...] = _kv_slabs(kc_ref[...], vc_ref[...])

    for i in range(nblk):
        n = j * nblk + i
        start = pl.multiple_of(jnp.clip(n * BLOCK - WINDOW, 0, seq - LOCAL_SPAN), BLOCK)
        variant = jnp.where(n == 0, 0, jnp.where(n == nb - 1, 2, 1))

        def kv(h, start=start):
            kc, vc = slice(h * LANES, (h + 1) * LANES), slice(2 * h * LANES, 2 * (h + 1) * LANES)
            return (jnp.concatenate([kc_sc[:, kc], kl_ref[pl.ds(start, LOCAL_SPAN), kc]], axis=0),
                    jnp.concatenate([vc_sc[:, vc], vl_ref[pl.ds(start, LOCAL_SPAN), vc]], axis=0))

        _attend(qm_ref, o_ref, slice(i * BLOCK, (i + 1) * BLOCK), sink_ref, kv, bias_ref[variant])


def _lat_attn_call(qm, kd, vd, kctx, vctx, bias, sink, layer, batch, seq, casts=()):
    past = kctx.shape[2]
    nblk = 4 * ATTN_BLOCKS_PER_STEP
    steps = seq // (BLOCK * nblk)
    rows = BLOCK * nblk
    c_in, c_out, c_shape = _cast_specs(casts, batch * steps, lambda b, j: b * steps + j)
    out = pl.pallas_call(
        functools.partial(_lat_attn_kernel, seq=seq, nblk=nblk, ncast=len(casts)),
        out_shape=[jax.ShapeDtypeStruct((batch * seq, ATTN_WIDTH), BF16)] + c_shape,
        grid=(batch, steps),
        in_specs=[pl.BlockSpec((rows, QM_WIDTH), lambda b, j: (b * steps + j, 0)),
                  pl.BlockSpec((seq, KD_WIDTH), lambda b, j: (b, 0)),
                  pl.BlockSpec((seq, VD_WIDTH), lambda b, j: (b, 0)),
                  pl.BlockSpec((None, None, past, KV_WIDTH), lambda b, j: (b, layer, 0, 0)),
                  pl.BlockSpec((None, None, past, KV_WIDTH), lambda b, j: (b, layer, 0, 0)),
                  pl.BlockSpec(bias.shape, lambda b, j: (0, 0, 0)),
                  pl.BlockSpec((None, N_HEADS, LANES), lambda b, j: (layer, 0, 0))] + c_in,
        out_specs=[pl.BlockSpec((rows, ATTN_WIDTH), lambda b, j: (b * steps + j, 0))] + c_out,
        scratch_shapes=[pltpu.VMEM((past, KD_WIDTH), BF16), pltpu.VMEM((past, VD_WIDTH), BF16)],
        compiler_params=_params("arbitrary", "arbitrary"),
        name="lat_attn",
    )(qm, kd, vd, kctx, vctx, bias, sink, *[job[0] for job in casts])
    return out[0], out[1:]


def _fnet_kernel(x_ref, wc_ref, dm_ref, wf_ref, o_ref, *, seq):
    for i in range(x_ref.shape[0] // seq):
        rows = slice(i * seq, (i + 1) * seq)
        y = _dot(x_ref[rows, :], wc_ref[...])
        yst = jnp.concatenate([y[:, :FNET_WIDTH], y[:, FNET_WIDTH:]], axis=0).astype(BF16)
        z = _dot(dm_ref[...], yst) * ((seq * FNET_HEAD_DIM) ** -0.5)
        o_ref[rows, :] = _dot(z.astype(BF16), wf_ref[...]).astype(o_ref.dtype)


def _fnet_call(x, wc, dm, wf, layer, batch, seq):
    rows = max(seq, TOKEN_TILE)
    return pl.pallas_call(
        functools.partial(_fnet_kernel, seq=seq),
        out_shape=jax.ShapeDtypeStruct((batch * seq, FNET_WIDTH), BF16),
        grid=(batch * seq // rows,),
        in_specs=[_row_spec(rows, FNET_WIDTH),
                  _const_spec((FNET_WIDTH, 2 * FNET_WIDTH)),
                  _const_spec((seq, 2 * seq)),
                  _const_spec((FNET_WIDTH, FNET_WIDTH), (layer,))],
        out_specs=_row_spec(rows, FNET_WIDTH),
        compiler_params=_params("arbitrary"),
        name="fourier_mix",
    )(x, wc, dm, wf)


def _s5_kernel(xf_ref, xb_ref, xfn_ref, xbn_ref, h0_ref, bm_ref, cm_ref, lam_ref, *rest,
               steps, nchunks, ncast):
    yf_ref, yb_ref, fin_ref = rest[ncast:ncast + 3]
    xs, buf0, buf1, sb, ys, carry = rest[-6:]
    _run_casts(rest[:ncast], rest[ncast + 3:-6])
    _s5_body(xf_ref, xb_ref, xfn_ref, xbn_ref, h0_ref, bm_ref, cm_ref, lam_ref,
             yf_ref, yb_ref, fin_ref, xs, buf0, buf1, sb, ys, carry, steps=steps, nchunks=nchunks)


def _s5_body(xf_ref, xb_ref, xfn_ref, xbn_ref, h0_ref, bm_ref, cm_ref, lam_ref,
             yf_ref, yb_ref, fin_ref, xs, buf0, buf1, sb, ys, carry, *, steps, nchunks):
    first = (pl.program_id(0) == 0) & (pl.program_id(1) == 0)
    c = pl.program_id(1)
    nslab = SSM_WIDTH // LANES
    w = SCAN_LANE_CHUNK

    def relayout_in(x_refs, base):
        xin = []
        for d, x_ref in enumerate(x_refs):
            for b in range(SUBLANES):
                for sl in range(nslab):
                    c0 = b * SSM_WIDTH + sl * LANES
                    xs[base + d * nslab + sl, pl.ds(b, steps, stride=SUBLANES), :] = x_ref[:, c0:c0 + LANES]
            xin.append(jnp.concatenate([xs[base + d * nslab + sl] for sl in range(nslab)],
                                       axis=1).astype(BF16))
        return xin

    def step(cur, nxt):
        xin = relayout_in((xfn_ref, xbn_ref), 2 * nslab)
        yacc = [None, None]
        for lc in range(SSM_LANES // w):
            re = slice(lc * w, (lc + 1) * w)
            im = slice(SSM_LANES + lc * w, SSM_LANES + (lc + 1) * w)
            for d in range(2):
                nxt[d, :, re] = _dot(xin[d], bm_ref[d, :, re])
                nxt[d, :, im] = _dot(xin[d], bm_ref[d, :, im])
            lam = [(lam_ref[d, :, re], lam_ref[d, :, im]) for d in range(2)]
            st = [(carry[d, :, re], carry[d, :, im]) for d in range(2)]
            prev = [None, None]
            for t in range(steps):
                for d in range(2):
                    tt = t if d == 0 else steps - 1 - t
                    rows = slice(tt * SUBLANES, (tt + 1) * SUBLANES)
                    (lr, li), (sr, si) = lam[d], st[d]
                    nr = (lr * sr - li * si) + cur[d, rows, re]
                    ni = (lr * si + li * sr) + cur[d, rows, im]
                    st[d] = (nr, ni)
                    if t % 2 == 0:
                        prev[d] = (nr, ni)
                    else:
                        lo = min(tt, tt + (1 if d == 1 else -1)) * SUBLANES
                        pair = slice(lo, lo + 2 * SUBLANES)
                        first, second = (prev[d], (nr, ni)) if d == 0 else ((nr, ni), prev[d])
                        sb[d, pair, re] = jnp.concatenate([first[0], second[0]], axis=0).astype(BF16)
                        sb[d, pair, im] = jnp.concatenate([first[1], second[1]], axis=0).astype(BF16)
            for d in range(2):
                carry[d, :, re], carry[d, :, im] = st[d]
            for d in range(2):
                part = _dot(sb[d, :, re], cm_ref[d, re, :]) + _dot(sb[d, :, im], cm_ref[d, im, :])
                yacc[d] = part if yacc[d] is None else yacc[d] + part
        for d, y_ref in enumerate((yf_ref, yb_ref)):
            for sl in range(nslab):
                ys[d * nslab + sl] = yacc[d][:, sl * LANES:(sl + 1) * LANES]
            for b in range(SUBLANES):
                for sl in range(nslab):
                    c0 = b * SSM_WIDTH + sl * LANES
                    y_ref[:, c0:c0 + LANES] = ys[d * nslab + sl, pl.ds(b, steps, stride=SUBLANES), :]

    @pl.when(c == 0)
    def _():
        carry[...] = h0_ref[0]

    @pl.when(first)
    def _():
        xin = relayout_in((xf_ref, xb_ref), 0)
        for d in range(2):
            buf0[d] = _dot(xin[d], bm_ref[d])

    @pl.when(c % 2 == 0)
    def _():
        step(buf0, buf1)

    @pl.when(c % 2 == 1)
    def _():
        step(buf1, buf0)

    @pl.when(c == nchunks - 1)
    def _():
        fin_ref[0] = carry[...]


def _s5_call(x_wide, h0, pw, layer, groups, seq, casts=()):
    steps = SCAN_STEPS
    rows = steps * SUBLANES
    nchunks = seq // steps
    assert nchunks % 2 == 0
    width = 2 * SSM_LANES
    nslab = SSM_WIDTH // LANES
    last = groups * nchunks - 1

    def nxt(g, c):
        s = jnp.minimum(g * nchunks + c + 1, last)
        return s // nchunks, s % nchunks

    def fwd_next(g, c):
        g2, c2 = nxt(g, c)
        return (g2, c2, 0)

    def bwd_next(g, c):
        g2, c2 = nxt(g, c)
        return (g2, nchunks - 1 - c2, 0)

    c_in, c_out, c_shape = _cast_specs(casts, groups * nchunks, lambda g, c: g * nchunks + c)
    out = pl.pallas_call(
        functools.partial(_s5_kernel, steps=steps, nchunks=nchunks, ncast=len(casts)),
        out_shape=[jax.ShapeDtypeStruct((groups, seq, WIDE), F32),
                   jax.ShapeDtypeStruct((groups, seq, WIDE), F32),
                   jax.ShapeDtypeStruct((groups, 2, SUBLANES, width), F32)] + c_shape,
        grid=(groups, nchunks),
        in_specs=[pl.BlockSpec((None, steps, WIDE), lambda g, c: (0, 0, 0)),
                  pl.BlockSpec((None, steps, WIDE), lambda g, c: (0, nchunks - 1, 0)),
                  pl.BlockSpec((None, steps, WIDE), fwd_next),
                  pl.BlockSpec((None, steps, WIDE), bwd_next),
                  pl.BlockSpec((1, 2, SUBLANES, width), lambda g, c: (g, 0, 0, 0)),
                  pl.BlockSpec((None, 2, SSM_WIDTH, width), lambda g, c: (layer, 0, 0, 0)),
                  pl.BlockSpec((None, 2, width, SSM_WIDTH), lambda g, c: (layer, 0, 0, 0)),
                  pl.BlockSpec((None, 2, SUBLANES, width), lambda g, c: (layer, 0, 0, 0))] + c_in,
        out_specs=[pl.BlockSpec((None, steps, WIDE), lambda g, c: (g, c, 0)),
                   pl.BlockSpec((None, steps, WIDE), lambda g, c: (g, nchunks - 1 - c, 0)),
                   pl.BlockSpec((1, 2, SUBLANES, width), lambda g, c: (g, 0, 0, 0))] + c_out,
        scratch_shapes=[pltpu.VMEM((4 * nslab, rows, LANES), F32),
                        pltpu.VMEM((2, rows, width), F32),
                        pltpu.VMEM((2, rows, width), F32),
                        pltpu.VMEM((2, rows, width), BF16),
                        pltpu.VMEM((2 * nslab, rows, LANES), F32),
                        pltpu.VMEM((2, SUBLANES, width), F32)],
        compiler_params=_params("arbitrary", "arbitrary"),
        name="s5_scan",
    )(x_wide, x_wide, x_wide, x_wide, h0, pw['bm'], pw['cm'], pw['lam'], *[job[0] for job in casts])
    return out[0], out[1], out[2], out[3:]


def _out_ffn_kernel(x_ref, mod_ref, a_ref, f_ref, us_ref, yf_ref, yb_ref, d_ref, wg_ref, wo_ref,
                    g_ref, w1_ref, w3_ref, w2_ref, o_ref, *, seq):
    m = mod_ref[0]
    us, yf, yb = (_wide_load(r, seq) for r in (us_ref, yf_ref, yb_ref))
    ys = jax.nn.gelu((d_ref[...] * us + yf) + yb)
    gl = _dot(ys.astype(BF16), wg_ref[...])
    so = gl[:, :SSM_WIDTH] * jax.nn.sigmoid(gl[:, SSM_WIDTH:])
    mixed = jnp.concatenate([a_ref[...], f_ref[...], so.astype(BF16)], axis=1)
    x2 = x_ref[...] + m[5:6] * _dot(mixed, wo_ref[...])
    o_ref[...] = _half_ffn(x2, m, 6, g_ref[...], w1_ref, w3_ref, w2_ref)


def _out_ffn_call(x, mod, a, f, us, yf, yb, pw, layer, seq, rows_per_group):
    n = x.shape[0]
    tm = _token_tile(seq)
    wide = _wide_spec(tm, seq)
    return pl.pallas_call(
        functools.partial(_out_ffn_kernel, seq=seq),
        out_shape=jax.ShapeDtypeStruct((n, D_MODEL), F32),
        grid=(n // tm,),
        in_specs=[_row_spec(tm, D_MODEL), _mod_spec(rows_per_group // tm),
                  _row_spec(tm, ATTN_WIDTH), _row_spec(tm, FNET_WIDTH), wide, wide, wide,
                  _const_spec((1, SSM_WIDTH), (layer,)),
                  _const_spec((SSM_WIDTH, 2 * SSM_WIDTH), (layer,)),
                  _const_spec((MIX_WIDTH, D_MODEL), (layer,)),
                  _const_spec((1, D_MODEL), (layer, 2)),
                  _const_spec((D_MODEL, D_FF)),
                  _const_spec((D_MODEL, D_FF)),
                  _const_spec((D_FF, D_MODEL))],
        out_specs=_row_spec(tm, D_MODEL),
        compiler_params=_params("arbitrary"),
        name="out_proj_ffn",
    )(x, mod, a, f, us, yf, yb, pw['d'], pw['w_glu'], pw['w_out'], pw['g'], *pw['ffn'][(layer, 1)])


def _rope_tables(seq):
    pos = np.arange(seq)
    row, col = pos // GRID_W, pos % GRID_W
    quarter = HEAD_DIM // 4
    inv_freq = 1.0 / (ROPE_BASE ** (np.arange(quarter, dtype=np.float64) * 2.0 / (HEAD_DIM // 2)))
    lane = np.arange(LANES)
    in_head = lane % HEAD_DIM
    p = np.where((in_head < HEAD_DIM // 2)[None, :], row[:, None], col[:, None]).astype(np.float64)
    ang = p * inv_freq[lane % quarter][None, :]
    first = ((lane % (HEAD_DIM // 2)) < quarter)[None, :]
    cos = np.cos(ang)
    sa = np.where(first, -np.sin(ang), 0.0)
    sb = np.where(first, 0.0, np.sin(ang))
    return tuple(jnp.asarray(t, dtype=F32) for t in (cos, sa, sb))


def _dft_tables(seq):
    kl = np.outer(np.arange(seq), np.arange(seq)) % seq
    ang = 2.0 * np.pi * kl / seq
    dm = np.concatenate([np.cos(ang), -np.sin(ang)], axis=1)
    mc = np.outer(np.arange(FNET_HEAD_DIM), np.arange(FNET_HEAD_DIM)) % FNET_HEAD_DIM
    a64 = 2.0 * np.pi * mc / FNET_HEAD_DIM
    eye = np.eye(FNET_HEADS)
    wc = np.concatenate([np.kron(eye, np.cos(a64)), np.kron(eye, np.sin(a64))], axis=1)
    return jnp.asarray(dm, dtype=F32).astype(BF16), jnp.asarray(wc, dtype=F32).astype(BF16)


def _head_mean_matrix():
    return jnp.asarray(np.kron(np.eye(N_HEADS), np.full((HEAD_DIM, HEAD_DIM), 1.0 / HEAD_DIM)), dtype=BF16)


def _window_bias(past):
    i = np.arange(GQA_GROUP * BLOCK)[:, None] % BLOCK
    j = np.arange(LOCAL_SPAN)[None, :]
    out = np.zeros((3, GQA_GROUP * BLOCK, past + LOCAL_SPAN), np.float32)
    for v, off in enumerate((0, WINDOW, 2 * WINDOW)):
        out[v, :, past:] = np.where(np.abs(j - off - i) <= WINDOW, 0.0, NEG_INF)
    return jnp.asarray(out)


def _s5_matrices(lam_re, lam_im, b_re, b_im, c_re, c_im, log_step):
    step = jnp.exp(log_step)[..., None]
    mag = jnp.exp(lam_re * step)
    lr = mag * jnp.cos(lam_im * step)
    li = mag * jnp.sin(lam_im * step)
    den = lam_re * lam_re + lam_im * lam_im
    cr = ((lr - 1.0) * lam_re + li * lam_im) / den
    ci = (li * lam_re - (lr - 1.0) * lam_im) / den
    bbr = cr[..., None] * b_re - ci[..., None] * b_im
    bbi = cr[..., None] * b_im + ci[..., None] * b_re
    same_group = jnp.asarray(np.arange(SSM_WIDTH)[:, None] // SSM_GROUP
                             == np.arange(SSM_LANES)[None, :] // SSM_STATE, dtype=F32)

    def in_mat(b):
        rows = b.transpose(0, 1, 4, 2, 3).reshape(DEPTH, 2, SSM_GROUP, SSM_LANES)
        return jnp.tile(rows, (1, 1, SSM_GROUPS, 1)) * same_group

    def out_mat(cw):
        rows = cw.transpose(0, 1, 4, 2, 3).reshape(DEPTH, 2, SSM_STATE, SSM_WIDTH)
        return jnp.tile(rows, (1, 1, SSM_GROUPS, 1)) * same_group.T

    bm = jnp.concatenate([in_mat(bbr), in_mat(bbi)], axis=-1).astype(BF16)
    cm = jnp.concatenate([out_mat(c_re), -out_mat(c_im)], axis=-2).astype(BF16)
    lrow = jnp.concatenate([lr.reshape(DEPTH, 2, SSM_LANES), li.reshape(DEPTH, 2, SSM_LANES)], axis=-1)
    lam = jnp.broadcast_to(lrow[:, :, None, :], (DEPTH, 2, SUBLANES, 2 * SSM_LANES))
    return bm, cm, lam


def _stream_layer(x, mod, pw, layer, batch, seq, rows_per_group, tables, ctx_kv, h0):
    latent = ctx_kv is not None
    groups = batch // SUBLANES
    x1, qm, k, v, kd, vd, f, us = _ffn_in_call(x, mod, pw, layer, tables, latent, batch, seq, rows_per_group)
    pending = [key for key in ([(0, 1)] if not latent else [(l, h) for l in range(1, DEPTH) for h in range(2)])
               if key not in pw['ffn']]
    casts = [(w, l, h) for (l, h) in pending for w in pw['ffn_f32']]
    if latent:
        a, cast = _lat_attn_call(qm, kd, vd, ctx_kv[0], ctx_kv[1], tables['bias'], pw['sink'], layer,
                                 batch, seq, casts)
    else:
        a, _ = _ctx_attn_call(qm, kd, vd, pw['sink'], layer, batch, seq)
    dm, wc = tables['dft'][seq]
    fz = _fnet_call(f, wc, dm, pw['w_fnet'], layer, batch, seq)
    yf, yb, fin, s5_cast = _s5_call(us, h0, pw, layer, groups, seq, () if latent else casts)
    if not latent:
        cast = s5_cast
    for i, key in enumerate(pending):
        pw['ffn'][key] = tuple(cast[3 * i:3 * i + 3])
    x = _out_ffn_call(x1, mod, a, fz, us, yf, yb, pw, layer, seq, rows_per_group)
    return x, (k, v, fin)


def kernel(x_prompt, x_sample, cache_k, cache_v, state_ssm_re, state_ssm_im, c, c_ctx, w_mod, b_mod, norm_g, ffn_w1, ffn_w3, ffn_w2, w_in, w_out, q_norm_g, k_norm_g, attn_sink, w_fnet, ssm_lambda_re, ssm_lambda_im, ssm_b_re, ssm_b_im, ssm_c_re, ssm_c_im, ssm_d, ssm_log_step, ssm_w_glu):
    batch, seq, _ = x_prompt.shape
    dec_batch, dec_seq, _ = x_sample.shape
    past = cache_k.shape[2]

    tables = {'e': _head_mean_matrix(), 'rope': _rope_tables(dec_seq), 'bias': _window_bias(past),
              'dft': {s: _dft_tables(s) for s in {seq, dec_seq}}}

    bm, cm, lam = _s5_matrices(ssm_lambda_re, ssm_lambda_im, ssm_b_re, ssm_b_im, ssm_c_re, ssm_c_im,
                               ssm_log_step)
    pw = {'g': norm_g.reshape(DEPTH, 3, 1, D_MODEL),
          'ffn_f32': (ffn_w1, ffn_w3, ffn_w2),
          'ffn': {(0, 0): tuple(w[0, 0].astype(BF16) for w in (ffn_w1, ffn_w3, ffn_w2))},
          'w_in': w_in.astype(BF16), 'w_out': w_out.astype(BF16),
          'qg': jnp.tile(q_norm_g, (1, N_HEADS)).reshape(DEPTH, 1, ATTN_WIDTH),
          'kg': jnp.tile(k_norm_g, (1, N_KV_HEADS)).reshape(DEPTH, 1, KV_WIDTH),
          'sink': jnp.broadcast_to(attn_sink[:, :, None], (DEPTH, N_HEADS, LANES)),
          'w_fnet': w_fnet.astype(BF16), 'bm': bm, 'cm': cm, 'lam': lam,
          'd': ssm_d.reshape(DEPTH, 1, SSM_WIDTH), 'w_glu': ssm_w_glu.astype(BF16)}

    cond = jnp.zeros((COND_ROWS, D_MODEL), F32).at[0].set(c_ctx).at[1:1 + dec_batch].set(c)
    mods = _mod_call(cond, w_mod, b_mod).reshape(DEPTH, COND_ROWS, N_MOD, D_MODEL)

    ctx_kv = (cache_k.reshape(dec_batch, DEPTH, past, KV_WIDTH), cache_v.reshape(dec_batch, DEPTH, past, KV_WIDTH))
    h0_lat = jnp.concatenate([state_ssm_re.reshape(dec_batch, DEPTH, 2, SSM_LANES),
                              state_ssm_im.reshape(dec_batch, DEPTH, 2, SSM_LANES)], axis=-1)
    h0_lat = h0_lat.reshape(dec_batch // SUBLANES, SUBLANES, DEPTH, 2, 2 * SSM_LANES).transpose(2, 0, 3, 1, 4)
    h0_ctx = jnp.zeros((batch // SUBLANES, 2, SUBLANES, 2 * SSM_LANES), F32)

    yp = x_prompt.reshape(batch * seq, D_MODEL)
    ys = x_sample.reshape(dec_batch * dec_seq, D_MODEL)
    ks, vs, sre, sim = [], [], [], []
    for l in range(DEPTH):
        yp, (k_l, v_l, fin) = _stream_layer(yp, mods[l, 0:1], pw, l, batch, seq, batch * seq, tables,
                                            None, h0_ctx)
        ks.append(k_l.reshape(batch, seq, N_KV_HEADS, HEAD_DIM))
        vs.append(v_l.reshape(batch, seq, N_KV_HEADS, HEAD_DIM))
        fin = fin.reshape(batch // SUBLANES, 2, SUBLANES, 2, SSM_GROUPS, SSM_STATE)
        fin = fin.transpose(3, 0, 2, 1, 4, 5).reshape(2, batch, 2, SSM_GROUPS, SSM_STATE)
        sre.append(fin[0])
        sim.append(fin[1])
        ys, _ = _stream_layer(ys, mods[l, 1:1 + dec_batch], pw, l, dec_batch, dec_seq, dec_seq, tables,
                              ctx_kv, h0_lat[l])

    return (yp.reshape(batch, seq, D_MODEL), ys.reshape(dec_batch, dec_seq, D_MODEL),
            jnp.stack(ks, axis=1), jnp.stack(vs, axis=1), jnp.stack(sre, axis=1), jnp.stack(sim, axis=1))
```
